```python
import jax, jax.numpy as jnp
from jax import lax
import numpy as np

D_MODEL = 2048
BATCH = 4
SEQ = 4096
DEPTH = 4
DEC_BATCH = 32
DEC_SEQ = 64
PAST_LEN = 1024

CHUNK = 64
GMLP_CHUNK = 128
GMLP_GROUPS = 12
GMLP_GROUP_DIM = 128
GMLP_WIDTH = GMLP_GROUPS * GMLP_GROUP_DIM
SWA_HEADS = 24
SWA_KV_HEADS = 4
SWA_HEAD_DIM = 64
SWA_GROUP = SWA_HEADS // SWA_KV_HEADS
SWA_WIDTH = SWA_HEADS * SWA_HEAD_DIM
SWA_KV_WIDTH = SWA_KV_HEADS * SWA_HEAD_DIM
WINDOW = 128
WINDOW_CHUNKS = WINDOW // CHUNK
BAND = (WINDOW_CHUNKS + 1) * CHUNK
ROPE_THETA = 500000.0
ROPE_DIM = SWA_HEAD_DIM // 4
MEM_TOKENS = 256
MEM_HEADS = 4
MEM_HEAD_DIM = 256
MEM_WIDTH = MEM_HEADS * MEM_HEAD_DIM
N_BRANCHES = 3
FFN_HIDDEN = 4 * D_MODEL
IN_WIDTH = 2 * GMLP_WIDTH + SWA_WIDTH + 2 * SWA_KV_WIDTH + MEM_WIDTH + N_BRANCHES * D_MODEL
EPS = 1e-6
NEG = -1e30

kernel_name = "hybrid_streaming_gmlp_swa_memory_step"


def rmsnorm(x, g):
    xf = x.astype(jnp.float32)
    y = xf * lax.rsqrt(jnp.mean(xf * xf, axis=-1, keepdims=True) + EPS)
    return (y * g.astype(jnp.float32)).astype(x.dtype)


def layernorm(x, g, b):
    xf = x.astype(jnp.float32)
    mu = jnp.mean(xf, axis=-1, keepdims=True)
    var = jnp.mean(jnp.square(xf - mu), axis=-1, keepdims=True)
    y = (xf - mu) * lax.rsqrt(var + EPS) * g.astype(jnp.float32) + b.astype(jnp.float32)
    return y.astype(x.dtype)


def partial_rope(x, pos):
    half = ROPE_DIM // 2
    inv = ROPE_THETA ** (-jnp.arange(half, dtype=jnp.float32) / half)
    ang = pos.astype(jnp.float32)[:, None] * inv[None, :]
    cos = jnp.cos(ang)[:, None, :]
    sin = jnp.sin(ang)[:, None, :]
    xr = x[..., :ROPE_DIM].astype(jnp.float32)
    x1, x2 = xr[..., :half], xr[..., half:]
    rot = jnp.concatenate([x1 * cos - x2 * sin, x2 * cos + x1 * sin], axis=-1).astype(x.dtype)
    return jnp.concatenate([rot, x[..., ROPE_DIM:]], axis=-1)


def split_in(z):
    sizes = (GMLP_WIDTH, GMLP_WIDTH, SWA_WIDTH, SWA_KV_WIDTH, SWA_KV_WIDTH, MEM_WIDTH)
    idx = [int(i) for i in np.cumsum(sizes)]
    return jnp.split(z, idx, axis=-1)


def gmlp_spatial(vc, w_s, b_s):
    L = vc.shape[2]
    w = jnp.tril(w_s[:, :L, :L]).astype(vc.dtype)
    bias = jnp.transpose(b_s[:, :L])[None, None, :, :, None].astype(vc.dtype)
    return jnp.einsum('gij,bcjgd->bcigd', w, vc) + bias


def mixer_projections(h, pos, chunk_len, w_in, ln_g, ln_b, w_s, b_s):
    B, S, _ = h.shape
    a_u, a_v, bq, bk, bv, cq, gate_logits = split_in(h @ w_in)
    u = jax.nn.gelu(a_u)
    v_rows = layernorm(jax.nn.gelu(a_v), ln_g, ln_b)
    vc = v_rows.reshape(B, S // chunk_len, chunk_len, GMLP_GROUPS, GMLP_GROUP_DIM)
    o_a = u * gmlp_spatial(vc, w_s, b_s).reshape(B, S, GMLP_WIDTH)
    q = partial_rope(bq.reshape(B, S, SWA_HEADS, SWA_HEAD_DIM), pos)
    k = partial_rope(bk.reshape(B, S, SWA_KV_HEADS, SWA_HEAD_DIM), pos)
    v = bv.reshape(B, S, SWA_KV_HEADS, SWA_HEAD_DIM)
    cq = cq.reshape(B, S, MEM_HEADS, MEM_HEAD_DIM)
    gates = jax.nn.sigmoid(gate_logits.astype(jnp.float32)).astype(h.dtype)
    gates = gates.reshape(B, S, N_BRANCHES, D_MODEL)
    return o_a, v_rows, q, k, v, cq, gates


def sink_softmax(scores, sink):
    s = jnp.broadcast_to(sink.astype(jnp.float32).reshape(SWA_KV_HEADS, SWA_GROUP, 1, 1),
                         scores.shape[:-1] + (1,))
    p = jax.nn.softmax(jnp.concatenate([scores, s], axis=-1), axis=-1)
    return p[..., :-1]


def swa_prompt(q, k, v, sink):
    B, S = q.shape[:2]
    n_c = S // CHUNK
    pad = WINDOW_CHUNKS * CHUNK
    kp = jnp.pad(k, ((0, 0), (pad, 0), (0, 0), (0, 0))).reshape(B, n_c + WINDOW_CHUNKS, CHUNK, SWA_KV_HEADS, SWA_HEAD_DIM)
    vp = jnp.pad(v, ((0, 0), (pad, 0), (0, 0), (0, 0))).reshape(B, n_c + WINDOW_CHUNKS, CHUNK, SWA_KV_HEADS, SWA_HEAD_DIM)
    band_k = jnp.concatenate([kp[:, j:j + n_c] for j in range(WINDOW_CHUNKS + 1)], axis=2)
    band_v = jnp.concatenate([vp[:, j:j + n_c] for j in range(WINDOW_CHUNKS + 1)], axis=2)
    qc = q.reshape(B, n_c, CHUNK, SWA_KV_HEADS, SWA_GROUP, SWA_HEAD_DIM)
    scores = jnp.einsum('bcqhgd,bckhd->bchgqk', qc, band_k).astype(jnp.float32) * (SWA_HEAD_DIM ** -0.5)
    kpos = (jnp.arange(n_c)[:, None] - WINDOW_CHUNKS) * CHUNK + jnp.arange(BAND)[None, :]
    valid = (kpos >= 0)[None, :, None, None, None, :]
    scores = jnp.where(valid, scores, NEG)
    p = sink_softmax(scores, sink).astype(v.dtype)
    o = jnp.einsum('bchgqk,bckhd->bcqhgd', p, band_v)
    return o.reshape(B, S, SWA_WIDTH)


def swa_sample(q, k, v, cache_k, cache_v, sink):
    B, S = q.shape[:2]
    k_all = jnp.concatenate([cache_k.astype(k.dtype), k], axis=1)
    v_all = jnp.concatenate([cache_v.astype(v.dtype), v], axis=1)
    qg = q.reshape(B, S, SWA_KV_HEADS, SWA_GROUP, SWA_HEAD_DIM)
    scores = jnp.einsum('bqhgd,bkhd->bhgqk', qg, k_all).astype(jnp.float32) * (SWA_HEAD_DIM ** -0.5)
    p = sink_softmax(scores, sink).astype(v.dtype)
    o = jnp.einsum('bhgqk,bkhd->bqhgd', p, v_all)
    return o.reshape(B, S, SWA_WIDTH)


def memory_kv(mem, g, w_mk, w_mv):
    B = mem.shape[0]
    m = rmsnorm(mem, g)
    k = (m @ w_mk).reshape(B, MEM_TOKENS, MEM_HEADS, MEM_HEAD_DIM)
    v = (m @ w_mv).reshape(B, MEM_TOKENS, MEM_HEADS, MEM_HEAD_DIM)
    return k, v


def memory_attend(cq, mk, mv):
    B, S = cq.shape[:2]
    scores = jnp.einsum('bshd,bmhd->bhsm', cq, mk.astype(cq.dtype)).astype(jnp.float32) * (MEM_HEAD_DIM ** -0.5)
    p = jax.nn.softmax(scores, axis=-1).astype(cq.dtype)
    o = jnp.einsum('bhsm,bmhd->bshd', p, mv.astype(cq.dtype))
    return o.reshape(B, S, MEM_WIDTH)


def merge_branches(o_a, o_b, o_c, gates, w_pa, w_pb, w_pc, w_o):
    merged = (gates[..., 0, :] * (o_a @ w_pa)
              + gates[..., 1, :] * (o_b @ w_pb)
              + gates[..., 2, :] * (o_c @ w_pc))
    return merged @ w_o


def ffn_sublayer(x, g_pre, g_post, w_up, w_down):
    h = rmsnorm(x, g_pre)
    return x + rmsnorm(jnp.square(jax.nn.relu(h @ w_up)) @ w_down, g_post)


def setup_inputs(seed: int = 0) -> dict:
    key = jax.random.key(seed)
    ks = jax.random.split(key, 26)

    def nrm(k, shape, scale=1.0):
        return jax.random.normal(k, shape, jnp.float32) * scale

    return {
        "x_prompt": nrm(ks[0], (BATCH, SEQ, D_MODEL)),
        "x_sample": nrm(ks[1], (DEC_BATCH, DEC_SEQ, D_MODEL)),
        "cache_swa_k": nrm(ks[2], (DEPTH, DEC_BATCH, WINDOW, SWA_KV_HEADS, SWA_HEAD_DIM)),
        "cache_swa_v": nrm(ks[3], (DEPTH, DEC_BATCH, WINDOW, SWA_KV_HEADS, SWA_HEAD_DIM)),
        "cache_mem_k": nrm(ks[4], (DEPTH, DEC_BATCH, MEM_TOKENS, MEM_HEADS, MEM_HEAD_DIM)),
        "cache_mem_v": nrm(ks[5], (DEPTH, DEC_BATCH, MEM_TOKENS, MEM_HEADS, MEM_HEAD_DIM)),
        "mem_prompt": nrm(ks[6], (BATCH, MEM_TOKENS, D_MODEL)),
        "w_in": nrm(ks[7], (DEPTH, D_MODEL, IN_WIDTH), D_MODEL ** -0.5),
        "ln_v_g": 1.0 + nrm(ks[8], (DEPTH, GMLP_WIDTH), 0.02),
        "ln_v_b": nrm(ks[9], (DEPTH, GMLP_WIDTH), 0.02),
        "w_s": nrm(ks[10], (DEPTH, GMLP_GROUPS, GMLP_CHUNK, GMLP_CHUNK), GMLP_CHUNK ** -0.5),
        "b_s": 1.0 + nrm(ks[11], (DEPTH, GMLP_GROUPS, GMLP_CHUNK), 0.1),
        "sinks": nrm(ks[12], (DEPTH, SWA_HEADS), 0.5),
        "mem_norm": 1.0 + nrm(ks[13], (DEPTH, D_MODEL), 0.02),
        "w_mem_k": nrm(ks[14], (DEPTH, D_MODEL, MEM_WIDTH), D_MODEL ** -0.5),
        "w_mem_v": nrm(ks[15], (DEPTH, D_MODEL, MEM_WIDTH), D_MODEL ** -0.5),
        "w_pa": nrm(ks[16], (DEPTH, GMLP_WIDTH, D_MODEL), GMLP_WIDTH ** -0.5),
        "w_pb": nrm(ks[17], (DEPTH, SWA_WIDTH, D_MODEL), SWA_WIDTH ** -0.5),
        "w_pc": nrm(ks[18], (DEPTH, MEM_WIDTH, D_MODEL), MEM_WIDTH ** -0.5),
        "w_o": nrm(ks[19], (DEPTH, D_MODEL, D_MODEL), D_MODEL ** -0.5),
        "norm_mix_pre": 1.0 + nrm(ks[20], (DEPTH, D_MODEL), 0.02),
        "norm_mix_post": 1.0 + nrm(ks[21], (DEPTH, D_MODEL), 0.02),
        "norm_ffn_pre": 1.0 + nrm(ks[22], (DEPTH, D_MODEL), 0.02),
        "norm_ffn_post": 1.0 + nrm(ks[23], (DEPTH, D_MODEL), 0.02),
        "w_up": nrm(ks[24], (DEPTH, D_MODEL, FFN_HIDDEN), D_MODEL ** -0.5),
        "w_down": nrm(ks[25], (DEPTH, FFN_HIDDEN, D_MODEL), FFN_HIDDEN ** -0.5),
    }


def reference(x_prompt, x_sample, cache_swa_k, cache_swa_v, cache_mem_k, cache_mem_v, mem_prompt,
              w_in, ln_v_g, ln_v_b, w_s, b_s, sinks, mem_norm, w_mem_k, w_mem_v,
              w_pa, w_pb, w_pc, w_o, norm_mix_pre, norm_mix_post, norm_ffn_pre, norm_ffn_post,
              w_up, w_down):
    seq_p = x_prompt.shape[1]
    seq_s = x_sample.shape[1]
    pos_p = jnp.arange(seq_p, dtype=jnp.int32)
    pos_s = PAST_LEN + jnp.arange(seq_s, dtype=jnp.int32)
    yp, ys = x_prompt, x_sample
    k_p, v_p, mk_p, mv_p, k_s, v_s, gv_s = [], [], [], [], [], [], []
    for l in range(DEPTH):
        h = rmsnorm(yp, norm_mix_pre[l])
        o_a, _, q, k, v, cq, gates = mixer_projections(h, pos_p, GMLP_CHUNK, w_in[l], ln_v_g[l], ln_v_b[l], w_s[l], b_s[l])
        o_b = swa_prompt(q, k, v, sinks[l])
        mk, mv = memory_kv(mem_prompt, mem_norm[l], w_mem_k[l], w_mem_v[l])
        o_c = memory_attend(cq, mk, mv)
        mix = merge_branches(o_a, o_b, o_c, gates, w_pa[l], w_pb[l], w_pc[l], w_o[l])
        yp = yp + rmsnorm(mix, norm_mix_post[l])
        yp = ffn_sublayer(yp, norm_ffn_pre[l], norm_ffn_post[l], w_up[l], w_down[l])
        k_p.append(k[:, -WINDOW:])
        v_p.append(v[:, -WINDOW:])
        mk_p.append(mk)
        mv_p.append(mv)
        hs = rmsnorm(ys, norm_mix_pre[l])
        o_a, v_rows, q, k, v, cq, gates = mixer_projections(hs, pos_s, seq_s, w_in[l], ln_v_g[l], ln_v_b[l], w_s[l], b_s[l])
        o_b = swa_sample(q, k, v, cache_swa_k[l], cache_swa_v[l], sinks[l])
        o_c = memory_attend(cq, cache_mem_k[l], cache_mem_v[l])
        mix = merge_branches(o_a, o_b, o_c, gates, w_pa[l], w_pb[l], w_pc[l], w_o[l])
        ys = ys + rmsnorm(mix, norm_mix_post[l])
        ys = ffn_sublayer(ys, norm_ffn_pre[l], norm_ffn_post[l], w_up[l], w_down[l])
        k_s.append(k)
        v_s.append(v)
        gv_s.append(v_rows)
    new_swa_k_prompt = jnp.stack(k_p)
    new_swa_v_prompt = jnp.stack(v_p)
    new_mem_k_prompt = jnp.stack(mk_p)
    new_mem_v_prompt = jnp.stack(mv_p)
    new_swa_k_sample = jnp.stack(k_s)
    new_swa_v_sample = jnp.stack(v_s)
    new_gmlp_v_sample = jnp.stack(gv_s)
    return (yp, ys, new_swa_k_prompt, new_swa_v_prompt, new_mem_k_prompt, new_mem_v_prompt,
            new_swa_k_sample, new_swa_v_sample, new_gmlp_v_sample)
```

```python
import functools

import jax
import jax.numpy as jnp
from jax import lax
from jax.experimental import pallas as pl
from jax.experimental.pallas import tpu as pltpu

F32 = jnp.float32
BF16 = jnp.bfloat16

CHUNK = 64
GMLP_CHUNK = 128
GMLP_GROUPS = 12
GMLP_GROUP_DIM = 128
GMLP_WIDTH = GMLP_GROUPS * GMLP_GROUP_DIM
SWA_HEADS = 24
SWA_KV_HEADS = 4
SWA_HEAD_DIM = 64
SWA_WIDTH = SWA_HEADS * SWA_HEAD_DIM
SWA_KV_WIDTH = SWA_KV_HEADS * SWA_HEAD_DIM
WINDOW = 128
ROPE_THETA = 500000.0
ROPE_DIM = SWA_HEAD_DIM // 4
MEM_HEADS = 4
MEM_HEAD_DIM = 256
MEM_WIDTH = MEM_HEADS * MEM_HEAD_DIM
PAST_LEN = 1024
EPS = 1e-6
NEG = -1e30

COL_U = 0
COL_V = GMLP_WIDTH
COL_Q = 2 * GMLP_WIDTH
COL_K = COL_Q + SWA_WIDTH
COL_VV = COL_K + SWA_KV_WIDTH
COL_CQ = COL_VV + SWA_KV_WIDTH
COL_GATE = COL_CQ + MEM_WIDTH

LANES = 128
HEADS_PER_LANE_TILE = LANES // SWA_HEAD_DIM
VMEM_LIMIT = 56 * 1024 * 1024


def _params(*sem):
    return pltpu.CompilerParams(dimension_semantics=sem, vmem_limit_bytes=VMEM_LIMIT)


def _rms(x, g):
    return x * lax.rsqrt(jnp.mean(x * x, axis=-1, keepdims=True) + EPS) * g


def _rmsnorm_kernel(x_ref, g_ref, o_ref):
    o_ref[...] = _rms(x_ref[...], g_ref[...]).astype(o_ref.dtype)


def rmsnorm_cast(x, g_all, layer, tm=512):
    m, d = x.shape
    return pl.pallas_call(
        _rmsnorm_kernel,
        grid=(m // tm,),
        in_specs=[pl.BlockSpec((tm, d), lambda i: (i, 0)),
                  pl.BlockSpec((None, 1, d), lambda i: (layer, 0, 0))],
        out_specs=pl.BlockSpec((tm, d), lambda i: (i, 0)),
        out_shape=jax.ShapeDtypeStruct((m, d), BF16),
        compiler_params=_params("parallel"),
        name="rmsnorm_cast",
    )(x, g_all)


def _proj_in_kernel(h_ref, w_ref, o_ref, *, n_gelu, n_lin):
    j = pl.program_id(1)
    acc = jnp.dot(h_ref[...], w_ref[...], preferred_element_type=F32)

    @pl.when(j < n_gelu)
    def _():
        o_ref[...] = jax.nn.gelu(acc).astype(o_ref.dtype)

    @pl.when(jnp.logical_and(j >= n_gelu, j < n_gelu + n_lin))
    def _():
        o_ref[...] = acc.astype(o_ref.dtype)

    @pl.when(j >= n_gelu + n_lin)
    def _():
        o_ref[...] = jax.nn.sigmoid(acc).astype(o_ref.dtype)


def proj_in(h, w_all, layer, tm=1024, tn=1024):
    m, d = h.shape
    n = w_all.shape[-1]
    kern = functools.partial(_proj_in_kernel, n_gelu=COL_Q // tn, n_lin=(COL_GATE - COL_Q) // tn)
    return pl.pallas_call(
        kern,
        grid=(m // tm, n // tn),
        in_specs=[pl.BlockSpec((tm, d), lambda i, j: (i, 0)),
                  pl.BlockSpec((None, d, tn), lambda i, j: (layer, 0, j))],
        out_specs=pl.BlockSpec((tm, tn), lambda i, j: (i, j)),
        out_shape=jax.ShapeDtypeStruct((m, n), BF16),
        compiler_params=_params("parallel", "arbitrary"),
        name="proj_in",
    )(h, w_all)


def _gmlp_kernel(u_ref, gv_ref, lng_ref, lnb_ref, ws_ref, bias_ref, o_ref, *v_refs, chunk):
    rows = u_ref.shape[0]
    n_chunks = rows // chunk
    gv = gv_ref[...].astype(F32)
    mu = jnp.mean(gv, axis=-1, keepdims=True)
    dev = gv - mu
    var = jnp.mean(dev * dev, axis=-1, keepdims=True)
    v = dev * lax.rsqrt(var + EPS) * lng_ref[...] + lnb_ref[...]
    if v_refs:
        v_refs[0][...] = v
    vb = v.astype(BF16)
    r_io = lax.broadcasted_iota(jnp.int32, (chunk, chunk), 0)
    c_io = lax.broadcasted_iota(jnp.int32, (chunk, chunk), 1)
    causal = c_io <= r_io
    for g in range(GMLP_GROUPS):
        cols = slice(g * GMLP_GROUP_DIM, (g + 1) * GMLP_GROUP_DIM)
        w = jnp.where(causal, ws_ref[g, :chunk, :chunk], 0.0).astype(BF16)
        vg = jnp.concatenate([vb[c * chunk:(c + 1) * chunk, cols] for c in range(n_chunks)], axis=1)
        s = jnp.dot(w, vg, preferred_element_type=F32)
        bias = bias_ref[:, cols]
        for c in range(n_chunks):
            rs = slice(c * chunk, (c + 1) * chunk)
            sc = s[:, c * GMLP_GROUP_DIM:(c + 1) * GMLP_GROUP_DIM] + bias
            o_ref[rs, cols] = (u_ref[rs, cols].astype(F32) * sc).astype(o_ref.dtype)


def gmlp(z, ln_g, ln_b, w_s, bias_tab, layer, chunk, emit_v, rows=512):
    m = z.shape[0]
    wdt = GMLP_WIDTH
    out_shape = [jax.ShapeDtypeStruct((m, wdt), BF16)]
    out_specs = [pl.BlockSpec((rows, wdt), lambda i: (i, 0))]
    if emit_v:
        out_shape.append(jax.ShapeDtypeStruct((m, wdt), F32))
        out_specs.append(pl.BlockSpec((rows, wdt), lambda i: (i, 0)))
    res = pl.pallas_call(
        functools.partial(_gmlp_kernel, chunk=chunk),
        grid=(m // rows,),
        in_specs=[pl.BlockSpec((rows, wdt), lambda i: (i, COL_U // wdt)),
                  pl.BlockSpec((rows, wdt), lambda i: (i, COL_V // wdt)),
                  pl.BlockSpec((None, 1, wdt), lambda i: (layer, 0, 0)),
                  pl.BlockSpec((None, 1, wdt), lambda i: (layer, 0, 0)),
                  pl.BlockSpec((None, GMLP_GROUPS, GMLP_CHUNK, GMLP_CHUNK), lambda i: (layer, 0, 0, 0)),
                  pl.BlockSpec((None, chunk, wdt), lambda i: (layer, 0, 0))],
        out_specs=out_specs,
        out_shape=out_shape,
        compiler_params=_params("parallel"),
        name="gmlp_c%d" % chunk,
    )(z, z, ln_g, ln_b, w_s, bias_tab)
    return res if emit_v else (res[0], None)


def _rope(x, tab):
    return x * tab[0] + pltpu.roll(x, LANES - ROPE_DIM // 2, 1) * tab[1] + pltpu.roll(x, ROPE_DIM // 2, 1) * tab[2]


def _rope_kv_width(k_ref, tab):
    parts = [_rope(k_ref[:, t * LANES:(t + 1) * LANES].astype(F32), tab)
             for t in range(SWA_KV_WIDTH // LANES)]
    return jnp.concatenate(parts, axis=1)


def _swa_core(q_ref, tab_q, k_rot, v_all, valid, sink_ref, o_ref):
    tq = q_ref.shape[0]
    nk = k_rot.shape[0]
    group = SWA_HEADS // SWA_KV_HEADS
    tiles_per_group = group // HEADS_PER_LANE_TILE
    scale = SWA_HEAD_DIM ** -0.5
    n_tiles = SWA_WIDTH // LANES
    q_rot = [(_rope(q_ref[:, t * LANES:(t + 1) * LANES].astype(F32), tab_q) * scale).astype(BF16)
             for t in range(n_tiles)]
    low_half = lax.broadcasted_iota(jnp.int32, (nk, LANES), 1) < SWA_HEAD_DIM
    for kvh in range(SWA_KV_HEADS):
        c0 = (kvh // HEADS_PER_LANE_TILE) * LANES
        kk = k_rot[:, c0:c0 + LANES]
        vv = v_all[:, c0:c0 + LANES]
        kk_sw = pltpu.roll(kk, SWA_HEAD_DIM, 1)
        vv_sw = pltpu.roll(vv, SWA_HEAD_DIM, 1)
        if kvh % HEADS_PER_LANE_TILE == 0:
            k_ext = [jnp.where(low_half, kk, 0.0), jnp.where(low_half, 0.0, kk_sw)]
            v_ext = [jnp.where(low_half, vv, 0.0), jnp.where(low_half, 0.0, vv_sw)]
        else:
            k_ext = [jnp.where(low_half, kk_sw, 0.0), jnp.where(low_half, 0.0, kk)]
            v_ext = [jnp.where(low_half, vv_sw, 0.0), jnp.where(low_half, 0.0, vv)]
        t0 = kvh * tiles_per_group
        q_stack = jnp.concatenate(q_rot[t0:t0 + tiles_per_group], axis=0)
        acc = None
        for e in range(HEADS_PER_LANE_TILE):
            s = lax.dot_general(q_stack, k_ext[e].astype(BF16), (((1,), (1,)), ((), ())),
                                preferred_element_type=F32)
            probs = []
            for t in range(tiles_per_group):
                sink = sink_ref[HEADS_PER_LANE_TILE * (t0 + t) + e]
                st = jnp.where(valid, s[t * tq:(t + 1) * tq], NEG)
                mx = jnp.maximum(jnp.max(st, axis=-1, keepdims=True), sink)
                p = jnp.exp(st - mx)
                den = jnp.sum(p, axis=-1, keepdims=True) + jnp.exp(sink - mx)
                probs.append((p * (1.0 / den)).astype(BF16))
            pv = jnp.dot(jnp.concatenate(probs, axis=0), v_ext[e].astype(BF16),
                         preferred_element_type=F32)
            acc = pv if acc is None else acc + pv
        for t in range(tiles_per_group):
            o_ref[:, (t0 + t) * LANES:(t0 + t + 1) * LANES] = acc[t * tq:(t + 1) * tq].astype(o_ref.dtype)


def _swa_prompt_kernel(sink_ref, q_ref, kc_ref, kp_ref, vc_ref, vp_ref, tabc_ref, tabp_ref,
                       o_ref, krot_ref):
    tq = q_ref.shape[0]
    tab_c = tabc_ref[...]
    k_cur = _rope_kv_width(kc_ref, tab_c)
    krot_ref[...] = k_cur
    k_rot = jnp.concatenate([_rope_kv_width(kp_ref, tabp_ref[...]), k_cur], axis=0)
    v_all = jnp.concatenate([vp_ref[...], vc_ref[...]], axis=0).astype(F32)
    nk = k_rot.shape[0]
    q_chunk = lax.broadcasted_iota(jnp.int32, (tq, nk), 0) // CHUNK
    k_col = lax.broadcasted_iota(jnp.int32, (tq, nk), 1)
    k_chunk = k_col // CHUNK
    first_valid = jnp.where(pl.program_id(1) == 0, WINDOW, 0)
    valid = (k_chunk >= q_chunk) & (k_chunk <= q_chunk + WINDOW // CHUNK) & (k_col >= first_valid)
    _swa_core(q_ref, tab_c, k_rot, v_all, valid, sink_ref, o_ref)


def swa_prompt(z, sinks, tab, layer, batch, seq):
    tq = WINDOW
    nqb = seq // tq
    m = z.shape[0]

    def prev(i):
        return jnp.maximum(i - 1, 0)

    return pl.pallas_call(
        _swa_prompt_kernel,
        grid=(batch, nqb),
        in_specs=[pl.BlockSpec(memory_space=pltpu.SMEM),
                  pl.BlockSpec((tq, SWA_WIDTH), lambda b, i: (b * nqb + i, COL_Q // SWA_WIDTH)),
                  pl.BlockSpec((tq, SWA_KV_WIDTH), lambda b, i: (b * nqb + i, COL_K // SWA_KV_WIDTH)),
                  pl.BlockSpec((tq, SWA_KV_WIDTH), lambda b, i: (b * nqb + prev(i), COL_K // SWA_KV_WIDTH)),
                  pl.BlockSpec((tq, SWA_KV_WIDTH), lambda b, i: (b * nqb + i, COL_VV // SWA_KV_WIDTH)),
                  pl.BlockSpec((tq, SWA_KV_WIDTH), lambda b, i: (b * nqb + prev(i), COL_VV // SWA_KV_WIDTH)),
                  pl.BlockSpec((3, tq, LANES), lambda b, i: (0, i, 0)),
                  pl.BlockSpec((3, tq, LANES), lambda b, i: (0, prev(i), 0))],
        out_specs=[pl.BlockSpec((tq, SWA_WIDTH), lambda b, i: (b * nqb + i, 0)),
                   pl.BlockSpec((tq, SWA_KV_WIDTH), lambda b, i: (b * nqb + i, 0))],
        out_shape=[jax.ShapeDtypeStruct((m, SWA_WIDTH), BF16),
                   jax.ShapeDtypeStruct((m, SWA_KV_WIDTH), F32)],
        compiler_params=_params("parallel", "arbitrary"),
        name="swa_prompt",
    )(sinks[layer], z, z, z, z, z, tab, tab)


def _swa_sample_kernel(sink_ref, q_ref, kn_ref, vn_ref, ck_ref, cv_ref, tab_ref, o_ref, krot_ref):
    tq = q_ref.shape[0]
    tab = tab_ref[...]
    k_new = _rope_kv_width(kn_ref, tab)
    krot_ref[...] = k_new
    pad = jnp.zeros((2 * WINDOW - WINDOW - tq, SWA_KV_WIDTH), F32)
    k_rot = jnp.concatenate([ck_ref[...], k_new, pad], axis=0)
    v_all = jnp.concatenate([cv_ref[...], vn_ref[...].astype(F32), pad], axis=0)
    nk = k_rot.shape[0]
    valid = lax.broadcasted_iota(jnp.int32, (tq, nk), 1) < WINDOW + tq
    _swa_core(q_ref, tab, k_rot, v_all, valid, sink_ref, o_ref)


def swa_sample(z, cache_k, cache_v, sinks, tab, layer, batch, seq):
    m = z.shape[0]
    return pl.pallas_call(
        _swa_sample_kernel,
        grid=(batch,),
        in_specs=[pl.BlockSpec(memory_space=pltpu.SMEM),
                  pl.BlockSpec((seq, SWA_WIDTH), lambda b: (b, COL_Q // SWA_WIDTH)),
                  pl.BlockSpec((seq, SWA_KV_WIDTH), lambda b: (b, COL_K // SWA_KV_WIDTH)),
                  pl.BlockSpec((seq, SWA_KV_WIDTH), lambda b: (b, COL_VV // SWA_KV_WIDTH)),
                  pl.BlockSpec((None, None, WINDOW, SWA_KV_WIDTH), lambda b: (layer, b, 0, 0)),
                  pl.BlockSpec((None, None, WINDOW, SWA_KV_WIDTH), lambda b: (layer, b, 0, 0)),
                  pl.BlockSpec((3, seq, LANES), lambda b: (0, 0, 0))],
        out_specs=[pl.BlockSpec((seq, SWA_WIDTH), lambda b: (b, 0)),
                   pl.BlockSpec((seq, SWA_KV_WIDTH), lambda b: (b, 0))],
        out_shape=[jax.ShapeDtypeStruct((m, SWA_WIDTH), BF16),
                   jax.ShapeDtypeStruct((m, SWA_KV_WIDTH), F32)],
        compiler_params=_params("parallel"),
        name="swa_sample",
    )(sinks[layer], z, z, z, cache_k, cache_v, tab)


def rope_table(pos):
    half = ROPE_DIM // 2
    inv = ROPE_THETA ** (-jnp.arange(half, dtype=F32) / half)
    ang = pos.astype(F32)[:, None] * inv[None, :]
    cos, sin = jnp.cos(ang), jnp.sin(ang)
    s = pos.shape[0]
    rest = SWA_HEAD_DIM - ROPE_DIM
    c_head = jnp.concatenate([cos, cos, jnp.ones((s, rest), F32)], axis=1)
    up_head = jnp.concatenate([-sin, jnp.zeros((s, half + rest), F32)], axis=1)
    dn_head = jnp.concatenate([jnp.zeros((s, half), F32), sin, jnp.zeros((s, rest), F32)], axis=1)
    return jnp.stack([jnp.tile(t, (1, HEADS_PER_LANE_TILE)) for t in (c_head, up_head, dn_head)])


def _memory_kv_kernel(mem_ref, g_ref, wk_ref, wv_ref, k_ref, v_ref):
    h = _rms(mem_ref[...], g_ref[...]).astype(BF16)
    k_ref[...] = jnp.dot(h, wk_ref[...], preferred_element_type=F32)
    v_ref[...] = jnp.dot(h, wv_ref[...], preferred_element_type=F32)


def memory_kv(mem, g_all, wk_all, wv_all):
    batch, tokens, d = mem.shape
    depth = wk_all.shape[0]
    shp = jax.ShapeDtypeStruct((depth, batch, tokens, MEM_WIDTH), F32)
    return pl.pallas_call(
        _memory_kv_kernel,
        grid=(depth, batch),
        in_specs=[pl.BlockSpec((None, tokens, d), lambda l, b: (b, 0, 0)),
                  pl.BlockSpec((None, 1, d), lambda l, b: (l, 0, 0)),
                  pl.BlockSpec((None, d, MEM_WIDTH), lambda l, b: (l, 0, 0)),
                  pl.BlockSpec((None, d, MEM_WIDTH), lambda l, b: (l, 0, 0))],
        out_specs=[pl.BlockSpec((None, None, tokens, MEM_WIDTH), lambda l, b: (l, b, 0, 0)),
                   pl.BlockSpec((None, None, tokens, MEM_WIDTH), lambda l, b: (l, b, 0, 0))],
        out_shape=[shp, shp],
        compiler_params=_params("parallel", "arbitrary"),
        name="memory_kv",
    )(mem, g_all, wk_all, wv_all)


def _mem_attn_kernel(cq_ref, mk_ref, mv_ref, o_ref):
    scale = MEM_HEAD_DIM ** -0.5
    for h in range(MEM_HEADS):
        cols = slice(h * MEM_HEAD_DIM, (h + 1) * MEM_HEAD_DIM)
        s = lax.dot_general(cq_ref[:, cols], mk_ref[:, cols].astype(BF16), (((1,), (1,)), ((), ())),
                            preferred_element_type=F32) * scale
        mx = jnp.max(s, axis=-1, keepdims=True)
        p = jnp.exp(s - mx)
        den = jnp.sum(p, axis=-1, keepdims=True)
        p = (p * (1.0 / den)).astype(BF16)
        o_ref[:, cols] = jnp.dot(p, mv_ref[:, cols].astype(BF16),
                                 preferred_element_type=F32).astype(o_ref.dtype)


def mem_attn(z, mk_all, mv_all, layer, batch, seq, tq):
    m = z.shape[0]
    nqb = seq // tq
    tokens = mk_all.shape[2]
    return pl.pallas_call(
        _mem_attn_kernel,
        grid=(batch, nqb),
        in_specs=[pl.BlockSpec((tq, MEM_WIDTH), lambda b, i: (b * nqb + i, COL_CQ // MEM_WIDTH)),
                  pl.BlockSpec((None, None, tokens, MEM_WIDTH), lambda b, i: (layer, b, 0, 0)),
                  pl.BlockSpec((None, None, tokens, MEM_WIDTH), lambda b, i: (layer, b, 0, 0))],
        out_specs=pl.BlockSpec((tq, MEM_WIDTH), lambda b, i: (b * nqb + i, 0)),
        out_shape=jax.ShapeDtypeStruct((m, MEM_WIDTH), BF16),
        compiler_params=_params("parallel", "arbitrary"),
        name="mem_attn",
    )(z, mk_all, mv_all)


def _merge_kernel(oa_ref, ob_ref, oc_ref, wa_ref, wb_ref, wc_ref, ga_ref, gb_ref, gc_ref, o_ref):
    acc = ga_ref[...].astype(F32) * jnp.dot(oa_ref[...], wa_ref[...], preferred_element_type=F32)
    acc = acc + gb_ref[...].astype(F32) * jnp.dot(ob_ref[...], wb_ref[...], preferred_element_type=F32)
    acc = acc + gc_ref[...].astype(F32) * jnp.dot(oc_ref[...], wc_ref[...], preferred_element_type=F32)
    o_ref[...] = acc.astype(o_ref.dtype)


def merge(z, oa, ob, oc, wa_all, wb_all, wc_all, layer, tm=1024, tn=512):
    m = z.shape[0]
    d = wa_all.shape[-1]
    g0 = COL_GATE // tn
    gs = d // tn
    return pl.pallas_call(
        _merge_kernel,
        grid=(m // tm, d // tn),
        in_specs=[pl.BlockSpec((tm, GMLP_WIDTH), lambda i, j: (i, 0)),
                  pl.BlockSpec((tm, SWA_WIDTH), lambda i, j: (i, 0)),
                  pl.BlockSpec((tm, MEM_WIDTH), lambda i, j: (i, 0)),
                  pl.BlockSpec((None, GMLP_WIDTH, tn), lambda i, j: (layer, 0, j)),
                  pl.BlockSpec((None, SWA_WIDTH, tn), lambda i, j: (layer, 0, j)),
                  pl.BlockSpec((None, MEM_WIDTH, tn), lambda i, j: (layer, 0, j)),
                  pl.BlockSpec((tm, tn), lambda i, j: (i, g0 + j)),
                  pl.BlockSpec((tm, tn), lambda i, j: (i, g0 + gs + j)),
                  pl.BlockSpec((tm, tn), lambda i, j: (i, g0 + 2 * gs + j))],
        out_specs=pl.BlockSpec((tm, tn), lambda i, j: (i, j)),
        out_shape=jax.ShapeDtypeStruct((m, d), BF16),
        compiler_params=_params("parallel", "arbitrary"),
        name="merge",
    )(oa, ob, oc, wa_all, wb_all, wc_all, z, z, z)


def _out_proj_kernel(mg_ref, w_ref, x_ref, gpost_ref, gnext_ref, xo_ref, ho_ref):
    mix = jnp.dot(mg_ref[...], w_ref[...], preferred_element_type=F32)
    xn = x_ref[...] + _rms(mix, gpost_ref[...])
    xo_ref[...] = xn
    ho_ref[...] = _rms(xn, gnext_ref[...]).astype(ho_ref.dtype)


def out_proj(mg, w_all, x, gpost_all, gnext_all, layer, tm=256):
    m, d = x.shape
    return pl.pallas_call(
        _out_proj_kernel,
        grid=(m // tm,),
        in_specs=[pl.BlockSpec((tm, d), lambda i: (i, 0)),
                  pl.BlockSpec((None, d, d), lambda i: (layer, 0, 0)),
                  pl.BlockSpec((tm, d), lambda i: (i, 0)),
                  pl.BlockSpec((None, 1, d), lambda i: (layer, 0, 0)),
                  pl.BlockSpec((None, 1, d), lambda i: (layer, 0, 0))],
        out_specs=[pl.BlockSpec((tm, d), lambda i: (i, 0)),
                   pl.BlockSpec((tm, d), lambda i: (i, 0))],
        out_shape=[jax.ShapeDtypeStruct((m, d), F32), jax.ShapeDtypeStruct((m, d), BF16)],
        compiler_params=_params("parallel"),
        name="out_proj",
    )(mg, w_all, x, gpost_all, gnext_all)


def _ffn_kernel(h_ref, wu_ref, wd_ref, x_ref, gpost_ref, *rest, emit_next):
    if emit_next:
        gnext_ref, xo_ref, ho_ref, acc_ref = rest
    else:
        xo_ref, acc_ref = rest
    j = pl.program_id(1)
    a = jnp.dot(h_ref[...], wu_ref[...], preferred_element_type=F32)
    a = jnp.square(jnp.maximum(a, 0.0)).astype(BF16)
    part = jnp.dot(a, wd_ref[...], preferred_element_type=F32)

    @pl.when(j == 0)
    def _():
        acc_ref[...] = part

    @pl.when(j > 0)
    def _():
        acc_ref[...] += part

    @pl.when(j == pl.num_programs(1) - 1)
    def _():
        xn = x_ref[...] + _rms(acc_ref[...], gpost_ref[...])
        xo_ref[...] = xn
        if emit_next:
            ho_ref[...] = _rms(xn, gnext_ref[...]).astype(ho_ref.dtype)


def ffn(h, wu_all, wd_all, x, gpost_all, gnext_all, layer, next_layer, tm=512, th=512):
    m, d = x.shape
    hidden = wu_all.shape[-1]
    emit_next = next_layer is not None
    in_specs = [pl.BlockSpec((tm, d), lambda i, j: (i, 0)),
                pl.BlockSpec((None, d, th), lambda i, j: (layer, 0, j)),
                pl.BlockSpec((None, th, d), lambda i, j: (layer, j, 0)),
                pl.BlockSpec((tm, d), lambda i, j: (i, 0)),
                pl.BlockSpec((None, 1, d), lambda i, j: (layer, 0, 0))]
    args = [h, wu_all, wd_all, x, gpost_all]
    out_specs = [pl.BlockSpec((tm, d), lambda i, j: (i, 0))]
    out_shape = [jax.ShapeDtypeStruct((m, d), F32)]
    if emit_next:
        in_specs.append(pl.BlockSpec((None, 1, d), lambda i, j: (next_layer, 0, 0)))
        args.append(gnext_all)
        out_specs.append(pl.BlockSpec((tm, d), lambda i, j: (i, 0)))
        out_shape.append(jax.ShapeDtypeStruct((m, d), BF16))
    res = pl.pallas_call(
        functools.partial(_ffn_kernel, emit_next=emit_next),
        grid=(m // tm, hidden // th),
        in_specs=in_specs,
        out_specs=out_specs,
        out_shape=out_shape,
        scratch_shapes=[pltpu.VMEM((tm, d), F32)],
        compiler_params=_params("parallel", "arbitrary"),
        name="ffn",
    )(*args)
    return (res[0], res[1]) if emit_next else (res[0], None)


def kernel(x_prompt, x_sample, cache_swa_k, cache_swa_v, cache_mem_k, cache_mem_v, mem_prompt,
           w_in, ln_v_g, ln_v_b, w_s, b_s, sinks, mem_norm, w_mem_k, w_mem_v,
           w_pa, w_pb, w_pc, w_o, norm_mix_pre, norm_mix_post, norm_ffn_pre, norm_ffn_post,
           w_up, w_down):
    batch, seq, d = x_prompt.shape
    dec_batch, dec_seq, _ = x_sample.shape
    depth = w_in.shape[0]
    mem_tokens = mem_prompt.shape[1]

    w_in_b, w_pa_b, w_pb_b, w_pc_b, w_o_b = (w.astype(BF16) for w in (w_in, w_pa, w_pb, w_pc, w_o))
    w_up_b, w_down_b = w_up.astype(BF16), w_down.astype(BF16)
    w_mk_b, w_mv_b = w_mem_k.astype(BF16), w_mem_v.astype(BF16)
    row = lambda g: g.reshape(depth, 1, g.shape[-1])
    g_mix_pre, g_mix_post = row(norm_mix_pre), row(norm_mix_post)
    g_ffn_pre, g_ffn_post = row(norm_ffn_pre), row(norm_ffn_post)
    g_mem, ln_g, ln_b = row(mem_norm), row(ln_v_g), row(ln_v_b)
    bias_p = jnp.repeat(jnp.swapaxes(b_s[:, :, :GMLP_CHUNK], 1, 2), GMLP_GROUP_DIM, axis=2)
    bias_s = jnp.repeat(jnp.swapaxes(b_s[:, :, :dec_seq], 1, 2), GMLP_GROUP_DIM, axis=2)
    tab_p = rope_table(jnp.arange(seq, dtype=jnp.int32))
    tab_s = rope_table(PAST_LEN + jnp.arange(dec_seq, dtype=jnp.int32))
    cache_k = cache_swa_k.reshape(depth, dec_batch, WINDOW, SWA_KV_WIDTH)
    cache_v = cache_swa_v.reshape(depth, dec_batch, WINDOW, SWA_KV_WIDTH)
    cmem_k = cache_mem_k.reshape(depth, dec_batch, mem_tokens, MEM_WIDTH)
    cmem_v = cache_mem_v.reshape(depth, dec_batch, mem_tokens, MEM_WIDTH)

    mk_p, mv_p = memory_kv(mem_prompt, g_mem, w_mk_b, w_mv_b)

    xp = x_prompt.reshape(batch * seq, d)
    xs = x_sample.reshape(dec_batch * dec_seq, d)
    hp = rmsnorm_cast(xp, g_mix_pre, 0)
    hs = rmsnorm_cast(xs, g_mix_pre, 0)

    k_p, v_p, k_s, v_s, gv_s = [], [], [], [], []
    for l in range(depth):
        nxt = l + 1 if l + 1 < depth else None

        def dense_tail(z, oa, ob, oc, x):
            mg = merge(z, oa, ob, oc, w_pa_b, w_pb_b, w_pc_b, l)
            x, h2 = out_proj(mg, w_o_b, x, g_mix_post, g_ffn_pre, l)
            return ffn(h2, w_up_b, w_down_b, x, g_ffn_post, g_mix_pre, l, nxt)

        z = proj_in(hp, w_in_b, l)
        oa, _ = gmlp(z, ln_g, ln_b, w_s, bias_p, l, GMLP_CHUNK, False)
        ob, krot = swa_prompt(z, sinks, tab_p, l, batch, seq)
        oc = mem_attn(z, mk_p, mv_p, l, batch, seq, 512)
        xp, hp = dense_tail(z, oa, ob, oc, xp)
        k_p.append(krot.reshape(batch, seq, SWA_KV_HEADS, SWA_HEAD_DIM)[:, -WINDOW:])
        v_p.append(z.reshape(batch, seq, -1)[:, -WINDOW:, COL_VV:COL_VV + SWA_KV_WIDTH]
                   .astype(F32).reshape(batch, WINDOW, SWA_KV_HEADS, SWA_HEAD_DIM))

        z = proj_in(hs, w_in_b, l)
        oa, v_rows = gmlp(z, ln_g, ln_b, w_s, bias_s, l, dec_seq, True)
        ob, krot = swa_sample(z, cache_k, cache_v, sinks, tab_s, l, dec_batch, dec_seq)
        oc = mem_attn(z, cmem_k, cmem_v, l, dec_batch, dec_seq, dec_seq)
        xs, hs = dense_tail(z, oa, ob, oc, xs)
        k_s.append(krot.reshape(dec_batch, dec_seq, SWA_KV_HEADS, SWA_HEAD_DIM))
        v_s.append(z[:, COL_VV:COL_VV + SWA_KV_WIDTH].astype(F32)
                   .reshape(dec_batch, dec_seq, SWA_KV_HEADS, SWA_HEAD_DIM))
        gv_s.append(v_rows.reshape(dec_batch, dec_seq, GMLP_WIDTH))

    mem_shape = (depth, batch, mem_tokens, MEM_HEADS, MEM_HEAD_DIM)
    return (xp.reshape(batch, seq, d), xs.reshape(dec_batch, dec_seq, d),
            jnp.stack(k_p), jnp.stack(v_p), mk_p.reshape(mem_shape), mv_p.reshape(mem_shape),
            jnp.stack(k_s), jnp.stack(v_s), jnp.stack(gv_s))
```

```python
import functools

import jax
import jax.numpy as jnp
from jax import lax
from jax.experimental import pallas as pl
from jax.experimental.pallas import tpu as pltpu

F32 = jnp.float32
BF16 = jnp.bfloat16

CHUNK = 64
GMLP_CHUNK = 128
GMLP_GROUPS = 12
GMLP_GROUP_DIM = 128
GMLP_WIDTH = GMLP_GROUPS * GMLP_GROUP_DIM
SWA_HEADS = 24
SWA_KV_HEADS = 4
SWA_HEAD_DIM = 64
SWA_WIDTH = SWA_HEADS * SWA_HEAD_DIM
SWA_KV_WIDTH = SWA_KV_HEADS * SWA_HEAD_DIM
WINDOW = 128
ROPE_THETA = 500000.0
ROPE_DIM = SWA_HEAD_DIM // 4
MEM_HEADS = 4
MEM_HEAD_DIM = 256
MEM_WIDTH = MEM_HEADS * MEM_HEAD_DIM
PAST_LEN = 1024
EPS = 1e-6
NEG = -1e30

COL_U = 0
COL_V = GMLP_WIDTH
COL_Q = 2 * GMLP_WIDTH
COL_K = COL_Q + SWA_WIDTH
COL_VV = COL_K + SWA_KV_WIDTH
COL_CQ = COL_VV + SWA_KV_WIDTH
COL_GATE = COL_CQ + MEM_WIDTH

LANES = 128
HEADS_PER_LANE_TILE = LANES // SWA_HEAD_DIM
VMEM_LIMIT = 56 * 1024 * 1024


def _params(*sem):
    return pltpu.CompilerParams(dimension_semantics=sem, vmem_limit_bytes=VMEM_LIMIT)


def _rms(x, g):
    return x * lax.rsqrt(jnp.mean(x * x, axis=-1, keepdims=True) + EPS) * g


def _rmsnorm_kernel(x_ref, g_ref, o_ref):
    o_ref[...] = _rms(x_ref[...], g_ref[...]).astype(o_ref.dtype)


def rmsnorm_cast(x, g_all, layer, tm=512):
    m, d = x.shape
    return pl.pallas_call(
        _rmsnorm_kernel,
        grid=(m // tm,),
        in_specs=[pl.BlockSpec((tm, d), lambda i: (i, 0)),
                  pl.BlockSpec((None, 1, d), lambda i: (layer, 0, 0))],
        out_specs=pl.BlockSpec((tm, d), lambda i: (i, 0)),
        out_shape=jax.ShapeDtypeStruct((m, d), BF16),
        compiler_params=_params("parallel"),
        name="rmsnorm_cast",
    )(x, g_all)


def _proj_in_kernel(h_ref, w_ref, o_ref, *, n_gelu, n_lin):
    j = pl.program_id(1)

    def project():
        return jnp.dot(h_ref[...], w_ref[...], preferred_element_type=F32)

    @pl.when(j < n_gelu)
    def _():
        o_ref[...] = jax.nn.gelu(project()).astype(o_ref.dtype)

    @pl.when(jnp.logical_and(j >= n_gelu, j < n_gelu + n_lin))
    def _():
        o_ref[...] = project().astype(o_ref.dtype)

    @pl.when(j >= n_gelu + n_lin)
    def _():
        o_ref[...] = jax.nn.sigmoid(project()).astype(o_ref.dtype)


def proj_in(h, w_all, layer, tm=1024, tn=1024):
    m, d = h.shape
    n = w_all.shape[-1]
    kern = functools.partial(_proj_in_kernel, n_gelu=COL_Q // tn, n_lin=(COL_GATE - COL_Q) // tn)
    return pl.pallas_call(
        kern,
        grid=(m // tm, n // tn),
        in_specs=[pl.BlockSpec((tm, d), lambda i, j: (i, 0)),
                  pl.BlockSpec((None, d, tn), lambda i, j: (layer, 0, j))],
        out_specs=pl.BlockSpec((tm, tn), lambda i, j: (i, j)),
        out_shape=jax.ShapeDtypeStruct((m, n), BF16),
        compiler_params=_params("parallel", "arbitrary"),
        name="proj_in",
    )(h, w_all)


def _gmlp_kernel(u_ref, gv_ref, lng_ref, lnb_ref, ws_ref, bias_ref, o_ref, *v_refs, chunk):
    rows = u_ref.shape[0]
    n_chunks = rows // chunk
    gv = gv_ref[...].astype(F32)
    mu = jnp.mean(gv, axis=-1, keepdims=True)
    dev = gv - mu
    var = jnp.mean(dev * dev, axis=-1, keepdims=True)
    v = dev * lax.rsqrt(var + EPS) * lng_ref[...] + lnb_ref[...]
    if v_refs:
        v_refs[0][...] = v
    vb = v.astype(BF16)
    r_io = lax.broadcasted_iota(jnp.int32, (chunk, chunk), 0)
    c_io = lax.broadcasted_iota(jnp.int32, (chunk, chunk), 1)
    causal = c_io <= r_io
    for g in range(GMLP_GROUPS):
        cols = slice(g * GMLP_GROUP_DIM, (g + 1) * GMLP_GROUP_DIM)
        w = jnp.where(causal, ws_ref[g, :chunk, :chunk], 0.0).astype(BF16)
        vg = jnp.concatenate([vb[c * chunk:(c + 1) * chunk, cols] for c in range(n_chunks)], axis=1)
        s = jnp.dot(w, vg, preferred_element_type=F32)
        bias = bias_ref[:, cols]
        for c in range(n_chunks):
            rs = slice(c * chunk, (c + 1) * chunk)
            sc = s[:, c * GMLP_GROUP_DIM:(c + 1) * GMLP_GROUP_DIM] + bias
            o_ref[rs, cols] = (u_ref[rs, cols].astype(F32) * sc).astype(o_ref.dtype)


def gmlp(z, ln_g, ln_b, w_s, bias_tab, layer, chunk, emit_v, rows=512):
    m = z.shape[0]
    wdt = GMLP_WIDTH
    out_shape = [jax.ShapeDtypeStruct((m, wdt), BF16)]
    out_specs = [pl.BlockSpec((rows, wdt), lambda i: (i, 0))]
    if emit_v:
        out_shape.append(jax.ShapeDtypeStruct((m, wdt), F32))
        out_specs.append(pl.BlockSpec((rows, wdt), lambda i: (i, 0)))
    res = pl.pallas_call(
        functools.partial(_gmlp_kernel, chunk=chunk),
        grid=(m // rows,),
        in_specs=[pl.BlockSpec((rows, wdt), lambda i: (i, COL_U // wdt)),
                  pl.BlockSpec((rows, wdt), lambda i: (i, COL_V // wdt)),
                  pl.BlockSpec((None, 1, wdt), lambda i: (layer, 0, 0)),
                  pl.BlockSpec((None, 1, wdt), lambda i: (layer, 0, 0)),
                  pl.BlockSpec((None, GMLP_GROUPS, GMLP_CHUNK, GMLP_CHUNK), lambda i: (layer, 0, 0, 0)),
                  pl.BlockSpec((None, chunk, wdt), lambda i: (layer, 0, 0))],
        out_specs=out_specs,
        out_shape=out_shape,
        compiler_params=_params("parallel"),
        name="gmlp_c%d" % chunk,
    )(z, z, ln_g, ln_b, w_s, bias_tab)
    return res if emit_v else (res[0], None)


def _rope(x, tab):
    return x * tab[0] + pltpu.roll(x, LANES - ROPE_DIM // 2, 1) * tab[1] + pltpu.roll(x, ROPE_DIM // 2, 1) * tab[2]


def _rope_kv_width(k_ref, tab):
    parts = [_rope(k_ref[:, t * LANES:(t + 1) * LANES].astype(F32), tab)
             for t in range(SWA_KV_WIDTH // LANES)]
    return jnp.concatenate(parts, axis=1)


def _swa_core(q_ref, tab_q, k_rot, v_all, valid, sink_ref, o_ref):
    tq = q_ref.shape[0]
    nk = k_rot.shape[0]
    group = SWA_HEADS // SWA_KV_HEADS
    tiles_per_group = group // HEADS_PER_LANE_TILE
    scale = SWA_HEAD_DIM ** -0.5
    n_tiles = SWA_WIDTH // LANES
    q_rot = [(_rope(q_ref[:, t * LANES:(t + 1) * LANES].astype(F32), tab_q) * scale).astype(BF16)
             for t in range(n_tiles)]
    low_half = lax.broadcasted_iota(jnp.int32, (nk, LANES), 1) < SWA_HEAD_DIM
    for kvh in range(SWA_KV_HEADS):
        c0 = (kvh // HEADS_PER_LANE_TILE) * LANES
        kk = k_rot[:, c0:c0 + LANES]
        vv = v_all[:, c0:c0 + LANES]
        kk_sw = pltpu.roll(kk, SWA_HEAD_DIM, 1)
        vv_sw = pltpu.roll(vv, SWA_HEAD_DIM, 1)
        if kvh % HEADS_PER_LANE_TILE == 0:
            k_ext = [jnp.where(low_half, kk, 0.0), jnp.where(low_half, 0.0, kk_sw)]
            v_ext = [jnp.where(low_half, vv, 0.0), jnp.where(low_half, 0.0, vv_sw)]
        else:
            k_ext = [jnp.where(low_half, kk_sw, 0.0), jnp.where(low_half, 0.0, kk)]
            v_ext = [jnp.where(low_half, vv_sw, 0.0), jnp.where(low_half, 0.0, vv)]
        t0 = kvh * tiles_per_group
        q_stack = jnp.concatenate(q_rot[t0:t0 + tiles_per_group], axis=0)
        acc = None
        for e in range(HEADS_PER_LANE_TILE):
            s = lax.dot_general(q_stack, k_ext[e].astype(BF16), (((1,), (1,)), ((), ())),
                                preferred_element_type=F32)
            probs = []
            for t in range(tiles_per_group):
                sink = sink_ref[HEADS_PER_LANE_TILE * (t0 + t) + e]
                st = jnp.where(valid, s[t * tq:(t + 1) * tq], NEG)
                mx = jnp.maximum(jnp.max(st, axis=-1, keepdims=True), sink)
                p = jnp.exp(st - mx)
                den = jnp.sum(p, axis=-1, keepdims=True) + jnp.exp(sink - mx)
                probs.append((p * (1.0 / den)).astype(BF16))
            pv = jnp.dot(jnp.concatenate(probs, axis=0), v_ext[e].astype(BF16),
                         preferred_element_type=F32)
            acc = pv if acc is None else acc + pv
        for t in range(tiles_per_group):
            o_ref[:, (t0 + t) * LANES:(t0 + t + 1) * LANES] = acc[t * tq:(t + 1) * tq].astype(o_ref.dtype)


def _swa_prompt_kernel(sink_ref, q_ref, kc_ref, kp_ref, vc_ref, vp_ref, tabc_ref, tabp_ref,
                       o_ref, krot_ref):
    tq = q_ref.shape[0]
    tab_c = tabc_ref[...]
    k_cur = _rope_kv_width(kc_ref, tab_c)
    krot_ref[...] = k_cur
    k_rot = jnp.concatenate([_rope_kv_width(kp_ref, tabp_ref[...]), k_cur], axis=0)
    v_all = jnp.concatenate([vp_ref[...], vc_ref[...]], axis=0).astype(F32)
    nk = k_rot.shape[0]
    q_chunk = lax.broadcasted_iota(jnp.int32, (tq, nk), 0) // CHUNK
    k_col = lax.broadcasted_iota(jnp.int32, (tq, nk), 1)
    k_chunk = k_col // CHUNK
    first_valid = jnp.where(pl.program_id(1) == 0, WINDOW, 0)
    valid = (k_chunk >= q_chunk) & (k_chunk <= q_chunk + WINDOW // CHUNK) & (k_col >= first_valid)
    _swa_core(q_ref, tab_c, k_rot, v_all, valid, sink_ref, o_ref)


def swa_prompt(z, sinks, tab, layer, batch, seq):
    tq = WINDOW
    nqb = seq // tq
    m = z.shape[0]

    def prev(i):
        return jnp.maximum(i - 1, 0)

    return pl.pallas_call(
        _swa_prompt_kernel,
        grid=(batch, nqb),
        in_specs=[pl.BlockSpec(memory_space=pltpu.SMEM),
                  pl.BlockSpec((tq, SWA_WIDTH), lambda b, i: (b * nqb + i, COL_Q // SWA_WIDTH)),
                  pl.BlockSpec((tq, SWA_KV_WIDTH), lambda b, i: (b * nqb + i, COL_K // SWA_KV_WIDTH)),
                  pl.BlockSpec((tq, SWA_KV_WIDTH), lambda b, i: (b * nqb + prev(i), COL_K // SWA_KV_WIDTH)),
                  pl.BlockSpec((tq, SWA_KV_WIDTH), lambda b, i: (b * nqb + i, COL_VV // SWA_KV_WIDTH)),
                  pl.BlockSpec((tq, SWA_KV_WIDTH), lambda b, i: (b * nqb + prev(i), COL_VV // SWA_KV_WIDTH)),
                  pl.BlockSpec((3, tq, LANES), lambda b, i: (0, i, 0)),
                  pl.BlockSpec((3, tq, LANES), lambda b, i: (0, prev(i), 0))],
        out_specs=[pl.BlockSpec((tq, SWA_WIDTH), lambda b, i: (b * nqb + i, 0)),
                   pl.BlockSpec((tq, SWA_KV_WIDTH), lambda b, i: (b * nqb + i, 0))],
        out_shape=[jax.ShapeDtypeStruct((m, SWA_WIDTH), BF16),
                   jax.ShapeDtypeStruct((m, SWA_KV_WIDTH), F32)],
        compiler_params=_params("parallel", "arbitrary"),
        name="swa_prompt",
    )(sinks[layer], z, z, z, z, z, tab, tab)


def _swa_sample_kernel(sink_ref, q_ref, kn_ref, vn_ref, ck_ref, cv_ref, tab_ref, o_ref, krot_ref):
    tq = q_ref.shape[0]
    tab = tab_ref[...]
    k_new = _rope_kv_width(kn_ref, tab)
    krot_ref[...] = k_new
    pad = jnp.zeros((2 * WINDOW - WINDOW - tq, SWA_KV_WIDTH), F32)
    k_rot = jnp.concatenate([ck_ref[...], k_new, pad], axis=0)
    v_all = jnp.concatenate([cv_ref[...], vn_ref[...].astype(F32), pad], axis=0)
    nk = k_rot.shape[0]
    valid = lax.broadcasted_iota(jnp.int32, (tq, nk), 1) < WINDOW + tq
    _swa_core(q_ref, tab, k_rot, v_all, valid, sink_ref, o_ref)


def swa_sample(z, cache_k, cache_v, sinks, tab, layer, batch, seq):
    m = z.shape[0]
    return pl.pallas_call(
        _swa_sample_kernel,
        grid=(batch,),
        in_specs=[pl.BlockSpec(memory_space=pltpu.SMEM),
                  pl.BlockSpec((seq, SWA_WIDTH), lambda b: (b, COL_Q // SWA_WIDTH)),
                  pl.BlockSpec((seq, SWA_KV_WIDTH), lambda b: (b, COL_K // SWA_KV_WIDTH)),
                  pl.BlockSpec((seq, SWA_KV_WIDTH), lambda b: (b, COL_VV // SWA_KV_WIDTH)),
                  pl.BlockSpec((None, None, WINDOW, SWA_KV_WIDTH), lambda b: (layer, b, 0, 0)),
                  pl.BlockSpec((None, None, WINDOW, SWA_KV_WIDTH), lambda b: (layer, b, 0, 0)),
                  pl.BlockSpec((3, seq, LANES), lambda b: (0, 0, 0))],
        out_specs=[pl.BlockSpec((seq, SWA_WIDTH), lambda b: (b, 0)),
                   pl.BlockSpec((seq, SWA_KV_WIDTH), lambda b: (b, 0))],
        out_shape=[jax.ShapeDtypeStruct((m, SWA_WIDTH), BF16),
                   jax.ShapeDtypeStruct((m, SWA_KV_WIDTH), F32)],
        compiler_params=_params("parallel"),
        name="swa_sample",
    )(sinks[layer], z, z, z, cache_k, cache_v, tab)


def rope_table(pos):
    half = ROPE_DIM // 2
    inv = ROPE_THETA ** (-jnp.arange(half, dtype=F32) / half)
    ang = pos.astype(F32)[:, None] * inv[None, :]
    cos, sin = jnp.cos(ang), jnp.sin(ang)
    s = pos.shape[0]
    rest = SWA_HEAD_DIM - ROPE_DIM
    c_head = jnp.concatenate([cos, cos, jnp.ones((s, rest), F32)], axis=1)
    up_head = jnp.concatenate([-sin, jnp.zeros((s, half + rest), F32)], axis=1)
    dn_head = jnp.concatenate([jnp.zeros((s, half), F32), sin, jnp.zeros((s, rest), F32)], axis=1)
    return jnp.stack([jnp.tile(t, (1, HEADS_PER_LANE_TILE)) for t in (c_head, up_head, dn_head)])


def _memory_kv_kernel(mem_ref, g_ref, wk_ref, wv_ref, k_ref, v_ref):
    h = _rms(mem_ref[...], g_ref[...]).astype(BF16)
    k_ref[...] = jnp.dot(h, wk_ref[...], preferred_element_type=F32)
    v_ref[...] = jnp.dot(h, wv_ref[...], preferred_element_type=F32)


def memory_kv(mem, g_all, wk_all, wv_all):
    batch, tokens, d = mem.shape
    depth = wk_all.shape[0]
    shp = jax.ShapeDtypeStruct((depth, batch, tokens, MEM_WIDTH), F32)
    return pl.pallas_call(
        _memory_kv_kernel,
        grid=(depth, batch),
        in_specs=[pl.BlockSpec((None, tokens, d), lambda l, b: (b, 0, 0)),
                  pl.BlockSpec((None, 1, d), lambda l, b: (l, 0, 0)),
                  pl.BlockSpec((None, d, MEM_WIDTH), lambda l, b: (l, 0, 0)),
                  pl.BlockSpec((None, d, MEM_WIDTH), lambda l, b: (l, 0, 0))],
        out_specs=[pl.BlockSpec((None, None, tokens, MEM_WIDTH), lambda l, b: (l, b, 0, 0)),
                   pl.BlockSpec((None, None, tokens, MEM_WIDTH), lambda l, b: (l, b, 0, 0))],
        out_shape=[shp, shp],
        compiler_params=_params("parallel", "arbitrary"),
        name="memory_kv",
    )(mem, g_all, wk_all, wv_all)


def _mem_attn_kernel(cq_ref, mk_ref, mv_ref, o_ref):
    scale = MEM_HEAD_DIM ** -0.5
    for h in range(MEM_HEADS):
        cols = slice(h * MEM_HEAD_DIM, (h + 1) * MEM_HEAD_DIM)
        s = lax.dot_general(cq_ref[:, cols], mk_ref[:, cols].astype(BF16), (((1,), (1,)), ((), ())),
                            preferred_element_type=F32) * scale
        mx = jnp.max(s, axis=-1, keepdims=True)
        p = jnp.exp(s - mx)
        den = jnp.sum(p, axis=-1, keepdims=True)
        p = (p * (1.0 / den)).astype(BF16)
        o_ref[:, cols] = jnp.dot(p, mv_ref[:, cols].astype(BF16),
                                 preferred_element_type=F32).astype(o_ref.dtype)


def mem_attn(z, mk_all, mv_all, layer, batch, seq, tq):
    m = z.shape[0]
    nqb = seq // tq
    tokens = mk_all.shape[2]
    return pl.pallas_call(
        _mem_attn_kernel,
        grid=(batch, nqb),
        in_specs=[pl.BlockSpec((tq, MEM_WIDTH), lambda b, i: (b * nqb + i, COL_CQ // MEM_WIDTH)),
                  pl.BlockSpec((None, None, tokens, MEM_WIDTH), lambda b, i: (layer, b, 0, 0)),
                  pl.BlockSpec((None, None, tokens, MEM_WIDTH), lambda b, i: (layer, b, 0, 0))],
        out_specs=pl.BlockSpec((tq, MEM_WIDTH), lambda b, i: (b * nqb + i, 0)),
        out_shape=jax.ShapeDtypeStruct((m, MEM_WIDTH), BF16),
        compiler_params=_params("parallel", "arbitrary"),
        name="mem_attn",
    )(z, mk_all, mv_all)


def _merge_kernel(oa_ref, ob_ref, oc_ref, wa_ref, wb_ref, wc_ref, ga_ref, gb_ref, gc_ref, o_ref):
    acc = ga_ref[...].astype(F32) * jnp.dot(oa_ref[...], wa_ref[...], preferred_element_type=F32)
    acc = acc + gb_ref[...].astype(F32) * jnp.dot(ob_ref[...], wb_ref[...], preferred_element_type=F32)
    acc = acc + gc_ref[...].astype(F32) * jnp.dot(oc_ref[...], wc_ref[...], preferred_element_type=F32)
    o_ref[...] = acc.astype(o_ref.dtype)


def merge(z, oa, ob, oc, wa_all, wb_all, wc_all, layer, tm=1024, tn=512):
    m = z.shape[0]
    d = wa_all.shape[-1]
    g0 = COL_GATE // tn
    gs = d // tn
    return pl.pallas_call(
        _merge_kernel,
        grid=(m // tm, d // tn),
        in_specs=[pl.BlockSpec((tm, GMLP_WIDTH), lambda i, j: (i, 0)),
                  pl.BlockSpec((tm, SWA_WIDTH), lambda i, j: (i, 0)),
                  pl.BlockSpec((tm, MEM_WIDTH), lambda i, j: (i, 0)),
                  pl.BlockSpec((None, GMLP_WIDTH, tn), lambda i, j: (layer, 0, j)),
                  pl.BlockSpec((None, SWA_WIDTH, tn), lambda i, j: (layer, 0, j)),
                  pl.BlockSpec((None, MEM_WIDTH, tn), lambda i, j: (layer, 0, j)),
                  pl.BlockSpec((tm, tn), lambda i, j: (i, g0 + j)),
                  pl.BlockSpec((tm, tn), lambda i, j: (i, g0 + gs + j)),
                  pl.BlockSpec((tm, tn), lambda i, j: (i, g0 + 2 * gs + j))],
        out_specs=pl.BlockSpec((tm, tn), lambda i, j: (i, j)),
        out_shape=jax.ShapeDtypeStruct((m, d), BF16),
        compiler_params=_params("parallel", "arbitrary"),
        name="merge",
    )(oa, ob, oc, wa_all, wb_all, wc_all, z, z, z)


def _out_proj_kernel(mg_ref, w_ref, x_ref, gpost_ref, gnext_ref, xo_ref, ho_ref):
    mix = jnp.dot(mg_ref[...], w_ref[...], preferred_element_type=F32)
    xn = x_ref[...] + _rms(mix, gpost_ref[...])
    xo_ref[...] = xn
    ho_ref[...] = _rms(xn, gnext_ref[...]).astype(ho_ref.dtype)


def out_proj(mg, w_all, x, gpost_all, gnext_all, layer, tm=256):
    m, d = x.shape
    return pl.pallas_call(
        _out_proj_kernel,
        grid=(m // tm,),
        in_specs=[pl.BlockSpec((tm, d), lambda i: (i, 0)),
                  pl.BlockSpec((None, d, d), lambda i: (layer, 0, 0)),
                  pl.BlockSpec((tm, d), lambda i: (i, 0)),
                  pl.BlockSpec((None, 1, d), lambda i: (layer, 0, 0)),
                  pl.BlockSpec((None, 1, d), lambda i: (layer, 0, 0))],
        out_specs=[pl.BlockSpec((tm, d), lambda i: (i, 0)),
                   pl.BlockSpec((tm, d), lambda i: (i, 0))],
        out_shape=[jax.ShapeDtypeStruct((m, d), F32), jax.ShapeDtypeStruct((m, d), BF16)],
        compiler_params=_params("parallel"),
        name="out_proj",
    )(mg, w_all, x, gpost_all, gnext_all)


def _ffn_kernel(h_ref, wu_ref, wd_ref, x_ref, gpost_ref, *rest, emit_next):
    if emit_next:
        gnext_ref, xo_ref, ho_ref = rest
    else:
        (xo_ref,) = rest
    j = pl.program_id(1)

    @pl.when(j == 0)
    def _():
        xo_ref[...] = jnp.zeros_like(xo_ref)

    a = jnp.dot(h_ref[...], wu_ref[...], preferred_element_type=F32)
    a = jnp.square(jnp.maximum(a, 0.0)).astype(BF16)
    xo_ref[...] += jnp.dot(a, wd_ref[...], preferred_element_type=F32)

    @pl.when(j == pl.num_programs(1) - 1)
    def _():
        xn = x_ref[...] + _rms(xo_ref[...], gpost_ref[...])
        xo_ref[...] = xn
        if emit_next:
            ho_ref[...] = _rms(xn, gnext_ref[...]).astype(ho_ref.dtype)


def ffn(h, wu_all, wd_all, x, gpost_all, gnext_all, layer, next_layer, tm=512, th=1024):
    m, d = x.shape
    hidden = wu_all.shape[-1]
    emit_next = next_layer is not None
    in_specs = [pl.BlockSpec((tm, d), lambda i, j: (i, 0)),
                pl.BlockSpec((None, d, th), lambda i, j: (layer, 0, j)),
                pl.BlockSpec((None, th, d), lambda i, j: (layer, j, 0)),
                pl.BlockSpec((tm, d), lambda i, j: (i, 0)),
                pl.BlockSpec((None, 1, d), lambda i, j: (layer, 0, 0))]
    args = [h, wu_all, wd_all, x, gpost_all]
    out_specs = [pl.BlockSpec((tm, d), lambda i, j: (i, 0))]
    out_shape = [jax.ShapeDtypeStruct((m, d), F32)]
    if emit_next:
        in_specs.append(pl.BlockSpec((None, 1, d), lambda i, j: (next_layer, 0, 0)))
        args.append(gnext_all)
        out_specs.append(pl.BlockSpec((tm, d), lambda i, j: (i, 0)))
        out_shape.append(jax.ShapeDtypeStruct((m, d), BF16))
    res = pl.pallas_call(
        functools.partial(_ffn_kernel, emit_next=emit_next),
        grid=(m // tm, hidden // th),
        in_specs=in_specs,
        out_specs=out_specs,
        out_shape=out_shape,
        compiler_params=_params("parallel", "arbitrary"),
        name="ffn",
    )(*args)
    return (res[0], res[1]) if emit_next else (res[0], None)


def kernel(x_prompt, x_sample, cache_swa_k, cache_swa_v, cache_mem_k, cache_mem_v, mem_prompt,
           w_in, ln_v_g, ln_v_b, w_s, b_s, sinks, mem_norm, w_mem_k, w_mem_v,
           w_pa, w_pb, w_pc, w_o, norm_mix_pre, norm_mix_post, norm_ffn_pre, norm_ffn_post,
           w_up, w_down):
    batch, seq, d = x_prompt.shape
    dec_batch, dec_seq, _ = x_sample.shape
    depth = w_in.shape[0]
    mem_tokens = mem_prompt.shape[1]

    w_in_b, w_pa_b, w_pb_b, w_pc_b, w_o_b = (w.astype(BF16) for w in (w_in, w_pa, w_pb, w_pc, w_o))
    w_up_b, w_down_b = w_up.astype(BF16), w_down.astype(BF16)
    w_mk_b, w_mv_b = w_mem_k.astype(BF16), w_mem_v.astype(BF16)
    row = lambda g: g.reshape(depth, 1, g.shape[-1])
    g_mix_pre, g_mix_post = row(norm_mix_pre), row(norm_mix_post)
    g_ffn_pre, g_ffn_post = row(norm_ffn_pre), row(norm_ffn_post)
    g_mem, ln_g, ln_b = row(mem_norm), row(ln_v_g), row(ln_v_b)
    bias_p = jnp.repeat(jnp.swapaxes(b_s[:, :, :GMLP_CHUNK], 1, 2), GMLP_GROUP_DIM, axis=2)
    bias_s = jnp.repeat(jnp.swapaxes(b_s[:, :, :dec_seq], 1, 2), GMLP_GROUP_DIM, axis=2)
    tab_p = rope_table(jnp.arange(seq, dtype=jnp.int32))
    tab_s = rope_table(PAST_LEN + jnp.arange(dec_seq, dtype=jnp.int32))
    cache_k = cache_swa_k.reshape(depth, dec_batch, WINDOW, SWA_KV_WIDTH)
    cache_v = cache_swa_v.reshape(depth, dec_batch, WINDOW, SWA_KV_WIDTH)
    cmem_k = cache_mem_k.reshape(depth, dec_batch, mem_tokens, MEM_WIDTH)
    cmem_v = cache_mem_v.reshape(depth, dec_batch, mem_tokens, MEM_WIDTH)

    mk_p, mv_p = memory_kv(mem_prompt, g_mem, w_mk_b, w_mv_b)

    xp = x_prompt.reshape(batch * seq, d)
    xs = x_sample.reshape(dec_batch * dec_seq, d)
    hp = rmsnorm_cast(xp, g_mix_pre, 0)
    hs = rmsnorm_cast(xs, g_mix_pre, 0)

    k_p, v_p, k_s, v_s, gv_s = [], [], [], [], []
    for l in range(depth):
        nxt = l + 1 if l + 1 < depth else None

        def dense_tail(z, oa, ob, oc, x):
            mg = merge(z, oa, ob, oc, w_pa_b, w_pb_b, w_pc_b, l)
            x, h2 = out_proj(mg, w_o_b, x, g_mix_post, g_ffn_pre, l)
            return ffn(h2, w_up_b, w_down_b, x, g_ffn_post, g_mix_pre, l, nxt)

        z = proj_in(hp, w_in_b, l)
        oa, _ = gmlp(z, ln_g, ln_b, w_s, bias_p, l, GMLP_CHUNK, False)
        ob, krot = swa_prompt(z, sinks, tab_p, l, batch, seq)
        oc = mem_attn(z, mk_p, mv_p, l, batch, seq, 512)
        xp, hp = dense_tail(z, oa, ob, oc, xp)
        k_p.append(krot.reshape(batch, seq, SWA_KV_HEADS, SWA_HEAD_DIM)[:, -WINDOW:])
        v_p.append(z.reshape(batch, seq, -1)[:, -WINDOW:, COL_VV:COL_VV + SWA_KV_WIDTH]
                   .astype(F32).reshape(batch, WINDOW, SWA_KV_HEADS, SWA_HEAD_DIM))

        z = proj_in(hs, w_in_b, l)
        oa, v_rows = gmlp(z, ln_g, ln_b, w_s, bias_s, l, dec_seq, True)
        ob, krot = swa_sample(z, cache_k, cache_v, sinks, tab_s, l, dec_batch, dec_seq)
        oc = mem_attn(z, cmem_k, cmem_v, l, dec_batch, dec_seq, dec_seq)
        xs, hs = dense_tail(z, oa, ob, oc, xs)
        k_s.append(krot.reshape(dec_batch, dec_seq, SWA_KV_HEADS, SWA_HEAD_DIM))
        v_s.append(z[:, COL_VV:COL_VV + SWA_KV_WIDTH].astype(F32)
                   .reshape(dec_batch, dec_seq, SWA_KV_HEADS, SWA_HEAD_DIM))
        gv_s.append(v_rows.reshape(dec_batch, dec_seq, GMLP_WIDTH))

    mem_shape = (depth, batch, mem_tokens, MEM_HEADS, MEM_HEAD_DIM)
    return (xp.reshape(batch, seq, d), xs.reshape(dec_batch, dec_seq, d),
            jnp.stack(k_p), jnp.stack(v_p), mk_p.reshape(mem_shape), mv_p.reshape(mem_shape),
            jnp.stack(k_s), jnp.stack(v_s), jnp.stack(gv_s))
```

```python
import functools

import jax
import jax.numpy as jnp
from jax import lax
from jax.experimental import pallas as pl
from jax.experimental.pallas import tpu as pltpu

F32 = jnp.float32
BF16 = jnp.bfloat16

CHUNK = 64
GMLP_CHUNK = 128
GMLP_GROUPS = 12
GMLP_GROUP_DIM = 128
GMLP_WIDTH = GMLP_GROUPS * GMLP_GROUP_DIM
SWA_HEADS = 24
SWA_KV_HEADS = 4
SWA_HEAD_DIM = 64
SWA_WIDTH = SWA_HEADS * SWA_HEAD_DIM
SWA_KV_WIDTH = SWA_KV_HEADS * SWA_HEAD_DIM
WINDOW = 128
ROPE_THETA = 500000.0
ROPE_DIM = SWA_HEAD_DIM // 4
MEM_HEADS = 4
MEM_HEAD_DIM = 256
MEM_WIDTH = MEM_HEADS * MEM_HEAD_DIM
PAST_LEN = 1024
EPS = 1e-6
NEG = -1e30

COL_U = 0
COL_V = GMLP_WIDTH
COL_Q = 2 * GMLP_WIDTH
COL_K = COL_Q + SWA_WIDTH
COL_VV = COL_K + SWA_KV_WIDTH
COL_CQ = COL_VV + SWA_KV_WIDTH
COL_GATE = COL_CQ + MEM_WIDTH

LOG2E = 1.4426950408889634
Q_SCALE = SWA_HEAD_DIM ** -0.5 * LOG2E

LANES = 128
BF16_SUBLANES = 16
HEADS_PER_LANE_TILE = LANES // SWA_HEAD_DIM
VMEM_LIMIT = 56 * 1024 * 1024


def _params(*sem):
    return pltpu.CompilerParams(dimension_semantics=sem, vmem_limit_bytes=VMEM_LIMIT)


def _rms(x, g):
    return x * lax.rsqrt(jnp.mean(x * x, axis=-1, keepdims=True) + EPS) * g


def _rmsnorm_kernel(x_ref, g_ref, o_ref):
    o_ref[...] = _rms(x_ref[...], g_ref[...]).astype(o_ref.dtype)


def rmsnorm_cast(x, g_all, layer, tm=512):
    m, d = x.shape
    return pl.pallas_call(
        _rmsnorm_kernel,
        grid=(m // tm,),
        in_specs=[pl.BlockSpec((tm, d), lambda i: (i, 0)),
                  pl.BlockSpec((None, 1, d), lambda i: (layer, 0, 0))],
        out_specs=pl.BlockSpec((tm, d), lambda i: (i, 0)),
        out_shape=jax.ShapeDtypeStruct((m, d), BF16),
        compiler_params=_params("parallel"),
        name="rmsnorm_cast",
    )(x, g_all)


def _rope(x, tab):
    return x * tab[0] + pltpu.roll(x, LANES - ROPE_DIM // 2, 1) * tab[1] + pltpu.roll(x, ROPE_DIM // 2, 1) * tab[2]


def rope_table(pos):
    half = ROPE_DIM // 2
    inv = ROPE_THETA ** (-jnp.arange(half, dtype=F32) / half)
    ang = pos.astype(F32)[:, None] * inv[None, :]
    cos, sin = jnp.cos(ang), jnp.sin(ang)
    s = pos.shape[0]
    rest = SWA_HEAD_DIM - ROPE_DIM
    c_head = jnp.concatenate([cos, cos, jnp.ones((s, rest), F32)], axis=1)
    up_head = jnp.concatenate([-sin, jnp.zeros((s, half + rest), F32)], axis=1)
    dn_head = jnp.concatenate([jnp.zeros((s, half), F32), sin, jnp.zeros((s, rest), F32)], axis=1)
    return jnp.stack([jnp.tile(t, (1, HEADS_PER_LANE_TILE)) for t in (c_head, up_head, dn_head)])


def _proj_in_kernel(h_ref, w_ref, tab_ref, o_ref, *, tn):
    j = pl.program_id(1)

    def project():
        return jnp.dot(h_ref[...], w_ref[...], preferred_element_type=F32)

    @pl.when(j < COL_Q // tn)
    def _():
        o_ref[...] = jax.nn.gelu(project()).astype(o_ref.dtype)

    for jj in range(COL_Q // tn, COL_GATE // tn):
        @pl.when(j == jj)
        def _(jj=jj):
            acc = project()
            if jj * tn >= COL_VV:
                o_ref[...] = acc.astype(o_ref.dtype)
                return
            tab = tab_ref[...]
            for t in range(tn // LANES):
                col = jj * tn + t * LANES
                x = acc[:, t * LANES:(t + 1) * LANES]
                if col < COL_K:
                    x = _rope(x, tab) * Q_SCALE
                elif col < COL_VV:
                    x = _rope(x, tab)
                o_ref[:, t * LANES:(t + 1) * LANES] = x.astype(o_ref.dtype)

    @pl.when(j >= COL_GATE // tn)
    def _():
        o_ref[...] = jax.nn.sigmoid(project()).astype(o_ref.dtype)


def proj_in(h, w_all, tab, layer, tm=1024, tn=1024):
    m, d = h.shape
    n = w_all.shape[-1]
    tab_blocks = tab.shape[1] // tm
    return pl.pallas_call(
        functools.partial(_proj_in_kernel, tn=tn),
        grid=(m // tm, n // tn),
        in_specs=[pl.BlockSpec((tm, d), lambda i, j: (i, 0)),
                  pl.BlockSpec((None, d, tn), lambda i, j: (layer, 0, j)),
                  pl.BlockSpec((3, tm, LANES), lambda i, j: (0, i % tab_blocks, 0))],
        out_specs=pl.BlockSpec((tm, tn), lambda i, j: (i, j)),
        out_shape=jax.ShapeDtypeStruct((m, n), BF16),
        compiler_params=_params("parallel", "arbitrary"),
        name="proj_in",
    )(h, w_all, tab)


def _gmlp_kernel(u_ref, gv_ref, lng_ref, lnb_ref, ws_ref, bias_ref, o_ref, *v_refs, chunk):
    rows = u_ref.shape[0]
    n_chunks = rows // chunk
    gv = gv_ref[...].astype(F32)
    mu = jnp.mean(gv, axis=-1, keepdims=True)
    dev = gv - mu
    var = jnp.mean(dev * dev, axis=-1, keepdims=True)
    v = dev * lax.rsqrt(var + EPS) * lng_ref[...] + lnb_ref[...]
    if v_refs:
        v_refs[0][...] = v
    vb = v.astype(BF16)
    r_io = lax.broadcasted_iota(jnp.int32, (chunk, chunk), 0)
    c_io = lax.broadcasted_iota(jnp.int32, (chunk, chunk), 1)
    causal = c_io <= r_io
    for g in range(GMLP_GROUPS):
        cols = slice(g * GMLP_GROUP_DIM, (g + 1) * GMLP_GROUP_DIM)
        w = jnp.where(causal, ws_ref[g, :chunk, :chunk], 0.0).astype(BF16)
        vg = jnp.concatenate([vb[c * chunk:(c + 1) * chunk, cols] for c in range(n_chunks)], axis=1)
        s = jnp.dot(w, vg, preferred_element_type=F32)
        bias = bias_ref[:, cols]
        for c in range(n_chunks):
            rs = slice(c * chunk, (c + 1) * chunk)
            sc = s[:, c * GMLP_GROUP_DIM:(c + 1) * GMLP_GROUP_DIM] + bias
            o_ref[rs, cols] = (u_ref[rs, cols].astype(F32) * sc).astype(o_ref.dtype)


def gmlp(z, ln_g, ln_b, w_s, bias_tab, layer, chunk, emit_v, rows=512):
    m = z.shape[0]
    wdt = GMLP_WIDTH
    out_shape = [jax.ShapeDtypeStruct((m, wdt), BF16)]
    out_specs = [pl.BlockSpec((rows, wdt), lambda i: (i, 0))]
    if emit_v:
        out_shape.append(jax.ShapeDtypeStruct((m, wdt), F32))
        out_specs.append(pl.BlockSpec((rows, wdt), lambda i: (i, 0)))
    res = pl.pallas_call(
        functools.partial(_gmlp_kernel, chunk=chunk),
        grid=(m // rows,),
        in_specs=[pl.BlockSpec((rows, wdt), lambda i: (i, COL_U // wdt)),
                  pl.BlockSpec((rows, wdt), lambda i: (i, COL_V // wdt)),
                  pl.BlockSpec((None, 1, wdt), lambda i: (layer, 0, 0)),
                  pl.BlockSpec((None, 1, wdt), lambda i: (layer, 0, 0)),
                  pl.BlockSpec((None, GMLP_GROUPS, GMLP_CHUNK, GMLP_CHUNK), lambda i: (layer, 0, 0, 0)),
                  pl.BlockSpec((None, chunk, wdt), lambda i: (layer, 0, 0))],
        out_specs=out_specs,
        out_shape=out_shape,
        compiler_params=_params("parallel"),
        name="gmlp_c%d" % chunk,
    )(z, z, ln_g, ln_b, w_s, bias_tab)
    return res if emit_v else (res[0], None)


def _swa_core(q_ref, k_all, v_all, valid, sink_ref, o_ref):
    rows = q_ref.shape[0]
    nk = k_all.shape[0]
    tiles_per_group = SWA_HEADS // SWA_KV_HEADS // HEADS_PER_LANE_TILE
    low_half = lax.broadcasted_iota(jnp.int32, (nk, LANES), 1) < SWA_HEAD_DIM
    v_t = v_all.T
    ones_rows = jnp.ones((BF16_SUBLANES, nk), BF16)

    def operands(kvh):
        c0 = (kvh // HEADS_PER_LANE_TILE) * LANES
        kk = k_all[:, c0:c0 + LANES]
        kk_sw = pltpu.roll(kk, SWA_HEAD_DIM, 1)
        if kvh % HEADS_PER_LANE_TILE == 0:
            k_ext = [jnp.where(low_half, kk, 0.0), jnp.where(low_half, 0.0, kk_sw)]
        else:
            k_ext = [jnp.where(low_half, kk_sw, 0.0), jnp.where(low_half, 0.0, kk)]
        vt_h = jnp.concatenate([v_t[kvh * SWA_HEAD_DIM:(kvh + 1) * SWA_HEAD_DIM, :].astype(BF16), ones_rows],
                               axis=0)
        q_tiles = []
        for t in range(kvh * tiles_per_group, (kvh + 1) * tiles_per_group):
            qt = q_ref[:, t * LANES:(t + 1) * LANES]
            if rows < LANES:
                qt = jnp.concatenate([qt, jnp.zeros((LANES - rows, LANES), BF16)], axis=0)
            q_tiles.append(qt)
        q_stack = jnp.concatenate(q_tiles, axis=0)
        return [k.astype(BF16) for k in k_ext], vt_h, q_stack

    def scores(ops, e):
        return lax.dot_general(ops[0][e], ops[2], (((1,), (1,)), ((), ())),
                               preferred_element_type=F32)

    def attend(ops, kvh, e, s):
        probs, sink_p = [], []
        for t in range(tiles_per_group):
            sink = sink_ref[HEADS_PER_LANE_TILE * (kvh * tiles_per_group + t) + e] * LOG2E
            st = jnp.where(valid, s[:, t * LANES:(t + 1) * LANES], NEG)
            mx = jnp.maximum(jnp.max(st, axis=0, keepdims=True), sink)
            probs.append(jnp.exp2(st - mx).astype(BF16))
            sink_p.append(jnp.exp2(sink - mx))
        pv = jnp.dot(ops[1], jnp.concatenate(probs, axis=1), preferred_element_type=F32)
        den = pv[SWA_HEAD_DIM:SWA_HEAD_DIM + 1] + jnp.concatenate(sink_p, axis=1)
        return pv[:SWA_HEAD_DIM] * (1.0 / den)

    items = [(kvh, e) for kvh in range(SWA_KV_HEADS) for e in range(HEADS_PER_LANE_TILE)]
    ops_next = operands(0)
    s_next = scores(ops_next, 0)
    halves = []
    for idx, (kvh, e) in enumerate(items):
        ops, s = ops_next, s_next
        if idx + 1 < len(items):
            kvh_n, e_n = items[idx + 1]
            if kvh_n != kvh:
                ops_next = operands(kvh_n)
            s_next = scores(ops_next, e_n)
        halves.append(attend(ops, kvh, e, s))
        if e == HEADS_PER_LANE_TILE - 1:
            o_t = jnp.concatenate(halves, axis=0)
            halves = []
            for t in range(tiles_per_group):
                tile = kvh * tiles_per_group + t
                o_tile = o_t[:, t * LANES:(t + 1) * LANES].T
                o_ref[:, tile * LANES:(tile + 1) * LANES] = o_tile[:rows].astype(o_ref.dtype)


def _swa_prompt_kernel(sink_ref, q_ref, kc_ref, kp_ref, vc_ref, vp_ref, o_ref):
    tq = q_ref.shape[0]
    k_all = jnp.concatenate([kp_ref[...], kc_ref[...]], axis=0).astype(F32)
    v_all = jnp.concatenate([vp_ref[...], vc_ref[...]], axis=0).astype(F32)
    nk = k_all.shape[0]
    k_row = lax.broadcasted_iota(jnp.int32, (nk, tq), 0)
    k_chunk = k_row // CHUNK
    q_chunk = lax.broadcasted_iota(jnp.int32, (nk, tq), 1) // CHUNK
    first_valid = jnp.where(pl.program_id(1) == 0, WINDOW, 0)
    valid = (k_chunk >= q_chunk) & (k_chunk <= q_chunk + WINDOW // CHUNK) & (k_row >= first_valid)
    _swa_core(q_ref, k_all, v_all, valid, sink_ref, o_ref)


def swa_prompt(z, sinks, layer, batch, seq):
    tq = WINDOW
    nqb = seq // tq
    m = z.shape[0]

    def prev(i):
        return jnp.maximum(i - 1, 0)

    return pl.pallas_call(
        _swa_prompt_kernel,
        grid=(batch, nqb),
        in_specs=[pl.BlockSpec(memory_space=pltpu.SMEM),
                  pl.BlockSpec((tq, SWA_WIDTH), lambda b, i: (b * nqb + i, COL_Q // SWA_WIDTH)),
                  pl.BlockSpec((tq, SWA_KV_WIDTH), lambda b, i: (b * nqb + i, COL_K // SWA_KV_WIDTH)),
                  pl.BlockSpec((tq, SWA_KV_WIDTH), lambda b, i: (b * nqb + prev(i), COL_K // SWA_KV_WIDTH)),
                  pl.BlockSpec((tq, SWA_KV_WIDTH), lambda b, i: (b * nqb + i, COL_VV // SWA_KV_WIDTH)),
                  pl.BlockSpec((tq, SWA_KV_WIDTH), lambda b, i: (b * nqb + prev(i), COL_VV // SWA_KV_WIDTH))],
        out_specs=pl.BlockSpec((tq, SWA_WIDTH), lambda b, i: (b * nqb + i, 0)),
        out_shape=jax.ShapeDtypeStruct((m, SWA_WIDTH), BF16),
        compiler_params=_params("parallel", "arbitrary"),
        name="swa_prompt",
    )(sinks[layer], z, z, z, z, z)


def _swa_sample_kernel(sink_ref, q_ref, kn_ref, vn_ref, ck_ref, cv_ref, o_ref):
    tq = q_ref.shape[0]
    pad = jnp.zeros((WINDOW - tq, SWA_KV_WIDTH), F32)
    k_all = jnp.concatenate([ck_ref[...], kn_ref[...].astype(F32), pad], axis=0)
    v_all = jnp.concatenate([cv_ref[...], vn_ref[...].astype(F32), pad], axis=0)
    valid = lax.broadcasted_iota(jnp.int32, (k_all.shape[0], LANES), 0) < WINDOW + tq
    _swa_core(q_ref, k_all, v_all, valid, sink_ref, o_ref)


def swa_sample(z, cache_k, cache_v, sinks, layer, batch, seq):
    m = z.shape[0]
    return pl.pallas_call(
        _swa_sample_kernel,
        grid=(batch,),
        in_specs=[pl.BlockSpec(memory_space=pltpu.SMEM),
                  pl.BlockSpec((seq, SWA_WIDTH), lambda b: (b, COL_Q // SWA_WIDTH)),
                  pl.BlockSpec((seq, SWA_KV_WIDTH), lambda b: (b, COL_K // SWA_KV_WIDTH)),
                  pl.BlockSpec((seq, SWA_KV_WIDTH), lambda b: (b, COL_VV // SWA_KV_WIDTH)),
                  pl.BlockSpec((None, None, WINDOW, SWA_KV_WIDTH), lambda b: (layer, b, 0, 0)),
                  pl.BlockSpec((None, None, WINDOW, SWA_KV_WIDTH), lambda b: (layer, b, 0, 0))],
        out_specs=pl.BlockSpec((seq, SWA_WIDTH), lambda b: (b, 0)),
        out_shape=jax.ShapeDtypeStruct((m, SWA_WIDTH), BF16),
        compiler_params=_params("parallel"),
        name="swa_sample",
    )(sinks[layer], z, z, z, cache_k, cache_v)


def _memory_kv_kernel(mem_ref, g_ref, wk_ref, wv_ref, k_ref, v_ref):
    h = _rms(mem_ref[...], g_ref[...]).astype(BF16)
    k_ref[...] = jnp.dot(h, wk_ref[...], preferred_element_type=F32)
    v_ref[...] = jnp.dot(h, wv_ref[...], preferred_element_type=F32)


def memory_kv(mem, g_all, wk_all, wv_all):
    batch, tokens, d = mem.shape
    depth = wk_all.shape[0]
    shp = jax.ShapeDtypeStruct((depth, batch, tokens, MEM_WIDTH), F32)
    return pl.pallas_call(
        _memory_kv_kernel,
        grid=(depth, batch),
        in_specs=[pl.BlockSpec((None, tokens, d), lambda l, b: (b, 0, 0)),
                  pl.BlockSpec((None, 1, d), lambda l, b: (l, 0, 0)),
                  pl.BlockSpec((None, d, MEM_WIDTH), lambda l, b: (l, 0, 0)),
                  pl.BlockSpec((None, d, MEM_WIDTH), lambda l, b: (l, 0, 0))],
        out_specs=[pl.BlockSpec((None, None, tokens, MEM_WIDTH), lambda l, b: (l, b, 0, 0)),
                   pl.BlockSpec((None, None, tokens, MEM_WIDTH), lambda l, b: (l, b, 0, 0))],
        out_shape=[shp, shp],
        compiler_params=_params("parallel", "arbitrary"),
        name="memory_kv",
    )(mem, g_all, wk_all, wv_all)


def _mem_attn_kernel(cq_ref, mk_ref, mv_ref, o_ref):
    scale = MEM_HEAD_DIM ** -0.5
    split_heads = len(mk_ref.shape) == 3
    for h in range(MEM_HEADS):
        cols = slice(h * MEM_HEAD_DIM, (h + 1) * MEM_HEAD_DIM)
        mk = mk_ref[:, h, :] if split_heads else mk_ref[:, cols]
        mv = mv_ref[:, h, :] if split_heads else mv_ref[:, cols]
        s = lax.dot_general(cq_ref[:, cols], mk.astype(BF16), (((1,), (1,)), ((), ())),
                            preferred_element_type=F32) * scale
        mx = jnp.max(s, axis=-1, keepdims=True)
        p = jnp.exp(s - mx)
        den = jnp.sum(p, axis=-1, keepdims=True)
        p = (p * (1.0 / den)).astype(BF16)
        o_ref[:, cols] = jnp.dot(p, mv.astype(BF16), preferred_element_type=F32).astype(o_ref.dtype)


def mem_attn(z, mk_all, mv_all, layer, batch, seq, tq):
    m = z.shape[0]
    nqb = seq // tq
    tokens = mk_all.shape[2]
    if mk_all.ndim == 5:
        kv_spec = pl.BlockSpec((None, None, tokens, MEM_HEADS, MEM_HEAD_DIM), lambda b, i: (layer, b, 0, 0, 0))
    else:
        kv_spec = pl.BlockSpec((None, None, tokens, MEM_WIDTH), lambda b, i: (layer, b, 0, 0))
    return pl.pallas_call(
        _mem_attn_kernel,
        grid=(batch, nqb),
        in_specs=[pl.BlockSpec((tq, MEM_WIDTH), lambda b, i: (b * nqb + i, COL_CQ // MEM_WIDTH)),
                  kv_spec, kv_spec],
        out_specs=pl.BlockSpec((tq, MEM_WIDTH), lambda b, i: (b * nqb + i, 0)),
        out_shape=jax.ShapeDtypeStruct((m, MEM_WIDTH), BF16),
        compiler_params=_params("parallel", "arbitrary"),
        name="mem_attn",
    )(z, mk_all, mv_all)


def _merge_kernel(oa_ref, ob_ref, oc_ref, wa_ref, wb_ref, wc_ref, ga_ref, gb_ref, gc_ref, o_ref):
    acc = ga_ref[...].astype(F32) * jnp.dot(oa_ref[...], wa_ref[...], preferred_element_type=F32)
    acc = acc + gb_ref[...].astype(F32) * jnp.dot(ob_ref[...], wb_ref[...], preferred_element_type=F32)
    acc = acc + gc_ref[...].astype(F32) * jnp.dot(oc_ref[...], wc_ref[...], preferred_element_type=F32)
    o_ref[...] = acc.astype(o_ref.dtype)


def merge(z, oa, ob, oc, wa_all, wb_all, wc_all, layer, tm=1024, tn=512):
    m = z.shape[0]
    d = wa_all.shape[-1]
    g0 = COL_GATE // tn
    gs = d // tn
    return pl.pallas_call(
        _merge_kernel,
        grid=(m // tm, d // tn),
        in_specs=[pl.BlockSpec((tm, GMLP_WIDTH), lambda i, j: (i, 0)),
                  pl.BlockSpec((tm, SWA_WIDTH), lambda i, j: (i, 0)),
                  pl.BlockSpec((tm, MEM_WIDTH), lambda i, j: (i, 0)),
                  pl.BlockSpec((None, GMLP_WIDTH, tn), lambda i, j: (layer, 0, j)),
                  pl.BlockSpec((None, SWA_WIDTH, tn), lambda i, j: (layer, 0, j)),
                  pl.BlockSpec((None, MEM_WIDTH, tn), lambda i, j: (layer, 0, j)),
                  pl.BlockSpec((tm, tn), lambda i, j: (i, g0 + j)),
                  pl.BlockSpec((tm, tn), lambda i, j: (i, g0 + gs + j)),
                  pl.BlockSpec((tm, tn), lambda i, j: (i, g0 + 2 * gs + j))],
        out_specs=pl.BlockSpec((tm, tn), lambda i, j: (i, j)),
        out_shape=jax.ShapeDtypeStruct((m, d), BF16),
        compiler_params=_params("parallel", "arbitrary"),
        name="merge",
    )(oa, ob, oc, wa_all, wb_all, wc_all, z, z, z)


def _out_proj_kernel(mg_ref, w_ref, x_ref, gpost_ref, gnext_ref, xo_ref, ho_ref):
    mix = jnp.dot(mg_ref[...], w_ref[...], preferred_element_type=F32)
    xn = x_ref[...] + _rms(mix, gpost_ref[...])
    xo_ref[...] = xn
    ho_ref[...] = _rms(xn, gnext_ref[...]).astype(ho_ref.dtype)


def out_proj(mg, w_all, x, gpost_all, gnext_all, layer, tm=256):
    m, d = x.shape
    return pl.pallas_call(
        _out_proj_kernel,
        grid=(m // tm,),
        in_specs=[pl.BlockSpec((tm, d), lambda i: (i, 0)),
                  pl.BlockSpec((None, d, d), lambda i: (layer, 0, 0)),
                  pl.BlockSpec((tm, d), lambda i: (i, 0)),
                  pl.BlockSpec((None, 1, d), lambda i: (layer, 0, 0)),
                  pl.BlockSpec((None, 1, d), lambda i: (layer, 0, 0))],
        out_specs=[pl.BlockSpec((tm, d), lambda i: (i, 0)),
                   pl.BlockSpec((tm, d), lambda i: (i, 0))],
        out_shape=[jax.ShapeDtypeStruct((m, d), F32), jax.ShapeDtypeStruct((m, d), BF16)],
        compiler_params=_params("parallel"),
        name="out_proj",
    )(mg, w_all, x, gpost_all, gnext_all)


def _ffn_kernel(h_ref, wu_ref, wd_ref, x_ref, gpost_ref, *rest, emit_next):
    if emit_next:
        gnext_ref, xo_ref, ho_ref = rest
    else:
        (xo_ref,) = rest
    j = pl.program_id(1)

    @pl.when(j == 0)
    def _():
        xo_ref[...] = jnp.zeros_like(xo_ref)

    a = jnp.dot(h_ref[...], wu_ref[...], preferred_element_type=F32)
    a = jnp.square(jnp.maximum(a, 0.0)).astype(BF16)
    xo_ref[...] += jnp.dot(a, wd_ref[...], preferred_element_type=F32)

    @pl.when(j == pl.num_programs(1) - 1)
    def _():
        xn = x_ref[...] + _rms(xo_ref[...], gpost_ref[...])
        xo_ref[...] = xn
        if emit_next:
            ho_ref[...] = _rms(xn, gnext_ref[...]).astype(ho_ref.dtype)


def ffn(h, wu_all, wd_all, x, gpost_all, gnext_all, layer, next_layer, tm=512, th=1024):
    m, d = x.shape
    hidden = wu_all.shape[-1]
    emit_next = next_layer is not None
    in_specs = [pl.BlockSpec((tm, d), lambda i, j: (i, 0)),
                pl.BlockSpec((None, d, th), lambda i, j: (layer, 0, j)),
                pl.BlockSpec((None, th, d), lambda i, j: (layer, j, 0)),
                pl.BlockSpec((tm, d), lambda i, j: (i, 0)),
                pl.BlockSpec((None, 1, d), lambda i, j: (layer, 0, 0))]
    args = [h, wu_all, wd_all, x, gpost_all]
    out_specs = [pl.BlockSpec((tm, d), lambda i, j: (i, 0))]
    out_shape = [jax.ShapeDtypeStruct((m, d), F32)]
    if emit_next:
        in_specs.append(pl.BlockSpec((None, 1, d), lambda i, j: (next_layer, 0, 0)))
        args.append(gnext_all)
        out_specs.append(pl.BlockSpec((tm, d), lambda i, j: (i, 0)))
        out_shape.append(jax.ShapeDtypeStruct((m, d), BF16))
    res = pl.pallas_call(
        functools.partial(_ffn_kernel, emit_next=emit_next),
        grid=(m // tm, hidden // th),
        in_specs=in_specs,
        out_specs=out_specs,
        out_shape=out_shape,
        compiler_params=_params("parallel", "arbitrary"),
        name="ffn",
    )(*args)
    return (res[0], res[1]) if emit_next else (res[0], None)


def kernel(x_prompt, x_sample, cache_swa_k, cache_swa_v, cache_mem_k, cache_mem_v, mem_prompt,
           w_in, ln_v_g, ln_v_b, w_s, b_s, sinks, mem_norm, w_mem_k, w_mem_v,
           w_pa, w_pb, w_pc, w_o, norm_mix_pre, norm_mix_post, norm_ffn_pre, norm_ffn_post,
           w_up, w_down):
    batch, seq, d = x_prompt.shape
    dec_batch, dec_seq, _ = x_sample.shape
    depth = w_in.shape[0]
    mem_tokens = mem_prompt.shape[1]

    w_in_b, w_pa_b, w_pb_b, w_pc_b, w_o_b = (w.astype(BF16) for w in (w_in, w_pa, w_pb, w_pc, w_o))
    w_up_b, w_down_b = w_up.astype(BF16), w_down.astype(BF16)
    w_mk_b, w_mv_b = w_mem_k.astype(BF16), w_mem_v.astype(BF16)
    row = lambda g: g.reshape(depth, 1, g.shape[-1])
    g_mix_pre, g_mix_post = row(norm_mix_pre), row(norm_mix_post)
    g_ffn_pre, g_ffn_post = row(norm_ffn_pre), row(norm_ffn_post)
    g_mem, ln_g, ln_b = row(mem_norm), row(ln_v_g), row(ln_v_b)
    bias_p = jnp.repeat(jnp.swapaxes(b_s[:, :, :GMLP_CHUNK], 1, 2), GMLP_GROUP_DIM, axis=2)
    bias_s = jnp.repeat(jnp.swapaxes(b_s[:, :, :dec_seq], 1, 2), GMLP_GROUP_DIM, axis=2)
    proj_rows = 1024
    tab_p = rope_table(jnp.arange(seq, dtype=jnp.int32))
    tab_s = jnp.tile(rope_table(PAST_LEN + jnp.arange(dec_seq, dtype=jnp.int32)), (1, proj_rows // dec_seq, 1))
    cache_k = cache_swa_k.reshape(depth, dec_batch, WINDOW, SWA_KV_WIDTH)
    cache_v = cache_swa_v.reshape(depth, dec_batch, WINDOW, SWA_KV_WIDTH)

    mk_p, mv_p = memory_kv(mem_prompt, g_mem, w_mk_b, w_mv_b)

    xp = x_prompt.reshape(batch * seq, d)
    xs = x_sample.reshape(dec_batch * dec_seq, d)
    hp = rmsnorm_cast(xp, g_mix_pre, 0)
    hs = rmsnorm_cast(xs, g_mix_pre, 0)

    k_p, v_p, k_s, v_s, gv_s = [], [], [], [], []
    for l in range(depth):
        nxt = l + 1 if l + 1 < depth else None

        def dense_tail(z, oa, ob, oc, x):
            mg = merge(z, oa, ob, oc, w_pa_b, w_pb_b, w_pc_b, l)
            x, h2 = out_proj(mg, w_o_b, x, g_mix_post, g_ffn_pre, l)
            return ffn(h2, w_up_b, w_down_b, x, g_ffn_post, g_mix_pre, l, nxt)

        z = proj_in(hp, w_in_b, tab_p, l, tm=proj_rows)
        oa, _ = gmlp(z, ln_g, ln_b, w_s, bias_p, l, GMLP_CHUNK, False)
        ob = swa_prompt(z, sinks, l, batch, seq)
        oc = mem_attn(z, mk_p, mv_p, l, batch, seq, 512)
        xp, hp = dense_tail(z, oa, ob, oc, xp)
        kv_tail = z.reshape(batch, seq, -1)[:, -WINDOW:, COL_K:COL_CQ].astype(F32)
        k_p.append(kv_tail[..., :SWA_KV_WIDTH].reshape(batch, WINDOW, SWA_KV_HEADS, SWA_HEAD_DIM))
        v_p.append(kv_tail[..., SWA_KV_WIDTH:].reshape(batch, WINDOW, SWA_KV_HEADS, SWA_HEAD_DIM))

        z = proj_in(hs, w_in_b, tab_s, l, tm=proj_rows)
        oa, v_rows = gmlp(z, ln_g, ln_b, w_s, bias_s, l, dec_seq, True)
        ob = swa_sample(z, cache_k, cache_v, sinks, l, dec_batch, dec_seq)
        oc = mem_attn(z, cache_mem_k, cache_mem_v, l, dec_batch, dec_seq, dec_seq)
        xs, hs = dense_tail(z, oa, ob, oc, xs)
        k_s.append(z[:, COL_K:COL_VV].astype(F32).reshape(dec_batch, dec_seq, SWA_KV_HEADS, SWA_HEAD_DIM))
        v_s.append(z[:, COL_VV:COL_CQ].astype(F32).reshape(dec_batch, dec_seq, SWA_KV_HEADS, SWA_HEAD_DIM))
        gv_s.append(v_rows.reshape(dec_batch, dec_seq, GMLP_WIDTH))

    mem_shape = (depth, batch, mem_tokens, MEM_HEADS, MEM_HEAD_DIM)
    return (xp.reshape(batch, seq, d), xs.reshape(dec_batch, dec_seq, d),
            jnp.stack(k_p), jnp.stack(v_p), mk_p.reshape(mem_shape), mv_p.reshape(mem_shape),
            jnp.stack(k_s), jnp.stack(v_s), jnp.stack(gv_s))
```

```python
import functools

import jax
import jax.numpy as jnp
from jax import lax
from jax.experimental import pallas as pl
from jax.experimental.pallas import tpu as pltpu

F32 = jnp.float32
BF16 = jnp.bfloat16

CHUNK = 64
GMLP_CHUNK = 128
GMLP_GROUPS = 12
GMLP_GROUP_DIM = 128
GMLP_WIDTH = GMLP_GROUPS * GMLP_GROUP_DIM
SWA_HEADS = 24
SWA_KV_HEADS = 4
SWA_HEAD_DIM = 64
SWA_WIDTH = SWA_HEADS * SWA_HEAD_DIM
SWA_KV_WIDTH = SWA_KV_HEADS * SWA_HEAD_DIM
WINDOW = 128
ROPE_THETA = 500000.0
ROPE_DIM = SWA_HEAD_DIM // 4
MEM_HEADS = 4
MEM_HEAD_DIM = 256
MEM_WIDTH = MEM_HEADS * MEM_HEAD_DIM
PAST_LEN = 1024
EPS = 1e-6
NEG = -1e30

COL_U = 0
COL_V = GMLP_WIDTH
COL_Q = 2 * GMLP_WIDTH
COL_K = COL_Q + SWA_WIDTH
COL_VV = COL_K + SWA_KV_WIDTH
COL_CQ = COL_VV + SWA_KV_WIDTH
COL_GATE = COL_CQ + MEM_WIDTH

LOG2E = 1.4426950408889634
Q_SCALE = SWA_HEAD_DIM ** -0.5 * LOG2E

LANES = 128
BF16_SUBLANES = 16
MERGE_COLS = 512
OUT_PROJ_SUB = 256
HEADS_PER_LANE_TILE = LANES // SWA_HEAD_DIM
VMEM_LIMIT = 56 * 1024 * 1024


def _params(*sem):
    return pltpu.CompilerParams(dimension_semantics=sem, vmem_limit_bytes=VMEM_LIMIT)


def _rms(x, g):
    return x * lax.rsqrt(jnp.mean(x * x, axis=-1, keepdims=True) + EPS) * g


def _rmsnorm_kernel(x_ref, g_ref, o_ref):
    o_ref[...] = _rms(x_ref[...], g_ref[...]).astype(o_ref.dtype)


def rmsnorm_cast(x, g_all, layer, tm=512):
    m, d = x.shape
    return pl.pallas_call(
        _rmsnorm_kernel,
        grid=(m // tm,),
        in_specs=[pl.BlockSpec((tm, d), lambda i: (i, 0)),
                  pl.BlockSpec((None, 1, d), lambda i: (layer, 0, 0))],
        out_specs=pl.BlockSpec((tm, d), lambda i: (i, 0)),
        out_shape=jax.ShapeDtypeStruct((m, d), BF16),
        compiler_params=_params("parallel"),
        name="rmsnorm_cast",
    )(x, g_all)


def _rope(x, tab):
    return x * tab[0] + pltpu.roll(x, LANES - ROPE_DIM // 2, 1) * tab[1] + pltpu.roll(x, ROPE_DIM // 2, 1) * tab[2]


def rope_table(pos):
    half = ROPE_DIM // 2
    inv = ROPE_THETA ** (-jnp.arange(half, dtype=F32) / half)
    ang = pos.astype(F32)[:, None] * inv[None, :]
    cos, sin = jnp.cos(ang), jnp.sin(ang)
    s = pos.shape[0]
    rest = SWA_HEAD_DIM - ROPE_DIM
    c_head = jnp.concatenate([cos, cos, jnp.ones((s, rest), F32)], axis=1)
    up_head = jnp.concatenate([-sin, jnp.zeros((s, half + rest), F32)], axis=1)
    dn_head = jnp.concatenate([jnp.zeros((s, half), F32), sin, jnp.zeros((s, rest), F32)], axis=1)
    return jnp.stack([jnp.tile(t, (1, HEADS_PER_LANE_TILE)) for t in (c_head, up_head, dn_head)])


def _proj_in_kernel(h_ref, w_ref, tab_ref, o_ref, wb_ref, *, tn):
    j = pl.program_id(0)

    @pl.when(pl.program_id(1) == 0)
    def _():
        wb_ref[...] = w_ref[...].astype(BF16)

    def project():
        return jnp.dot(h_ref[...], wb_ref[...], preferred_element_type=F32)

    @pl.when(j < COL_Q // tn)
    def _():
        o_ref[...] = jax.nn.gelu(project()).astype(o_ref.dtype)

    for jj in range(COL_Q // tn, COL_GATE // tn):
        @pl.when(j == jj)
        def _(jj=jj):
            acc = project()
            if jj * tn >= COL_VV:
                o_ref[...] = acc.astype(o_ref.dtype)
                return
            tab = tab_ref[...]
            for t in range(tn // LANES):
                col = jj * tn + t * LANES
                x = acc[:, t * LANES:(t + 1) * LANES]
                if col < COL_K:
                    x = _rope(x, tab) * Q_SCALE
                elif col < COL_VV:
                    x = _rope(x, tab)
                o_ref[:, t * LANES:(t + 1) * LANES] = x.astype(o_ref.dtype)

    @pl.when(j >= COL_GATE // tn)
    def _():
        o_ref[...] = jax.nn.sigmoid(project()).astype(o_ref.dtype)


def proj_in(h, w_all, tab, layer, tm=1024, tn=1024):
    m, d = h.shape
    n = w_all.shape[-1]
    tab_blocks = tab.shape[1] // tm

    def uses_tab(j):
        return jnp.logical_and(j >= COL_Q // tn, j * tn < COL_VV)

    return pl.pallas_call(
        functools.partial(_proj_in_kernel, tn=tn),
        grid=(n // tn, m // tm),
        in_specs=[pl.BlockSpec((tm, d), lambda j, i: (i, 0)),
                  pl.BlockSpec((None, d, tn), lambda j, i: (layer, 0, j)),
                  pl.BlockSpec((3, tm, LANES), lambda j, i: (0, jnp.where(uses_tab(j), i % tab_blocks, 0), 0))],
        out_specs=pl.BlockSpec((tm, tn), lambda j, i: (i, j)),
        out_shape=jax.ShapeDtypeStruct((m, n), BF16),
        scratch_shapes=[pltpu.VMEM((d, tn), BF16)],
        compiler_params=_params("arbitrary", "arbitrary"),
        name="proj_in",
    )(h, w_all, tab)


def _gmlp_kernel(u_ref, gv_ref, lng_ref, lnb_ref, ws_ref, bias_ref, o_ref, *v_refs, chunk):
    rows = u_ref.shape[0]
    n_chunks = rows // chunk
    gv = gv_ref[...].astype(F32)
    mu = jnp.mean(gv, axis=-1, keepdims=True)
    dev = gv - mu
    var = jnp.mean(dev * dev, axis=-1, keepdims=True)
    v = dev * lax.rsqrt(var + EPS) * lng_ref[...] + lnb_ref[...]
    if v_refs:
        v_refs[0][...] = v
    vb = v.astype(BF16)
    r_io = lax.broadcasted_iota(jnp.int32, (chunk, chunk), 0)
    c_io = lax.broadcasted_iota(jnp.int32, (chunk, chunk), 1)
    causal = c_io <= r_io
    for g in range(GMLP_GROUPS):
        cols = slice(g * GMLP_GROUP_DIM, (g + 1) * GMLP_GROUP_DIM)
        w = jnp.where(causal, ws_ref[g, :chunk, :chunk], 0.0).astype(BF16)
        vg = jnp.concatenate([vb[c * chunk:(c + 1) * chunk, cols] for c in range(n_chunks)], axis=1)
        s = jnp.dot(w, vg, preferred_element_type=F32)
        bias = bias_ref[:, cols]
        for c in range(n_chunks):
            rs = slice(c * chunk, (c + 1) * chunk)
            sc = s[:, c * GMLP_GROUP_DIM:(c + 1) * GMLP_GROUP_DIM] + bias
            o_ref[rs, cols] = (u_ref[rs, cols].astype(F32) * sc).astype(o_ref.dtype)


def gmlp(z, ln_g, ln_b, w_s, bias_tab, layer, chunk, emit_v, rows=512):
    m = z.shape[0]
    wdt = GMLP_WIDTH
    out_shape = [jax.ShapeDtypeStruct((m, wdt), BF16)]
    out_specs = [pl.BlockSpec((rows, wdt), lambda i: (i, 0))]
    if emit_v:
        out_shape.append(jax.ShapeDtypeStruct((m, wdt), F32))
        out_specs.append(pl.BlockSpec((rows, wdt), lambda i: (i, 0)))
    res = pl.pallas_call(
        functools.partial(_gmlp_kernel, chunk=chunk),
        grid=(m // rows,),
        in_specs=[pl.BlockSpec((rows, wdt), lambda i: (i, COL_U // wdt)),
                  pl.BlockSpec((rows, wdt), lambda i: (i, COL_V // wdt)),
                  pl.BlockSpec((None, 1, wdt), lambda i: (layer, 0, 0)),
                  pl.BlockSpec((None, 1, wdt), lambda i: (layer, 0, 0)),
                  pl.BlockSpec((None, GMLP_GROUPS, GMLP_CHUNK, GMLP_CHUNK), lambda i: (layer, 0, 0, 0)),
                  pl.BlockSpec((None, chunk, wdt), lambda i: (layer, 0, 0))],
        out_specs=out_specs,
        out_shape=out_shape,
        compiler_params=_params("parallel"),
        name="gmlp_c%d" % chunk,
    )(z, z, ln_g, ln_b, w_s, bias_tab)
    return res if emit_v else (res[0], None)


def _swa_core(q_ref, k_all, v_all, valid, sink_ref, o_ref):
    rows = q_ref.shape[0]
    nk = k_all.shape[0]
    tiles_per_group = SWA_HEADS // SWA_KV_HEADS // HEADS_PER_LANE_TILE
    low_half = lax.broadcasted_iota(jnp.int32, (nk, LANES), 1) < SWA_HEAD_DIM
    v_t = v_all.T
    ones_rows = jnp.ones((BF16_SUBLANES, nk), BF16)

    def operands(kvh):
        c0 = (kvh // HEADS_PER_LANE_TILE) * LANES
        kk = k_all[:, c0:c0 + LANES]
        kk_sw = pltpu.roll(kk, SWA_HEAD_DIM, 1)
        if kvh % HEADS_PER_LANE_TILE == 0:
            k_ext = [jnp.where(low_half, kk, 0.0), jnp.where(low_half, 0.0, kk_sw)]
        else:
            k_ext = [jnp.where(low_half, kk_sw, 0.0), jnp.where(low_half, 0.0, kk)]
        vt_h = jnp.concatenate([v_t[kvh * SWA_HEAD_DIM:(kvh + 1) * SWA_HEAD_DIM, :].astype(BF16), ones_rows],
                               axis=0)
        q_tiles = []
        for t in range(kvh * tiles_per_group, (kvh + 1) * tiles_per_group):
            qt = q_ref[:, t * LANES:(t + 1) * LANES]
            if rows < LANES:
                qt = jnp.concatenate([qt, jnp.zeros((LANES - rows, LANES), BF16)], axis=0)
            q_tiles.append(qt)
        q_stack = jnp.concatenate(q_tiles, axis=0)
        return [k.astype(BF16) for k in k_ext], vt_h, q_stack

    def scores(ops, e):
        return lax.dot_general(ops[0][e], ops[2], (((1,), (1,)), ((), ())),
                               preferred_element_type=F32)

    def attend(ops, kvh, e, s):
        probs, sink_p = [], []
        for t in range(tiles_per_group):
            sink = sink_ref[HEADS_PER_LANE_TILE * (kvh * tiles_per_group + t) + e] * LOG2E
            st = jnp.where(valid, s[:, t * LANES:(t + 1) * LANES], NEG)
            mx = jnp.maximum(jnp.max(st, axis=0, keepdims=True), sink)
            probs.append(jnp.exp2(st - mx).astype(BF16))
            sink_p.append(jnp.exp2(sink - mx))
        pv = jnp.dot(ops[1], jnp.concatenate(probs, axis=1), preferred_element_type=F32)
        den = pv[SWA_HEAD_DIM:SWA_HEAD_DIM + 1] + jnp.concatenate(sink_p, axis=1)
        return pv[:SWA_HEAD_DIM] * (1.0 / den)

    items = [(kvh, e) for kvh in range(SWA_KV_HEADS) for e in range(HEADS_PER_LANE_TILE)]
    ops_next = operands(0)
    s_next = scores(ops_next, 0)
    halves = []
    for idx, (kvh, e) in enumerate(items):
        ops, s = ops_next, s_next
        if idx + 1 < len(items):
            kvh_n, e_n = items[idx + 1]
            if kvh_n != kvh:
                ops_next = operands(kvh_n)
            s_next = scores(ops_next, e_n)
        halves.append(attend(ops, kvh, e, s))
        if e == HEADS_PER_LANE_TILE - 1:
            o_t = jnp.concatenate(halves, axis=0)
            halves = []
            for t in range(tiles_per_group):
                tile = kvh * tiles_per_group + t
                o_tile = o_t[:, t * LANES:(t + 1) * LANES].T
                o_ref[:, tile * LANES:(tile + 1) * LANES] = o_tile[:rows].astype(o_ref.dtype)


def _swa_prompt_kernel(sink_ref, q_ref, kc_ref, kp_ref, vc_ref, vp_ref, o_ref):
    tq = q_ref.shape[0]
    k_all = jnp.concatenate([kp_ref[...], kc_ref[...]], axis=0).astype(F32)
    v_all = jnp.concatenate([vp_ref[...], vc_ref[...]], axis=0).astype(F32)
    nk = k_all.shape[0]
    k_row = lax.broadcasted_iota(jnp.int32, (nk, tq), 0)
    k_chunk = k_row // CHUNK
    q_chunk = lax.broadcasted_iota(jnp.int32, (nk, tq), 1) // CHUNK
    first_valid = jnp.where(pl.program_id(1) == 0, WINDOW, 0)
    valid = (k_chunk >= q_chunk) & (k_chunk <= q_chunk + WINDOW // CHUNK) & (k_row >= first_valid)
    _swa_core(q_ref, k_all, v_all, valid, sink_ref, o_ref)


def swa_prompt(z, sinks, layer, batch, seq):
    tq = WINDOW
    nqb = seq // tq
    m = z.shape[0]

    def prev(i):
        return jnp.maximum(i - 1, 0)

    return pl.pallas_call(
        _swa_prompt_kernel,
        grid=(batch, nqb),
        in_specs=[pl.BlockSpec(memory_space=pltpu.SMEM),
                  pl.BlockSpec((tq, SWA_WIDTH), lambda b, i: (b * nqb + i, COL_Q // SWA_WIDTH)),
                  pl.BlockSpec((tq, SWA_KV_WIDTH), lambda b, i: (b * nqb + i, COL_K // SWA_KV_WIDTH)),
                  pl.BlockSpec((tq, SWA_KV_WIDTH), lambda b, i: (b * nqb + prev(i), COL_K // SWA_KV_WIDTH)),
                  pl.BlockSpec((tq, SWA_KV_WIDTH), lambda b, i: (b * nqb + i, COL_VV // SWA_KV_WIDTH)),
                  pl.BlockSpec((tq, SWA_KV_WIDTH), lambda b, i: (b * nqb + prev(i), COL_VV // SWA_KV_WIDTH))],
        out_specs=pl.BlockSpec((tq, SWA_WIDTH), lambda b, i: (b * nqb + i, 0)),
        out_shape=jax.ShapeDtypeStruct((m, SWA_WIDTH), BF16),
        compiler_params=_params("parallel", "arbitrary"),
        name="swa_prompt",
    )(sinks[layer], z, z, z, z, z)


def _swa_sample_kernel(sink_ref, q_ref, kn_ref, vn_ref, ck_ref, cv_ref, o_ref):
    tq = q_ref.shape[0]
    pad = jnp.zeros((WINDOW - tq, SWA_KV_WIDTH), F32)
    k_all = jnp.concatenate([ck_ref[...], kn_ref[...].astype(F32), pad], axis=0)
    v_all = jnp.concatenate([cv_ref[...], vn_ref[...].astype(F32), pad], axis=0)
    valid = lax.broadcasted_iota(jnp.int32, (k_all.shape[0], LANES), 0) < WINDOW + tq
    _swa_core(q_ref, k_all, v_all, valid, sink_ref, o_ref)


def swa_sample(z, cache_k, cache_v, sinks, layer, batch, seq):
    m = z.shape[0]
    return pl.pallas_call(
        _swa_sample_kernel,
        grid=(batch,),
        in_specs=[pl.BlockSpec(memory_space=pltpu.SMEM),
                  pl.BlockSpec((seq, SWA_WIDTH), lambda b: (b, COL_Q // SWA_WIDTH)),
                  pl.BlockSpec((seq, SWA_KV_WIDTH), lambda b: (b, COL_K // SWA_KV_WIDTH)),
                  pl.BlockSpec((seq, SWA_KV_WIDTH), lambda b: (b, COL_VV // SWA_KV_WIDTH)),
                  pl.BlockSpec((None, None, WINDOW, SWA_KV_WIDTH), lambda b: (layer, b, 0, 0)),
                  pl.BlockSpec((None, None, WINDOW, SWA_KV_WIDTH), lambda b: (layer, b, 0, 0))],
        out_specs=pl.BlockSpec((seq, SWA_WIDTH), lambda b: (b, 0)),
        out_shape=jax.ShapeDtypeStruct((m, SWA_WIDTH), BF16),
        compiler_params=_params("parallel"),
        name="swa_sample",
    )(sinks[layer], z, z, z, cache_k, cache_v)


def _memory_kv_kernel(mem_ref, g_ref, wk_ref, wv_ref, k_ref, v_ref):
    h = _rms(mem_ref[...], g_ref[...]).astype(BF16)
    k_ref[...] = jnp.dot(h, wk_ref[...], preferred_element_type=F32)
    v_ref[...] = jnp.dot(h, wv_ref[...], preferred_element_type=F32)


def memory_kv(mem, g_all, wk_all, wv_all):
    batch, tokens, d = mem.shape
    depth = wk_all.shape[0]
    shp = jax.ShapeDtypeStruct((depth, batch, tokens, MEM_WIDTH), F32)
    return pl.pallas_call(
        _memory_kv_kernel,
        grid=(depth, batch),
        in_specs=[pl.BlockSpec((None, tokens, d), lambda l, b: (b, 0, 0)),
                  pl.BlockSpec((None, 1, d), lambda l, b: (l, 0, 0)),
                  pl.BlockSpec((None, d, MEM_WIDTH), lambda l, b: (l, 0, 0)),
                  pl.BlockSpec((None, d, MEM_WIDTH), lambda l, b: (l, 0, 0))],
        out_specs=[pl.BlockSpec((None, None, tokens, MEM_WIDTH), lambda l, b: (l, b, 0, 0)),
                   pl.BlockSpec((None, None, tokens, MEM_WIDTH), lambda l, b: (l, b, 0, 0))],
        out_shape=[shp, shp],
        compiler_params=_params("parallel", "arbitrary"),
        name="memory_kv",
    )(mem, g_all, wk_all, wv_all)


def _mem_attn_kernel(cq_ref, mk_ref, mv_ref, o_ref):
    scale = MEM_HEAD_DIM ** -0.5
    split_heads = len(mk_ref.shape) == 3
    for h in range(MEM_HEADS):
        cols = slice(h * MEM_HEAD_DIM, (h + 1) * MEM_HEAD_DIM)
        mk = mk_ref[:, h, :] if split_heads else mk_ref[:, cols]
        mv = mv_ref[:, h, :] if split_heads else mv_ref[:, cols]
        s = lax.dot_general(cq_ref[:, cols], mk.astype(BF16), (((1,), (1,)), ((), ())),
                            preferred_element_type=F32) * scale
        mx = jnp.max(s, axis=-1, keepdims=True)
        p = jnp.exp(s - mx)
        den = jnp.sum(p, axis=-1, keepdims=True)
        p = (p * (1.0 / den)).astype(BF16)
        o_ref[:, cols] = jnp.dot(p, mv.astype(BF16), preferred_element_type=F32).astype(o_ref.dtype)


def mem_attn(z, mk_all, mv_all, layer, batch, seq, tq):
    m = z.shape[0]
    nqb = seq // tq
    tokens = mk_all.shape[2]
    if mk_all.ndim == 5:
        kv_spec = pl.BlockSpec((None, None, tokens, MEM_HEADS, MEM_HEAD_DIM), lambda b, i: (layer, b, 0, 0, 0))
    else:
        kv_spec = pl.BlockSpec((None, None, tokens, MEM_WIDTH), lambda b, i: (layer, b, 0, 0))
    return pl.pallas_call(
        _mem_attn_kernel,
        grid=(batch, nqb),
        in_specs=[pl.BlockSpec((tq, MEM_WIDTH), lambda b, i: (b * nqb + i, COL_CQ // MEM_WIDTH)),
                  kv_spec, kv_spec],
        out_specs=pl.BlockSpec((tq, MEM_WIDTH), lambda b, i: (b * nqb + i, 0)),
        out_shape=jax.ShapeDtypeStruct((m, MEM_WIDTH), BF16),
        compiler_params=_params("parallel", "arbitrary"),
        name="mem_attn",
    )(z, mk_all, mv_all)


def _merge_kernel(oa_ref, ob_ref, oc_ref, w_ref, g_ref, o_ref):
    d = o_ref.shape[1]
    ka, kb = GMLP_WIDTH, GMLP_WIDTH + SWA_WIDTH
    for c in range(d // MERGE_COLS):
        cols = slice(c * MERGE_COLS, (c + 1) * MERGE_COLS)

        def gate(k):
            return g_ref[:, k * d + c * MERGE_COLS:k * d + (c + 1) * MERGE_COLS].astype(F32)

        acc = gate(0) * jnp.dot(oa_ref[...], w_ref[0:ka, cols], preferred_element_type=F32)
        acc = acc + gate(1) * jnp.dot(ob_ref[...], w_ref[ka:kb, cols], preferred_element_type=F32)
        acc = acc + gate(2) * jnp.dot(oc_ref[...], w_ref[kb:, cols], preferred_element_type=F32)
        o_ref[:, cols] = acc.astype(o_ref.dtype)


def merge(z, oa, ob, oc, w_all, layer, tm=512):
    m = z.shape[0]
    k, d = w_all.shape[1:]
    return pl.pallas_call(
        _merge_kernel,
        grid=(m // tm,),
        in_specs=[pl.BlockSpec((tm, GMLP_WIDTH), lambda i: (i, 0)),
                  pl.BlockSpec((tm, SWA_WIDTH), lambda i: (i, 0)),
                  pl.BlockSpec((tm, MEM_WIDTH), lambda i: (i, 0)),
                  pl.BlockSpec((None, k, d), lambda i: (layer, 0, 0), pipeline_mode=pl.Buffered(1)),
                  pl.BlockSpec((tm, 3 * d), lambda i: (i, COL_GATE // (3 * d)))],
        out_specs=pl.BlockSpec((tm, d), lambda i: (i, 0)),
        out_shape=jax.ShapeDtypeStruct((m, d), BF16),
        compiler_params=_params("parallel"),
        name="merge",
    )(oa, ob, oc, w_all, z)


def _out_proj_kernel(mg_ref, w_ref, x_ref, gpost_ref, gnext_ref, xo_ref, ho_ref):
    n_sub = mg_ref.shape[0] // OUT_PROJ_SUB

    def mix(s):
        return jnp.dot(mg_ref[s * OUT_PROJ_SUB:(s + 1) * OUT_PROJ_SUB, :], w_ref[...],
                       preferred_element_type=F32)

    mix_next = mix(0)
    for s in range(n_sub):
        rows = slice(s * OUT_PROJ_SUB, (s + 1) * OUT_PROJ_SUB)
        mix_cur = mix_next
        if s + 1 < n_sub:
            mix_next = mix(s + 1)
        xn = x_ref[rows, :] + _rms(mix_cur, gpost_ref[...])
        xo_ref[rows, :] = xn
        ho_ref[rows, :] = _rms(xn, gnext_ref[...]).astype(ho_ref.dtype)


def out_proj(mg, w_all, x, gpost_all, gnext_all, layer, tm=512):
    m, d = x.shape
    return pl.pallas_call(
        _out_proj_kernel,
        grid=(m // tm,),
        in_specs=[pl.BlockSpec((tm, d), lambda i: (i, 0)),
                  pl.BlockSpec((None, d, d), lambda i: (layer, 0, 0), pipeline_mode=pl.Buffered(1)),
                  pl.BlockSpec((tm, d), lambda i: (i, 0)),
                  pl.BlockSpec((None, 1, d), lambda i: (layer, 0, 0)),
                  pl.BlockSpec((None, 1, d), lambda i: (layer, 0, 0))],
        out_specs=[pl.BlockSpec((tm, d), lambda i: (i, 0)),
                   pl.BlockSpec((tm, d), lambda i: (i, 0))],
        out_shape=[jax.ShapeDtypeStruct((m, d), F32), jax.ShapeDtypeStruct((m, d), BF16)],
        compiler_params=_params("parallel"),
        name="out_proj",
    )(mg, w_all, x, gpost_all, gnext_all)


def _ffn_kernel(h_ref, wu_ref, wd_ref, x_ref, gpost_ref, *rest, emit_next):
    if emit_next:
        gnext_ref, xo_ref, ho_ref = rest
    else:
        (xo_ref,) = rest
    j = pl.program_id(1)

    @pl.when(j == 0)
    def _():
        xo_ref[...] = jnp.zeros_like(xo_ref)

    a = jnp.dot(h_ref[...], wu_ref[...], preferred_element_type=F32)
    a = jnp.square(jnp.maximum(a, 0.0)).astype(BF16)
    xo_ref[...] += jnp.dot(a, wd_ref[...], preferred_element_type=F32)

    @pl.when(j == pl.num_programs(1) - 1)
    def _():
        xn = x_ref[...] + _rms(xo_ref[...], gpost_ref[...])
        xo_ref[...] = xn
        if emit_next:
            ho_ref[...] = _rms(xn, gnext_ref[...]).astype(ho_ref.dtype)


def ffn(h, wu_all, wd_all, x, gpost_all, gnext_all, layer, next_layer, tm=512, th=1024):
    m, d = x.shape
    hidden = wu_all.shape[-1]
    emit_next = next_layer is not None
    in_specs = [pl.BlockSpec((tm, d), lambda i, j: (i, 0)),
                pl.BlockSpec((None, d, th), lambda i, j: (layer, 0, j)),
                pl.BlockSpec((None, th, d), lambda i, j: (layer, j, 0)),
                pl.BlockSpec((tm, d), lambda i, j: (i, 0)),
                pl.BlockSpec((None, 1, d), lambda i, j: (layer, 0, 0))]
    args = [h, wu_all, wd_all, x, gpost_all]
    out_specs = [pl.BlockSpec((tm, d), lambda i, j: (i, 0))]
    out_shape = [jax.ShapeDtypeStruct((m, d), F32)]
    if emit_next:
        in_specs.append(pl.BlockSpec((None, 1, d), lambda i, j: (next_layer, 0, 0)))
        args.append(gnext_all)
        out_specs.append(pl.BlockSpec((tm, d), lambda i, j: (i, 0)))
        out_shape.append(jax.ShapeDtypeStruct((m, d), BF16))
    res = pl.pallas_call(
        functools.partial(_ffn_kernel, emit_next=emit_next),
        grid=(m // tm, hidden // th),
        in_specs=in_specs,
        out_specs=out_specs,
        out_shape=out_shape,
        compiler_params=_params("parallel", "arbitrary"),
        name="ffn",
    )(*args)
    return (res[0], res[1]) if emit_next else (res[0], None)


def kernel(x_prompt, x_sample, cache_swa_k, cache_swa_v, cache_mem_k, cache_mem_v, mem_prompt,
           w_in, ln_v_g, ln_v_b, w_s, b_s, sinks, mem_norm, w_mem_k, w_mem_v,
           w_pa, w_pb, w_pc, w_o, norm_mix_pre, norm_mix_post, norm_ffn_pre, norm_ffn_post,
           w_up, w_down):
    batch, seq, d = x_prompt.shape
    dec_batch, dec_seq, _ = x_sample.shape
    depth = w_in.shape[0]
    mem_tokens = mem_prompt.shape[1]

    w_p_b = jnp.concatenate([w_pa, w_pb, w_pc], axis=1).astype(BF16)
    w_o_b = w_o.astype(BF16)
    w_up_b, w_down_b = w_up.astype(BF16), w_down.astype(BF16)
    w_mk_b, w_mv_b = w_mem_k.astype(BF16), w_mem_v.astype(BF16)
    row = lambda g: g.reshape(depth, 1, g.shape[-1])
    g_mix_pre, g_mix_post = row(norm_mix_pre), row(norm_mix_post)
    g_ffn_pre, g_ffn_post = row(norm_ffn_pre), row(norm_ffn_post)
    g_mem, ln_g, ln_b = row(mem_norm), row(ln_v_g), row(ln_v_b)
    bias_p = jnp.repeat(jnp.swapaxes(b_s[:, :, :GMLP_CHUNK], 1, 2), GMLP_GROUP_DIM, axis=2)
    bias_s = jnp.repeat(jnp.swapaxes(b_s[:, :, :dec_seq], 1, 2), GMLP_GROUP_DIM, axis=2)
    proj_rows = 1024
    tab_p = rope_table(jnp.arange(seq, dtype=jnp.int32))
    tab_s = jnp.tile(rope_table(PAST_LEN + jnp.arange(dec_seq, dtype=jnp.int32)), (1, proj_rows // dec_seq, 1))
    cache_k = cache_swa_k.reshape(depth, dec_batch, WINDOW, SWA_KV_WIDTH)
    cache_v = cache_swa_v.reshape(depth, dec_batch, WINDOW, SWA_KV_WIDTH)

    mk_p, mv_p = memory_kv(mem_prompt, g_mem, w_mk_b, w_mv_b)

    xp = x_prompt.reshape(batch * seq, d)
    xs = x_sample.reshape(dec_batch * dec_seq, d)
    hp = rmsnorm_cast(xp, g_mix_pre, 0)
    hs = rmsnorm_cast(xs, g_mix_pre, 0)

    k_p, v_p, k_s, v_s, gv_s = [], [], [], [], []
    for l in range(depth):
        nxt = l + 1 if l + 1 < depth else None

        def dense_tail(z, oa, ob, oc, x):
            mg = merge(z, oa, ob, oc, w_p_b, l)
            x, h2 = out_proj(mg, w_o_b, x, g_mix_post, g_ffn_pre, l)
            return ffn(h2, w_up_b, w_down_b, x, g_ffn_post, g_mix_pre, l, nxt)

        z = proj_in(hp, w_in, tab_p, l, tm=proj_rows)
        oa, _ = gmlp(z, ln_g, ln_b, w_s, bias_p, l, GMLP_CHUNK, False)
        ob = swa_prompt(z, sinks, l, batch, seq)
        oc = mem_attn(z, mk_p, mv_p, l, batch, seq, 512)
        xp, hp = dense_tail(z, oa, ob, oc, xp)
        kv_tail = z.reshape(batch, seq, -1)[:, -WINDOW:, COL_K:COL_CQ].astype(F32)
        k_p.append(kv_tail[..., :SWA_KV_WIDTH].reshape(batch, WINDOW, SWA_KV_HEADS, SWA_HEAD_DIM))
        v_p.append(kv_tail[..., SWA_KV_WIDTH:].reshape(batch, WINDOW, SWA_KV_HEADS, SWA_HEAD_DIM))

        z = proj_in(hs, w_in, tab_s, l, tm=proj_rows)
        oa, v_rows = gmlp(z, ln_g, ln_b, w_s, bias_s, l, dec_seq, True)
        ob = swa_sample(z, cache_k, cache_v, sinks, l, dec_batch, dec_seq)
        oc = mem_attn(z, cache_mem_k, cache_mem_v, l, dec_batch, dec_seq, dec_seq)
        xs, hs = dense_tail(z, oa, ob, oc, xs)
        k_s.append(z[:, COL_K:COL_VV].astype(F32).reshape(dec_batch, dec_seq, SWA_KV_HEADS, SWA_HEAD_DIM))
        v_s.append(z[:, COL_VV:COL_CQ].astype(F32).reshape(dec_batch, dec_seq, SWA_KV_HEADS, SWA_HEAD_DIM))
        gv_s.append(v_rows.reshape(dec_batch, dec_seq, GMLP_WIDTH))

    mem_shape = (depth, batch, mem_tokens, MEM_HEADS, MEM_HEAD_DIM)
    return (xp.reshape(batch, seq, d), xs.reshape(dec_batch, dec_seq, d),
            jnp.stack(k_p), jnp.stack(v_p), mk_p.reshape(mem_shape), mv_p.reshape(mem_shape),
            jnp.stack(k_s), jnp.stack(v_s), jnp.stack(gv_s))
```

```python
import functools

import jax
import jax.numpy as jnp
from jax import lax
from jax.experimental import pallas as pl
from jax.experimental.pallas import tpu as pltpu

F32 = jnp.float32
BF16 = jnp.bfloat16

CHUNK = 64
GMLP_CHUNK = 128
GMLP_GROUPS = 12
GMLP_GROUP_DIM = 128
GMLP_WIDTH = GMLP_GROUPS * GMLP_GROUP_DIM
SWA_HEADS = 24
SWA_KV_HEADS = 4
SWA_HEAD_DIM = 64
SWA_WIDTH = SWA_HEADS * SWA_HEAD_DIM
SWA_KV_WIDTH = SWA_KV_HEADS * SWA_HEAD_DIM
WINDOW = 128
ROPE_THETA = 500000.0
ROPE_DIM = SWA_HEAD_DIM // 4
MEM_HEADS = 4
MEM_HEAD_DIM = 256
MEM_WIDTH = MEM_HEADS * MEM_HEAD_DIM
PAST_LEN = 1024
EPS = 1e-6
NEG = -1e30

COL_U = 0
COL_V = GMLP_WIDTH
COL_Q = 2 * GMLP_WIDTH
COL_K = COL_Q + SWA_WIDTH
COL_VV = COL_K + SWA_KV_WIDTH
COL_CQ = COL_VV + SWA_KV_WIDTH
COL_GATE = COL_CQ + MEM_WIDTH

LOG2E = 1.4426950408889634
Q_SCALE = SWA_HEAD_DIM ** -0.5 * LOG2E

LANES = 128
BF16_SUBLANES = 16
MERGE_COLS = 512
PROJ_SUB = 256
OUT_PROJ_SUB = 256
FFN_SUB = 256
SWA_LOOKAHEAD = 3
HEADS_PER_LANE_TILE = LANES // SWA_HEAD_DIM
VMEM_LIMIT = 56 * 1024 * 1024


def _params(*sem):
    return pltpu.CompilerParams(dimension_semantics=sem, vmem_limit_bytes=VMEM_LIMIT)


def _rms(x, g):
    return x * lax.rsqrt(jnp.mean(x * x, axis=-1, keepdims=True) + EPS) * g


def _sub_rows(s, size):
    return slice(s * size, (s + 1) * size)


def _skewed(n, compute, finish):
    nxt = compute(0)
    for s in range(n):
        cur = nxt
        if s + 1 < n:
            nxt = compute(s + 1)
        finish(s, cur)


def _rmsnorm_kernel(x_ref, g_ref, o_ref):
    o_ref[...] = _rms(x_ref[...], g_ref[...]).astype(o_ref.dtype)


def rmsnorm_cast(x, g_all, layer, tm=512):
    m, d = x.shape
    return pl.pallas_call(
        _rmsnorm_kernel,
        grid=(m // tm,),
        in_specs=[pl.BlockSpec((tm, d), lambda i: (i, 0)),
                  pl.BlockSpec((None, 1, d), lambda i: (layer, 0, 0))],
        out_specs=pl.BlockSpec((tm, d), lambda i: (i, 0)),
        out_shape=jax.ShapeDtypeStruct((m, d), BF16),
        compiler_params=_params("parallel"),
        name="rmsnorm_cast",
    )(x, g_all)


def _rope(x, tab):
    return x * tab[0] + pltpu.roll(x, LANES - ROPE_DIM // 2, 1) * tab[1] + pltpu.roll(x, ROPE_DIM // 2, 1) * tab[2]


def rope_table(pos):
    half = ROPE_DIM // 2
    inv = ROPE_THETA ** (-jnp.arange(half, dtype=F32) / half)
    ang = pos.astype(F32)[:, None] * inv[None, :]
    cos, sin = jnp.cos(ang), jnp.sin(ang)
    s = pos.shape[0]
    rest = SWA_HEAD_DIM - ROPE_DIM
    c_head = jnp.concatenate([cos, cos, jnp.ones((s, rest), F32)], axis=1)
    up_head = jnp.concatenate([-sin, jnp.zeros((s, half + rest), F32)], axis=1)
    dn_head = jnp.concatenate([jnp.zeros((s, half), F32), sin, jnp.zeros((s, rest), F32)], axis=1)
    return jnp.stack([jnp.tile(t, (1, HEADS_PER_LANE_TILE)) for t in (c_head, up_head, dn_head)])


def _proj_in_kernel(h_ref, w_ref, tab_ref, o_ref, wb_ref, *, tn):
    j = pl.program_id(0)

    @pl.when(pl.program_id(1) == 0)
    def _():
        wb_ref[...] = w_ref[...].astype(BF16)

    def project(rows):
        return jnp.dot(h_ref[rows, :], wb_ref[...], preferred_element_type=F32)

    def rotary_tile(jj):
        def finish(acc, rows):
            for t in range(tn // LANES):
                col = jj * tn + t * LANES
                x = acc[:, t * LANES:(t + 1) * LANES]
                if col < COL_K:
                    x = _rope(x, tab_ref[:, rows, :]) * Q_SCALE
                elif col < COL_VV:
                    x = _rope(x, tab_ref[:, rows, :])
                o_ref[rows, t * LANES:(t + 1) * LANES] = x.astype(o_ref.dtype)
        return finish

    def plain(acc, rows):
        o_ref[rows, :] = acc.astype(o_ref.dtype)

    def gelu(acc, rows):
        o_ref[rows, :] = jax.nn.gelu(acc).astype(o_ref.dtype)

    def sigmoid(acc, rows):
        o_ref[rows, :] = jax.nn.sigmoid(acc).astype(o_ref.dtype)

    def run(finish):
        _skewed(h_ref.shape[0] // PROJ_SUB, lambda s: project(_sub_rows(s, PROJ_SUB)),
                lambda s, acc: finish(acc, _sub_rows(s, PROJ_SUB)))

    pl.when(j < COL_Q // tn)(lambda: run(gelu))
    for jj in range(COL_Q // tn, COL_GATE // tn):
        pl.when(j == jj)(functools.partial(run, rotary_tile(jj) if jj * tn < COL_VV else plain))
    pl.when(j >= COL_GATE // tn)(lambda: run(sigmoid))


def proj_in(h, w_all, tab, layer, tm=1024, tn=1024):
    m, d = h.shape
    n = w_all.shape[-1]
    tab_blocks = tab.shape[1] // tm

    def uses_tab(j):
        return jnp.logical_and(j >= COL_Q // tn, j * tn < COL_VV)

    return pl.pallas_call(
        functools.partial(_proj_in_kernel, tn=tn),
        grid=(n // tn, m // tm),
        in_specs=[pl.BlockSpec((tm, d), lambda j, i: (i, 0)),
                  pl.BlockSpec((None, d, tn), lambda j, i: (layer, 0, j)),
                  pl.BlockSpec((3, tm, LANES), lambda j, i: (0, jnp.where(uses_tab(j), i % tab_blocks, 0), 0))],
        out_specs=pl.BlockSpec((tm, tn), lambda j, i: (i, j)),
        out_shape=jax.ShapeDtypeStruct((m, n), BF16),
        scratch_shapes=[pltpu.VMEM((d, tn), BF16)],
        compiler_params=_params("arbitrary", "arbitrary"),
        name="proj_in",
    )(h, w_all, tab)


def _gmlp_kernel(u_ref, gv_ref, lng_ref, lnb_ref, ws_ref, bias_ref, o_ref, *v_refs, chunk):
    rows = u_ref.shape[0]
    n_chunks = rows // chunk
    gv = gv_ref[...].astype(F32)
    mu = jnp.mean(gv, axis=-1, keepdims=True)
    dev = gv - mu
    var = jnp.mean(dev * dev, axis=-1, keepdims=True)
    v = dev * lax.rsqrt(var + EPS) * lng_ref[...] + lnb_ref[...]
    if v_refs:
        v_refs[0][...] = v
    vb = v.astype(BF16)
    r_io = lax.broadcasted_iota(jnp.int32, (chunk, chunk), 0)
    c_io = lax.broadcasted_iota(jnp.int32, (chunk, chunk), 1)
    causal = c_io <= r_io
    for g in range(GMLP_GROUPS):
        cols = slice(g * GMLP_GROUP_DIM, (g + 1) * GMLP_GROUP_DIM)
        w = jnp.where(causal, ws_ref[g, :chunk, :chunk], 0.0).astype(BF16)
        vg = jnp.concatenate([vb[c * chunk:(c + 1) * chunk, cols] for c in range(n_chunks)], axis=1)
        s = jnp.dot(w, vg, preferred_element_type=F32)
        bias = bias_ref[:, cols]
        for c in range(n_chunks):
            rs = slice(c * chunk, (c + 1) * chunk)
            sc = s[:, c * GMLP_GROUP_DIM:(c + 1) * GMLP_GROUP_DIM] + bias
            o_ref[rs, cols] = (u_ref[rs, cols].astype(F32) * sc).astype(o_ref.dtype)


def gmlp(z, ln_g, ln_b, w_s, bias_tab, layer, chunk, emit_v, rows=512):
    m = z.shape[0]
    wdt = GMLP_WIDTH
    out_shape = [jax.ShapeDtypeStruct((m, wdt), BF16)]
    out_specs = [pl.BlockSpec((rows, wdt), lambda i: (i, 0))]
    if emit_v:
        out_shape.append(jax.ShapeDtypeStruct((m, wdt), F32))
        out_specs.append(pl.BlockSpec((rows, wdt), lambda i: (i, 0)))
    res = pl.pallas_call(
        functools.partial(_gmlp_kernel, chunk=chunk),
        grid=(m // rows,),
        in_specs=[pl.BlockSpec((rows, wdt), lambda i: (i, COL_U // wdt)),
                  pl.BlockSpec((rows, wdt), lambda i: (i, COL_V // wdt)),
                  pl.BlockSpec((None, 1, wdt), lambda i: (layer, 0, 0)),
                  pl.BlockSpec((None, 1, wdt), lambda i: (layer, 0, 0)),
                  pl.BlockSpec((None, GMLP_GROUPS, GMLP_CHUNK, GMLP_CHUNK), lambda i: (layer, 0, 0, 0)),
                  pl.BlockSpec((None, chunk, wdt), lambda i: (layer, 0, 0))],
        out_specs=out_specs,
        out_shape=out_shape,
        compiler_params=_params("parallel"),
        name="gmlp_c%d" % chunk,
    )(z, z, ln_g, ln_b, w_s, bias_tab)
    return res if emit_v else (res[0], None)


def _swa_core(q_ref, k_all, v_all, valid, sink_ref, o_ref):
    rows = q_ref.shape[0]
    nk = k_all.shape[0]
    tiles_per_group = SWA_HEADS // SWA_KV_HEADS // HEADS_PER_LANE_TILE
    low_half = lax.broadcasted_iota(jnp.int32, (nk, LANES), 1) < SWA_HEAD_DIM
    v_t = v_all.T
    ones_rows = jnp.ones((BF16_SUBLANES, nk), BF16)

    def operands(kvh):
        c0 = (kvh // HEADS_PER_LANE_TILE) * LANES
        kk = k_all[:, c0:c0 + LANES]
        kk_sw = pltpu.roll(kk, SWA_HEAD_DIM, 1)
        if kvh % HEADS_PER_LANE_TILE == 0:
            k_ext = [jnp.where(low_half, kk, 0.0), jnp.where(low_half, 0.0, kk_sw)]
        else:
            k_ext = [jnp.where(low_half, kk_sw, 0.0), jnp.where(low_half, 0.0, kk)]
        vt_h = jnp.concatenate([v_t[kvh * SWA_HEAD_DIM:(kvh + 1) * SWA_HEAD_DIM, :].astype(BF16), ones_rows],
                               axis=0)
        q_tiles = []
        for t in range(kvh * tiles_per_group, (kvh + 1) * tiles_per_group):
            qt = q_ref[:, t * LANES:(t + 1) * LANES]
            if rows < LANES:
                qt = jnp.concatenate([qt, jnp.zeros((LANES - rows, LANES), BF16)], axis=0)
            q_tiles.append(qt)
        q_stack = jnp.concatenate(q_tiles, axis=0)
        return [k.astype(BF16) for k in k_ext], vt_h, q_stack

    def scores(ops, e):
        return lax.dot_general(ops[0][e], ops[2], (((1,), (1,)), ((), ())),
                               preferred_element_type=F32)

    def attend(ops, kvh, e, s):
        probs, sink_p = [], []
        for t in range(tiles_per_group):
            sink = sink_ref[HEADS_PER_LANE_TILE * (kvh * tiles_per_group + t) + e] * LOG2E
            st = jnp.where(valid, s[:, t * LANES:(t + 1) * LANES], NEG)
            mx = jnp.maximum(jnp.max(st, axis=0, keepdims=True), sink)
            probs.append(jnp.exp2(st - mx).astype(BF16))
            sink_p.append(jnp.exp2(sink - mx))
        pv = jnp.dot(ops[1], jnp.concatenate(probs, axis=1), preferred_element_type=F32)
        den = pv[SWA_HEAD_DIM:SWA_HEAD_DIM + 1] + jnp.concatenate(sink_p, axis=1)
        return pv[:SWA_HEAD_DIM] * (1.0 / den)

    items = [(kvh, e) for kvh in range(SWA_KV_HEADS) for e in range(HEADS_PER_LANE_TILE)]
    ops_of, pending = {}, []

    def issue(idx):
        kvh, e = items[idx]
        if kvh not in ops_of:
            ops_of[kvh] = operands(kvh)
        pending.append(scores(ops_of[kvh], e))

    for idx in range(min(SWA_LOOKAHEAD, len(items))):
        issue(idx)
    halves = []
    for idx, (kvh, e) in enumerate(items):
        if idx + SWA_LOOKAHEAD < len(items):
            issue(idx + SWA_LOOKAHEAD)
        halves.append(attend(ops_of[kvh], kvh, e, pending.pop(0)))
        if e == HEADS_PER_LANE_TILE - 1:
            o_t = jnp.concatenate(halves, axis=0)
            halves = []
            for t in range(tiles_per_group):
                tile = kvh * tiles_per_group + t
                o_tile = o_t[:, t * LANES:(t + 1) * LANES].T
                o_ref[:, tile * LANES:(tile + 1) * LANES] = o_tile[:rows].astype(o_ref.dtype)


def _swa_prompt_kernel(sink_ref, q_ref, kc_ref, kp_ref, vc_ref, vp_ref, o_ref):
    tq = q_ref.shape[0]
    k_all = jnp.concatenate([kp_ref[...], kc_ref[...]], axis=0).astype(F32)
    v_all = jnp.concatenate([vp_ref[...], vc_ref[...]], axis=0).astype(F32)
    nk = k_all.shape[0]
    k_row = lax.broadcasted_iota(jnp.int32, (nk, tq), 0)
    k_chunk = k_row // CHUNK
    q_chunk = lax.broadcasted_iota(jnp.int32, (nk, tq), 1) // CHUNK
    first_valid = jnp.where(pl.program_id(1) == 0, WINDOW, 0)
    valid = (k_chunk >= q_chunk) & (k_chunk <= q_chunk + WINDOW // CHUNK) & (k_row >= first_valid)
    _swa_core(q_ref, k_all, v_all, valid, sink_ref, o_ref)


def swa_prompt(z, sinks, layer, batch, seq):
    tq = WINDOW
    nqb = seq // tq
    m = z.shape[0]

    def prev(i):
        return jnp.maximum(i - 1, 0)

    return pl.pallas_call(
        _swa_prompt_kernel,
        grid=(batch, nqb),
        in_specs=[pl.BlockSpec(memory_space=pltpu.SMEM),
                  pl.BlockSpec((tq, SWA_WIDTH), lambda b, i: (b * nqb + i, COL_Q // SWA_WIDTH)),
                  pl.BlockSpec((tq, SWA_KV_WIDTH), lambda b, i: (b * nqb + i, COL_K // SWA_KV_WIDTH)),
                  pl.BlockSpec((tq, SWA_KV_WIDTH), lambda b, i: (b * nqb + prev(i), COL_K // SWA_KV_WIDTH)),
                  pl.BlockSpec((tq, SWA_KV_WIDTH), lambda b, i: (b * nqb + i, COL_VV // SWA_KV_WIDTH)),
                  pl.BlockSpec((tq, SWA_KV_WIDTH), lambda b, i: (b * nqb + prev(i), COL_VV // SWA_KV_WIDTH))],
        out_specs=pl.BlockSpec((tq, SWA_WIDTH), lambda b, i: (b * nqb + i, 0)),
        out_shape=jax.ShapeDtypeStruct((m, SWA_WIDTH), BF16),
        compiler_params=_params("parallel", "arbitrary"),
        name="swa_prompt",
    )(sinks[layer], z, z, z, z, z)


def _swa_sample_kernel(sink_ref, q_ref, kn_ref, vn_ref, ck_ref, cv_ref, o_ref):
    tq = q_ref.shape[0]
    pad = jnp.zeros((WINDOW - tq, SWA_KV_WIDTH), F32)
    k_all = jnp.concatenate([ck_ref[...], kn_ref[...].astype(F32), pad], axis=0)
    v_all = jnp.concatenate([cv_ref[...], vn_ref[...].astype(F32), pad], axis=0)
    valid = lax.broadcasted_iota(jnp.int32, (k_all.shape[0], LANES), 0) < WINDOW + tq
    _swa_core(q_ref, k_all, v_all, valid, sink_ref, o_ref)


def swa_sample(z, cache_k, cache_v, sinks, layer, batch, seq):
    m = z.shape[0]
    return pl.pallas_call(
        _swa_sample_kernel,
        grid=(batch,),
        in_specs=[pl.BlockSpec(memory_space=pltpu.SMEM),
                  pl.BlockSpec((seq, SWA_WIDTH), lambda b: (b, COL_Q // SWA_WIDTH)),
                  pl.BlockSpec((seq, SWA_KV_WIDTH), lambda b: (b, COL_K // SWA_KV_WIDTH)),
                  pl.BlockSpec((seq, SWA_KV_WIDTH), lambda b: (b, COL_VV // SWA_KV_WIDTH)),
                  pl.BlockSpec((None, None, WINDOW, SWA_KV_WIDTH), lambda b: (layer, b, 0, 0)),
                  pl.BlockSpec((None, None, WINDOW, SWA_KV_WIDTH), lambda b: (layer, b, 0, 0))],
        out_specs=pl.BlockSpec((seq, SWA_WIDTH), lambda b: (b, 0)),
        out_shape=jax.ShapeDtypeStruct((m, SWA_WIDTH), BF16),
        compiler_params=_params("parallel"),
        name="swa_sample",
    )(sinks[layer], z, z, z, cache_k, cache_v)


def _memory_kv_kernel(mem_ref, g_ref, wk_ref, wv_ref, k_ref, v_ref):
    h = _rms(mem_ref[...], g_ref[...]).astype(BF16)
    k_ref[...] = jnp.dot(h, wk_ref[...], preferred_element_type=F32)
    v_ref[...] = jnp.dot(h, wv_ref[...], preferred_element_type=F32)


def memory_kv(mem, g_all, wk_all, wv_all):
    batch, tokens, d = mem.shape
    depth = wk_all.shape[0]
    shp = jax.ShapeDtypeStruct((depth, batch, tokens, MEM_WIDTH), F32)
    return pl.pallas_call(
        _memory_kv_kernel,
        grid=(depth, batch),
        in_specs=[pl.BlockSpec((None, tokens, d), lambda l, b: (b, 0, 0)),
                  pl.BlockSpec((None, 1, d), lambda l, b: (l, 0, 0)),
                  pl.BlockSpec((None, d, MEM_WIDTH), lambda l, b: (l, 0, 0)),
                  pl.BlockSpec((None, d, MEM_WIDTH), lambda l, b: (l, 0, 0))],
        out_specs=[pl.BlockSpec((None, None, tokens, MEM_WIDTH), lambda l, b: (l, b, 0, 0)),
                   pl.BlockSpec((None, None, tokens, MEM_WIDTH), lambda l, b: (l, b, 0, 0))],
        out_shape=[shp, shp],
        compiler_params=_params("parallel", "arbitrary"),
        name="memory_kv",
    )(mem, g_all, wk_all, wv_all)


def _mem_attn_kernel(cq_ref, mk_ref, mv_ref, o_ref):
    scale = MEM_HEAD_DIM ** -0.5
    split_heads = len(mk_ref.shape) == 3

    def head_cols(h):
        return slice(h * MEM_HEAD_DIM, (h + 1) * MEM_HEAD_DIM)

    def scores(h):
        mk = mk_ref[:, h, :] if split_heads else mk_ref[:, head_cols(h)]
        return lax.dot_general(cq_ref[:, head_cols(h)], mk.astype(BF16), (((1,), (1,)), ((), ())),
                               preferred_element_type=F32) * scale

    def attend(h, s):
        mv = mv_ref[:, h, :] if split_heads else mv_ref[:, head_cols(h)]
        mx = jnp.max(s, axis=-1, keepdims=True)
        p = jnp.exp(s - mx)
        den = jnp.sum(p, axis=-1, keepdims=True)
        p = (p * (1.0 / den)).astype(BF16)
        o_ref[:, head_cols(h)] = jnp.dot(p, mv.astype(BF16), preferred_element_type=F32).astype(o_ref.dtype)

    _skewed(MEM_HEADS, scores, attend)


def mem_attn(z, mk_all, mv_all, layer, batch, seq, tq):
    m = z.shape[0]
    nqb = seq // tq
    tokens = mk_all.shape[2]
    if mk_all.ndim == 5:
        kv_spec = pl.BlockSpec((None, None, tokens, MEM_HEADS, MEM_HEAD_DIM), lambda b, i: (layer, b, 0, 0, 0))
    else:
        kv_spec = pl.BlockSpec((None, None, tokens, MEM_WIDTH), lambda b, i: (layer, b, 0, 0))
    return pl.pallas_call(
        _mem_attn_kernel,
        grid=(batch, nqb),
        in_specs=[pl.BlockSpec((tq, MEM_WIDTH), lambda b, i: (b * nqb + i, COL_CQ // MEM_WIDTH)),
                  kv_spec, kv_spec],
        out_specs=pl.BlockSpec((tq, MEM_WIDTH), lambda b, i: (b * nqb + i, 0)),
        out_shape=jax.ShapeDtypeStruct((m, MEM_WIDTH), BF16),
        compiler_params=_params("parallel", "arbitrary"),
        name="mem_attn",
    )(z, mk_all, mv_all)


def _merge_kernel(oa_ref, ob_ref, oc_ref, w_ref, g_ref, o_ref):
    d = o_ref.shape[1]
    ka, kb = GMLP_WIDTH, GMLP_WIDTH + SWA_WIDTH
    for c in range(d // MERGE_COLS):
        cols = slice(c * MERGE_COLS, (c + 1) * MERGE_COLS)

        def gate(k):
            return g_ref[:, k * d + c * MERGE_COLS:k * d + (c + 1) * MERGE_COLS].astype(F32)

        acc = gate(0) * jnp.dot(oa_ref[...], w_ref[0:ka, cols], preferred_element_type=F32)
        acc = acc + gate(1) * jnp.dot(ob_ref[...], w_ref[ka:kb, cols], preferred_element_type=F32)
        acc = acc + gate(2) * jnp.dot(oc_ref[...], w_ref[kb:, cols], preferred_element_type=F32)
        o_ref[:, cols] = acc.astype(o_ref.dtype)


def merge(z, oa, ob, oc, w_all, layer, tm=512):
    m = z.shape[0]
    k, d = w_all.shape[1:]
    return pl.pallas_call(
        _merge_kernel,
        grid=(m // tm,),
        in_specs=[pl.BlockSpec((tm, GMLP_WIDTH), lambda i: (i, 0)),
                  pl.BlockSpec((tm, SWA_WIDTH), lambda i: (i, 0)),
                  pl.BlockSpec((tm, MEM_WIDTH), lambda i: (i, 0)),
                  pl.BlockSpec((None, k, d), lambda i: (layer, 0, 0), pipeline_mode=pl.Buffered(1)),
                  pl.BlockSpec((tm, 3 * d), lambda i: (i, COL_GATE // (3 * d)))],
        out_specs=pl.BlockSpec((tm, d), lambda i: (i, 0)),
        out_shape=jax.ShapeDtypeStruct((m, d), BF16),
        compiler_params=_params("parallel"),
        name="merge",
    )(oa, ob, oc, w_all, z)


def _out_proj_kernel(mg_ref, w_ref, x_ref, gpost_ref, gnext_ref, xo_ref, ho_ref):
    def mix(s):
        return jnp.dot(mg_ref[_sub_rows(s, OUT_PROJ_SUB), :], w_ref[...], preferred_element_type=F32)

    def finish(s, mixed):
        rows = _sub_rows(s, OUT_PROJ_SUB)
        xn = x_ref[rows, :] + _rms(mixed, gpost_ref[...])
        xo_ref[rows, :] = xn
        ho_ref[rows, :] = _rms(xn, gnext_ref[...]).astype(ho_ref.dtype)

    _skewed(mg_ref.shape[0] // OUT_PROJ_SUB, mix, finish)


def out_proj(mg, w_all, x, gpost_all, gnext_all, layer, tm=512):
    m, d = x.shape
    return pl.pallas_call(
        _out_proj_kernel,
        grid=(m // tm,),
        in_specs=[pl.BlockSpec((tm, d), lambda i: (i, 0)),
                  pl.BlockSpec((None, d, d), lambda i: (layer, 0, 0), pipeline_mode=pl.Buffered(1)),
                  pl.BlockSpec((tm, d), lambda i: (i, 0)),
                  pl.BlockSpec((None, 1, d), lambda i: (layer, 0, 0)),
                  pl.BlockSpec((None, 1, d), lambda i: (layer, 0, 0))],
        out_specs=[pl.BlockSpec((tm, d), lambda i: (i, 0)),
                   pl.BlockSpec((tm, d), lambda i: (i, 0))],
        out_shape=[jax.ShapeDtypeStruct((m, d), F32), jax.ShapeDtypeStruct((m, d), BF16)],
        compiler_params=_params("parallel"),
        name="out_proj",
    )(mg, w_all, x, gpost_all, gnext_all)


def _ffn_kernel(h_ref, wu_ref, wd_ref, x_ref, gpost_ref, *rest, emit_next):
    if emit_next:
        gnext_ref, xo_ref, ho_ref = rest
    else:
        (xo_ref,) = rest
    j = pl.program_id(1)

    @pl.when(j == 0)
    def _():
        xo_ref[...] = jnp.zeros_like(xo_ref)

    def partial_sum(rows):
        a = jnp.dot(h_ref[rows, :], wu_ref[...], preferred_element_type=F32)
        a = jnp.square(jnp.maximum(a, 0.0)).astype(BF16)
        return jnp.dot(a, wd_ref[...], preferred_element_type=F32)

    last = pl.num_programs(1) - 1

    @pl.when(j < last)
    def _():
        xo_ref[...] += partial_sum(slice(None))

    @pl.when(j == last)
    def _():
        def finish(s, part):
            rows = _sub_rows(s, FFN_SUB)
            xn = x_ref[rows, :] + _rms(xo_ref[rows, :] + part, gpost_ref[...])
            xo_ref[rows, :] = xn
            if emit_next:
                ho_ref[rows, :] = _rms(xn, gnext_ref[...]).astype(ho_ref.dtype)

        _skewed(h_ref.shape[0] // FFN_SUB, lambda s: partial_sum(_sub_rows(s, FFN_SUB)), finish)


def ffn(h, wu_all, wd_all, x, gpost_all, gnext_all, layer, next_layer, tm=512, th=1024):
    m, d = x.shape
    hidden = wu_all.shape[-1]
    emit_next = next_layer is not None
    in_specs = [pl.BlockSpec((tm, d), lambda i, j: (i, 0)),
                pl.BlockSpec((None, d, th), lambda i, j: (layer, 0, j)),
                pl.BlockSpec((None, th, d), lambda i, j: (layer, j, 0)),
                pl.BlockSpec((tm, d), lambda i, j: (i, 0)),
                pl.BlockSpec((None, 1, d), lambda i, j: (layer, 0, 0))]
    args = [h, wu_all, wd_all, x, gpost_all]
    out_specs = [pl.BlockSpec((tm, d), lambda i, j: (i, 0))]
    out_shape = [jax.ShapeDtypeStruct((m, d), F32)]
    if emit_next:
        in_specs.append(pl.BlockSpec((None, 1, d), lambda i, j: (next_layer, 0, 0)))
        args.append(gnext_all)
        out_specs.append(pl.BlockSpec((tm, d), lambda i, j: (i, 0)))
        out_shape.append(jax.ShapeDtypeStruct((m, d), BF16))
    res = pl.pallas_call(
        functools.partial(_ffn_kernel, emit_next=emit_next),
        grid=(m // tm, hidden // th),
        in_specs=in_specs,
        out_specs=out_specs,
        out_shape=out_shape,
        compiler_params=_params("parallel", "arbitrary"),
        name="ffn",
    )(*args)
    return (res[0], res[1]) if emit_next else (res[0], None)


def kernel(x_prompt, x_sample, cache_swa_k, cache_swa_v, cache_mem_k, cache_mem_v, mem_prompt,
           w_in, ln_v_g, ln_v_b, w_s, b_s, sinks, mem_norm, w_mem_k, w_mem_v,
           w_pa, w_pb, w_pc, w_o, norm_mix_pre, norm_mix_post, norm_ffn_pre, norm_ffn_post,
           w_up, w_down):
    batch, seq, d = x_prompt.shape
    dec_batch, dec_seq, _ = x_sample.shape
    depth = w_in.shape[0]
    mem_tokens = mem_prompt.shape[1]

    w_p_b = jnp.concatenate([w_pa, w_pb, w_pc], axis=1).astype(BF16)
    w_o_b = w_o.astype(BF16)
    w_up_b, w_down_b = w_up.astype(BF16), w_down.astype(BF16)
    w_mk_b, w_mv_b = w_mem_k.astype(BF16), w_mem_v.astype(BF16)
    row = lambda g: g.reshape(depth, 1, g.shape[-1])
    g_mix_pre, g_mix_post = row(norm_mix_pre), row(norm_mix_post)
    g_ffn_pre, g_ffn_post = row(norm_ffn_pre), row(norm_ffn_post)
    g_mem, ln_g, ln_b = row(mem_norm), row(ln_v_g), row(ln_v_b)
    bias_p = jnp.repeat(jnp.swapaxes(b_s[:, :, :GMLP_CHUNK], 1, 2), GMLP_GROUP_DIM, axis=2)
    bias_s = jnp.repeat(jnp.swapaxes(b_s[:, :, :dec_seq], 1, 2), GMLP_GROUP_DIM, axis=2)
    proj_rows = 1024
    tab_p = rope_table(jnp.arange(seq, dtype=jnp.int32))
    tab_s = jnp.tile(rope_table(PAST_LEN + jnp.arange(dec_seq, dtype=jnp.int32)), (1, proj_rows // dec_seq, 1))
    cache_k = cache_swa_k.reshape(depth, dec_batch, WINDOW, SWA_KV_WIDTH)
    cache_v = cache_swa_v.reshape(depth, dec_batch, WINDOW, SWA_KV_WIDTH)

    mk_p, mv_p = memory_kv(mem_prompt, g_mem, w_mk_b, w_mv_b)

    xp = x_prompt.reshape(batch * seq, d)
    xs = x_sample.reshape(dec_batch * dec_seq, d)
    hp = rmsnorm_cast(xp, g_mix_pre, 0)
    hs = rmsnorm_cast(xs, g_mix_pre, 0)

    k_p, v_p, k_s, v_s, gv_s = [], [], [], [], []
    for l in range(depth):
        nxt = l + 1 if l + 1 < depth else None

        def dense_tail(z, oa, ob, oc, x):
            mg = merge(z, oa, ob, oc, w_p_b, l)
            x, h2 = out_proj(mg, w_o_b, x, g_mix_post, g_ffn_pre, l)
            return ffn(h2, w_up_b, w_down_b, x, g_ffn_post, g_mix_pre, l, nxt)

        z = proj_in(hp, w_in, tab_p, l, tm=proj_rows)
        oa, _ = gmlp(z, ln_g, ln_b, w_s, bias_p, l, GMLP_CHUNK, False)
        ob = swa_prompt(z, sinks, l, batch, seq)
        oc = mem_attn(z, mk_p, mv_p, l, batch, seq, 512)
        xp, hp = dense_tail(z, oa, ob, oc, xp)
        kv_tail = z.reshape(batch, seq, -1)[:, -WINDOW:, COL_K:COL_CQ].astype(F32)
        k_p.append(kv_tail[..., :SWA_KV_WIDTH].reshape(batch, WINDOW, SWA_KV_HEADS, SWA_HEAD_DIM))
        v_p.append(kv_tail[..., SWA_KV_WIDTH:].reshape(batch, WINDOW, SWA_KV_HEADS, SWA_HEAD_DIM))

        z = proj_in(hs, w_in, tab_s, l, tm=proj_rows)
        oa, v_rows = gmlp(z, ln_g, ln_b, w_s, bias_s, l, dec_seq, True)
        ob = swa_sample(z, cache_k, cache_v, sinks, l, dec_batch, dec_seq)
        oc = mem_attn(z, cache_mem_k, cache_mem_v, l, dec_batch, dec_seq, dec_seq)
        xs, hs = dense_tail(z, oa, ob, oc, xs)
        k_s.append(z[:, COL_K:COL_VV].astype(F32).reshape(dec_batch, dec_seq, SWA_KV_HEADS, SWA_HEAD_DIM))
        v_s.append(z[:, COL_VV:COL_CQ].astype(F32).reshape(dec_batch, dec_seq, SWA_KV_HEADS, SWA_HEAD_DIM))
        gv_s.append(v_rows.reshape(dec_batch, dec_seq, GMLP_WIDTH))

    mem_shape = (depth, batch, mem_tokens, MEM_HEADS, MEM_HEAD_DIM)
    return (xp.reshape(batch, seq, d), xs.reshape(dec_batch, dec_seq, d),
            jnp.stack(k_p), jnp.stack(v_p), mk_p.reshape(mem_shape), mv_p.reshape(mem_shape),
            jnp.stack(k_s), jnp.stack(v_s), jnp.stack(gv_s))
```

```python
import functools

import jax
import jax.numpy as jnp
from jax import lax
from jax.experimental import pallas as pl
from jax.experimental.pallas import tpu as pltpu

F32 = jnp.float32
BF16 = jnp.bfloat16

CHUNK = 64
GMLP_CHUNK = 128
GMLP_GROUPS = 12
GMLP_GROUP_DIM = 128
GMLP_WIDTH = GMLP_GROUPS * GMLP_GROUP_DIM
SWA_HEADS = 24
SWA_KV_HEADS = 4
SWA_HEAD_DIM = 64
SWA_WIDTH = SWA_HEADS * SWA_HEAD_DIM
SWA_KV_WIDTH = SWA_KV_HEADS * SWA_HEAD_DIM
WINDOW = 128
ROPE_THETA = 500000.0
ROPE_DIM = SWA_HEAD_DIM // 4
MEM_HEADS = 4
MEM_HEAD_DIM = 256
MEM_WIDTH = MEM_HEADS * MEM_HEAD_DIM
PAST_LEN = 1024
EPS = 1e-6
NEG = -1e30

COL_U = 0
COL_V = GMLP_WIDTH
COL_Q = 2 * GMLP_WIDTH
COL_K = COL_Q + SWA_WIDTH
COL_VV = COL_K + SWA_KV_WIDTH
COL_CQ = COL_VV + SWA_KV_WIDTH
COL_GATE = COL_CQ + MEM_WIDTH

LOG2E = 1.4426950408889634
Q_SCALE = SWA_HEAD_DIM ** -0.5 * LOG2E

LANES = 128
BF16_SUBLANES = 16
MERGE_COLS = 512
PROJ_SUB = 256
OUT_PROJ_SUB = 256
FFN_SUB = 256
FFN_CAST_SUB = 512
SWA_LOOKAHEAD = 3
HEADS_PER_LANE_TILE = LANES // SWA_HEAD_DIM
VMEM_LIMIT = 56 * 1024 * 1024


def _params(*sem):
    return pltpu.CompilerParams(dimension_semantics=sem, vmem_limit_bytes=VMEM_LIMIT)


def _rms(x, g):
    return x * lax.rsqrt(jnp.mean(x * x, axis=-1, keepdims=True) + EPS) * g


def _sub_rows(s, size):
    return slice(s * size, (s + 1) * size)


def _skewed(n, compute, finish):
    nxt = compute(0)
    for s in range(n):
        cur = nxt
        if s + 1 < n:
            nxt = compute(s + 1)
        finish(s, cur)


def _rmsnorm_kernel(x_ref, g_ref, o_ref):
    o_ref[...] = _rms(x_ref[...], g_ref[...]).astype(o_ref.dtype)


def rmsnorm_cast(x, g_all, layer, tm=512):
    m, d = x.shape
    return pl.pallas_call(
        _rmsnorm_kernel,
        grid=(m // tm,),
        in_specs=[pl.BlockSpec((tm, d), lambda i: (i, 0)),
                  pl.BlockSpec((None, 1, d), lambda i: (layer, 0, 0))],
        out_specs=pl.BlockSpec((tm, d), lambda i: (i, 0)),
        out_shape=jax.ShapeDtypeStruct((m, d), BF16),
        compiler_params=_params("parallel"),
        name="rmsnorm_cast",
    )(x, g_all)


def _rope(x, tab):
    return x * tab[0] + pltpu.roll(x, LANES - ROPE_DIM // 2, 1) * tab[1] + pltpu.roll(x, ROPE_DIM // 2, 1) * tab[2]


def rope_table(pos):
    half = ROPE_DIM // 2
    inv = ROPE_THETA ** (-jnp.arange(half, dtype=F32) / half)
    ang = pos.astype(F32)[:, None] * inv[None, :]
    cos, sin = jnp.cos(ang), jnp.sin(ang)
    s = pos.shape[0]
    rest = SWA_HEAD_DIM - ROPE_DIM
    c_head = jnp.concatenate([cos, cos, jnp.ones((s, rest), F32)], axis=1)
    up_head = jnp.concatenate([-sin, jnp.zeros((s, half + rest), F32)], axis=1)
    dn_head = jnp.concatenate([jnp.zeros((s, half), F32), sin, jnp.zeros((s, rest), F32)], axis=1)
    return jnp.stack([jnp.tile(t, (1, HEADS_PER_LANE_TILE)) for t in (c_head, up_head, dn_head)])


def _proj_in_kernel(h_ref, w_ref, tab_ref, o_ref, wb_ref, *, tn):
    j = pl.program_id(0)

    @pl.when(pl.program_id(1) == 0)
    def _():
        wb_ref[...] = w_ref[...].astype(BF16)

    def project(rows):
        return jnp.dot(h_ref[rows, :], wb_ref[...], preferred_element_type=F32)

    def rotary_tile(jj):
        def finish(acc, rows):
            for t in range(tn // LANES):
                col = jj * tn + t * LANES
                x = acc[:, t * LANES:(t + 1) * LANES]
                if col < COL_K:
                    x = _rope(x, tab_ref[:, rows, :]) * Q_SCALE
                elif col < COL_VV:
                    x = _rope(x, tab_ref[:, rows, :])
                o_ref[rows, t * LANES:(t + 1) * LANES] = x.astype(o_ref.dtype)
        return finish

    def plain(acc, rows):
        o_ref[rows, :] = acc.astype(o_ref.dtype)

    def gelu(acc, rows):
        o_ref[rows, :] = jax.nn.gelu(acc).astype(o_ref.dtype)

    def sigmoid(acc, rows):
        o_ref[rows, :] = jax.nn.sigmoid(acc).astype(o_ref.dtype)

    def run(finish):
        _skewed(h_ref.shape[0] // PROJ_SUB, lambda s: project(_sub_rows(s, PROJ_SUB)),
                lambda s, acc: finish(acc, _sub_rows(s, PROJ_SUB)))

    pl.when(j < COL_Q // tn)(lambda: run(gelu))
    for jj in range(COL_Q // tn, COL_GATE // tn):
        pl.when(j == jj)(functools.partial(run, rotary_tile(jj) if jj * tn < COL_VV else plain))
    pl.when(j >= COL_GATE // tn)(lambda: run(sigmoid))


def proj_in(h, w_all, tab, layer, tm=1024, tn=1024):
    m, d = h.shape
    n = w_all.shape[-1]
    tab_blocks = tab.shape[1] // tm

    def uses_tab(j):
        return jnp.logical_and(j >= COL_Q // tn, j * tn < COL_VV)

    return pl.pallas_call(
        functools.partial(_proj_in_kernel, tn=tn),
        grid=(n // tn, m // tm),
        in_specs=[pl.BlockSpec((tm, d), lambda j, i: (i, 0)),
                  pl.BlockSpec((None, d, tn), lambda j, i: (layer, 0, j)),
                  pl.BlockSpec((3, tm, LANES), lambda j, i: (0, jnp.where(uses_tab(j), i % tab_blocks, 0), 0))],
        out_specs=pl.BlockSpec((tm, tn), lambda j, i: (i, j)),
        out_shape=jax.ShapeDtypeStruct((m, n), BF16),
        scratch_shapes=[pltpu.VMEM((d, tn), BF16)],
        compiler_params=_params("arbitrary", "arbitrary"),
        name="proj_in",
    )(h, w_all, tab)


def _gmlp_kernel(u_ref, gv_ref, lng_ref, lnb_ref, ws_ref, bias_ref, o_ref, *v_refs, chunk):
    rows = u_ref.shape[0]
    n_chunks = rows // chunk
    gv = gv_ref[...].astype(F32)
    mu = jnp.mean(gv, axis=-1, keepdims=True)
    dev = gv - mu
    var = jnp.mean(dev * dev, axis=-1, keepdims=True)
    v = dev * lax.rsqrt(var + EPS) * lng_ref[...] + lnb_ref[...]
    if v_refs:
        v_refs[0][...] = v
    vb = v.astype(BF16)
    r_io = lax.broadcasted_iota(jnp.int32, (chunk, chunk), 0)
    c_io = lax.broadcasted_iota(jnp.int32, (chunk, chunk), 1)
    causal = c_io <= r_io
    for g in range(GMLP_GROUPS):
        cols = slice(g * GMLP_GROUP_DIM, (g + 1) * GMLP_GROUP_DIM)
        w = jnp.where(causal, ws_ref[g, :chunk, :chunk], 0.0).astype(BF16)
        vg = jnp.concatenate([vb[c * chunk:(c + 1) * chunk, cols] for c in range(n_chunks)], axis=1)
        s = jnp.dot(w, vg, preferred_element_type=F32)
        bias = bias_ref[:, cols]
        for c in range(n_chunks):
            rs = slice(c * chunk, (c + 1) * chunk)
            sc = s[:, c * GMLP_GROUP_DIM:(c + 1) * GMLP_GROUP_DIM] + bias
            o_ref[rs, cols] = (u_ref[rs, cols].astype(F32) * sc).astype(o_ref.dtype)


def gmlp(z, ln_g, ln_b, w_s, bias_tab, layer, chunk, emit_v, rows=512):
    m = z.shape[0]
    wdt = GMLP_WIDTH
    out_shape = [jax.ShapeDtypeStruct((m, wdt), BF16)]
    out_specs = [pl.BlockSpec((rows, wdt), lambda i: (i, 0))]
    if emit_v:
        out_shape.append(jax.ShapeDtypeStruct((m, wdt), F32))
        out_specs.append(pl.BlockSpec((rows, wdt), lambda i: (i, 0)))
    res = pl.pallas_call(
        functools.partial(_gmlp_kernel, chunk=chunk),
        grid=(m // rows,),
        in_specs=[pl.BlockSpec((rows, wdt), lambda i: (i, COL_U // wdt)),
                  pl.BlockSpec((rows, wdt), lambda i: (i, COL_V // wdt)),
                  pl.BlockSpec((None, 1, wdt), lambda i: (layer, 0, 0)),
                  pl.BlockSpec((None, 1, wdt), lambda i: (layer, 0, 0)),
                  pl.BlockSpec((None, GMLP_GROUPS, GMLP_CHUNK, GMLP_CHUNK), lambda i: (layer, 0, 0, 0)),
                  pl.BlockSpec((None, chunk, wdt), lambda i: (layer, 0, 0))],
        out_specs=out_specs,
        out_shape=out_shape,
        compiler_params=_params("parallel"),
        name="gmlp_c%d" % chunk,
    )(z, z, ln_g, ln_b, w_s, bias_tab)
    return res if emit_v else (res[0], None)


def _swa_core(q_ref, k_all, v_all, valid, sink_ref, o_ref):
    rows = q_ref.shape[0]
    nk = k_all.shape[0]
    tiles_per_group = SWA_HEADS // SWA_KV_HEADS // HEADS_PER_LANE_TILE
    low_half = lax.broadcasted_iota(jnp.int32, (nk, LANES), 1) < SWA_HEAD_DIM
    v_t = v_all.T
    ones_rows = jnp.ones((BF16_SUBLANES, nk), BF16)

    def operands(kvh):
        c0 = (kvh // HEADS_PER_LANE_TILE) * LANES
        kk = k_all[:, c0:c0 + LANES]
        kk_sw = pltpu.roll(kk, SWA_HEAD_DIM, 1)
        if kvh % HEADS_PER_LANE_TILE == 0:
            k_ext = [jnp.where(low_half, kk, 0.0), jnp.where(low_half, 0.0, kk_sw)]
        else:
            k_ext = [jnp.where(low_half, kk_sw, 0.0), jnp.where(low_half, 0.0, kk)]
        vt_h = jnp.concatenate([v_t[kvh * SWA_HEAD_DIM:(kvh + 1) * SWA_HEAD_DIM, :].astype(BF16), ones_rows],
                               axis=0)
        q_tiles = []
        for t in range(kvh * tiles_per_group, (kvh + 1) * tiles_per_group):
            qt = q_ref[:, t * LANES:(t + 1) * LANES]
            if rows < LANES:
                qt = jnp.concatenate([qt, jnp.zeros((LANES - rows, LANES), BF16)], axis=0)
            q_tiles.append(qt)
        q_stack = jnp.concatenate(q_tiles, axis=0)
        return [k.astype(BF16) for k in k_ext], vt_h, q_stack

    def scores(ops, e):
        return lax.dot_general(ops[0][e], ops[2], (((1,), (1,)), ((), ())),
                               preferred_element_type=F32)

    def attend(ops, kvh, e, s):
        probs, sink_p = [], []
        for t in range(tiles_per_group):
            sink = sink_ref[HEADS_PER_LANE_TILE * (kvh * tiles_per_group + t) + e] * LOG2E
            st = jnp.where(valid, s[:, t * LANES:(t + 1) * LANES], NEG)
            mx = jnp.maximum(jnp.max(st, axis=0, keepdims=True), sink)
            probs.append(jnp.exp2(st - mx).astype(BF16))
            sink_p.append(jnp.exp2(sink - mx))
        pv = jnp.dot(ops[1], jnp.concatenate(probs, axis=1), preferred_element_type=F32)
        den = pv[SWA_HEAD_DIM:SWA_HEAD_DIM + 1] + jnp.concatenate(sink_p, axis=1)
        return pv[:SWA_HEAD_DIM] * (1.0 / den)

    items = [(kvh, e) for kvh in range(SWA_KV_HEADS) for e in range(HEADS_PER_LANE_TILE)]
    ops_of, pending = {}, []

    def issue(idx):
        kvh, e = items[idx]
        if kvh not in ops_of:
            ops_of[kvh] = operands(kvh)
        pending.append(scores(ops_of[kvh], e))

    for idx in range(min(SWA_LOOKAHEAD, len(items))):
        issue(idx)
    halves = []
    for idx, (kvh, e) in enumerate(items):
        if idx + SWA_LOOKAHEAD < len(items):
            issue(idx + SWA_LOOKAHEAD)
        halves.append(attend(ops_of[kvh], kvh, e, pending.pop(0)))
        if e == HEADS_PER_LANE_TILE - 1:
            o_t = jnp.concatenate(halves, axis=0)
            halves = []
            for t in range(tiles_per_group):
                tile = kvh * tiles_per_group + t
                o_tile = o_t[:, t * LANES:(t + 1) * LANES].T
                o_ref[:, tile * LANES:(tile + 1) * LANES] = o_tile[:rows].astype(o_ref.dtype)


def _swa_prompt_kernel(sink_ref, q_ref, kc_ref, kp_ref, vc_ref, vp_ref, o_ref):
    tq = q_ref.shape[0]
    k_all = jnp.concatenate([kp_ref[...], kc_ref[...]], axis=0).astype(F32)
    v_all = jnp.concatenate([vp_ref[...], vc_ref[...]], axis=0).astype(F32)
    nk = k_all.shape[0]
    k_row = lax.broadcasted_iota(jnp.int32, (nk, tq), 0)
    k_chunk = k_row // CHUNK
    q_chunk = lax.broadcasted_iota(jnp.int32, (nk, tq), 1) // CHUNK
    first_valid = jnp.where(pl.program_id(1) == 0, WINDOW, 0)
    valid = (k_chunk >= q_chunk) & (k_chunk <= q_chunk + WINDOW // CHUNK) & (k_row >= first_valid)
    _swa_core(q_ref, k_all, v_all, valid, sink_ref, o_ref)


def swa_prompt(z, sinks, layer, batch, seq):
    tq = WINDOW
    nqb = seq // tq
    m = z.shape[0]

    def prev(i):
        return jnp.maximum(i - 1, 0)

    return pl.pallas_call(
        _swa_prompt_kernel,
        grid=(batch, nqb),
        in_specs=[pl.BlockSpec(memory_space=pltpu.SMEM),
                  pl.BlockSpec((tq, SWA_WIDTH), lambda b, i: (b * nqb + i, COL_Q // SWA_WIDTH)),
                  pl.BlockSpec((tq, SWA_KV_WIDTH), lambda b, i: (b * nqb + i, COL_K // SWA_KV_WIDTH)),
                  pl.BlockSpec((tq, SWA_KV_WIDTH), lambda b, i: (b * nqb + prev(i), COL_K // SWA_KV_WIDTH)),
                  pl.BlockSpec((tq, SWA_KV_WIDTH), lambda b, i: (b * nqb + i, COL_VV // SWA_KV_WIDTH)),
                  pl.BlockSpec((tq, SWA_KV_WIDTH), lambda b, i: (b * nqb + prev(i), COL_VV // SWA_KV_WIDTH))],
        out_specs=pl.BlockSpec((tq, SWA_WIDTH), lambda b, i: (b * nqb + i, 0)),
        out_shape=jax.ShapeDtypeStruct((m, SWA_WIDTH), BF16),
        compiler_params=_params("parallel", "arbitrary"),
        name="swa_prompt",
    )(sinks[layer], z, z, z, z, z)


def _swa_sample_kernel(sink_ref, q_ref, kn_ref, vn_ref, ck_ref, cv_ref, o_ref):
    tq = q_ref.shape[0]
    pad = jnp.zeros((WINDOW - tq, SWA_KV_WIDTH), F32)
    k_all = jnp.concatenate([ck_ref[...], kn_ref[...].astype(F32), pad], axis=0)
    v_all = jnp.concatenate([cv_ref[...], vn_ref[...].astype(F32), pad], axis=0)
    valid = lax.broadcasted_iota(jnp.int32, (k_all.shape[0], LANES), 0) < WINDOW + tq
    _swa_core(q_ref, k_all, v_all, valid, sink_ref, o_ref)


def swa_sample(z, cache_k, cache_v, sinks, layer, batch, seq):
    m = z.shape[0]
    return pl.pallas_call(
        _swa_sample_kernel,
        grid=(batch,),
        in_specs=[pl.BlockSpec(memory_space=pltpu.SMEM),
                  pl.BlockSpec((seq, SWA_WIDTH), lambda b: (b, COL_Q // SWA_WIDTH)),
                  pl.BlockSpec((seq, SWA_KV_WIDTH), lambda b: (b, COL_K // SWA_KV_WIDTH)),
                  pl.BlockSpec((seq, SWA_KV_WIDTH), lambda b: (b, COL_VV // SWA_KV_WIDTH)),
                  pl.BlockSpec((None, None, WINDOW, SWA_KV_WIDTH), lambda b: (layer, b, 0, 0)),
                  pl.BlockSpec((None, None, WINDOW, SWA_KV_WIDTH), lambda b: (layer, b, 0, 0))],
        out_specs=pl.BlockSpec((seq, SWA_WIDTH), lambda b: (b, 0)),
        out_shape=jax.ShapeDtypeStruct((m, SWA_WIDTH), BF16),
        compiler_params=_params("parallel"),
        name="swa_sample",
    )(sinks[layer], z, z, z, cache_k, cache_v)


def _memory_kv_kernel(mem_ref, g_ref, wk_ref, wv_ref, k_ref, v_ref, wkb_ref, wvb_ref):
    @pl.when(pl.program_id(1) == 0)
    def _():
        wkb_ref[...] = wk_ref[...].astype(BF16)
        wvb_ref[...] = wv_ref[...].astype(BF16)

    h = _rms(mem_ref[...], g_ref[...]).astype(BF16)
    k_ref[...] = jnp.dot(h, wkb_ref[...], preferred_element_type=F32)
    v_ref[...] = jnp.dot(h, wvb_ref[...], preferred_element_type=F32)


def memory_kv(mem, g_all, wk_all, wv_all):
    batch, tokens, d = mem.shape
    depth = wk_all.shape[0]
    shp = jax.ShapeDtypeStruct((depth, batch, tokens, MEM_WIDTH), F32)
    return pl.pallas_call(
        _memory_kv_kernel,
        grid=(depth, batch),
        in_specs=[pl.BlockSpec((None, tokens, d), lambda l, b: (b, 0, 0)),
                  pl.BlockSpec((None, 1, d), lambda l, b: (l, 0, 0)),
                  pl.BlockSpec((None, d, MEM_WIDTH), lambda l, b: (l, 0, 0)),
                  pl.BlockSpec((None, d, MEM_WIDTH), lambda l, b: (l, 0, 0))],
        out_specs=[pl.BlockSpec((None, None, tokens, MEM_WIDTH), lambda l, b: (l, b, 0, 0)),
                   pl.BlockSpec((None, None, tokens, MEM_WIDTH), lambda l, b: (l, b, 0, 0))],
        out_shape=[shp, shp],
        scratch_shapes=[pltpu.VMEM((d, MEM_WIDTH), BF16), pltpu.VMEM((d, MEM_WIDTH), BF16)],
        compiler_params=_params("arbitrary", "arbitrary"),
        name="memory_kv",
    )(mem, g_all, wk_all, wv_all)


def _head_cols(h):
    return slice(h * MEM_HEAD_DIM, (h + 1) * MEM_HEAD_DIM)


def _mem_attention(cq_ref, head_k, head_v, o_ref):
    scale = MEM_HEAD_DIM ** -0.5

    def scores(h):
        return lax.dot_general(cq_ref[:, _head_cols(h)], head_k(h).astype(BF16), (((1,), (1,)), ((), ())),
                               preferred_element_type=F32) * scale

    def attend(h, s):
        mx = jnp.max(s, axis=-1, keepdims=True)
        p = jnp.exp(s - mx)
        den = jnp.sum(p, axis=-1, keepdims=True)
        p = (p * (1.0 / den)).astype(BF16)
        o_ref[:, _head_cols(h)] = jnp.dot(p, head_v(h).astype(BF16),
                                          preferred_element_type=F32).astype(o_ref.dtype)

    _skewed(MEM_HEADS, scores, attend)


def _mem_attn_kernel(cq_ref, mk_ref, mv_ref, o_ref):
    _mem_attention(cq_ref, lambda h: mk_ref[:, _head_cols(h)], lambda h: mv_ref[:, _head_cols(h)], o_ref)


def mem_attn(z, mk_all, mv_all, layer, batch, seq, tq):
    m = z.shape[0]
    nqb = seq // tq
    tokens = mk_all.shape[2]
    kv_spec = pl.BlockSpec((None, None, tokens, MEM_WIDTH), lambda b, i: (layer, b, 0, 0))
    return pl.pallas_call(
        _mem_attn_kernel,
        grid=(batch, nqb),
        in_specs=[pl.BlockSpec((tq, MEM_WIDTH), lambda b, i: (b * nqb + i, COL_CQ // MEM_WIDTH)),
                  kv_spec, kv_spec],
        out_specs=pl.BlockSpec((tq, MEM_WIDTH), lambda b, i: (b * nqb + i, 0)),
        out_shape=jax.ShapeDtypeStruct((m, MEM_WIDTH), BF16),
        compiler_params=_params("parallel", "arbitrary"),
        name="mem_attn",
    )(z, mk_all, mv_all)


def _cache_head_copies(k_hbm, v_hbm, k_buf, v_buf, sems, layer, b, slot):
    copies = []
    for h in range(MEM_HEADS):
        copies.append(pltpu.make_async_copy(k_hbm.at[layer, b, :, h, :], k_buf.at[slot, h], sems.at[slot, h]))
        copies.append(pltpu.make_async_copy(v_hbm.at[layer, b, :, h, :], v_buf.at[slot, h],
                                            sems.at[slot, MEM_HEADS + h]))
    return copies


def _mem_attn_cache_kernel(cq_ref, k_hbm, v_hbm, o_ref, k_buf, v_buf, sems, *, layer):
    b = pl.program_id(0)
    slot = b % 2

    def copies(batch_index, into):
        return _cache_head_copies(k_hbm, v_hbm, k_buf, v_buf, sems, layer, batch_index, into)

    @pl.when(b == 0)
    def _():
        for c in copies(0, 0):
            c.start()

    @pl.when(b + 1 < pl.num_programs(0))
    def _():
        for c in copies(b + 1, 1 - slot):
            c.start()

    for c in copies(b, slot):
        c.wait()
    _mem_attention(cq_ref, lambda h: k_buf[slot, h], lambda h: v_buf[slot, h], o_ref)


def mem_attn_cache(z, k_cache, v_cache, layer, batch, seq):
    m = z.shape[0]
    tokens = k_cache.shape[2]
    buf = pltpu.VMEM((2, MEM_HEADS, tokens, MEM_HEAD_DIM), F32)
    return pl.pallas_call(
        functools.partial(_mem_attn_cache_kernel, layer=layer),
        grid=(batch,),
        in_specs=[pl.BlockSpec((seq, MEM_WIDTH), lambda b: (b, COL_CQ // MEM_WIDTH)),
                  pl.BlockSpec(memory_space=pl.ANY),
                  pl.BlockSpec(memory_space=pl.ANY)],
        out_specs=pl.BlockSpec((seq, MEM_WIDTH), lambda b: (b, 0)),
        out_shape=jax.ShapeDtypeStruct((m, MEM_WIDTH), BF16),
        scratch_shapes=[buf, buf, pltpu.SemaphoreType.DMA((2, 2 * MEM_HEADS))],
        compiler_params=_params("arbitrary"),
        name="mem_attn_cache",
    )(z, k_cache, v_cache)


def _merge_kernel(oa_ref, ob_ref, oc_ref, w_ref, g_ref, o_ref):
    d = o_ref.shape[1]
    ka, kb = GMLP_WIDTH, GMLP_WIDTH + SWA_WIDTH
    for c in range(d // MERGE_COLS):
        cols = slice(c * MERGE_COLS, (c + 1) * MERGE_COLS)

        def gate(k):
            return g_ref[:, k * d + c * MERGE_COLS:k * d + (c + 1) * MERGE_COLS].astype(F32)

        acc = gate(0) * jnp.dot(oa_ref[...], w_ref[0:ka, cols], preferred_element_type=F32)
        acc = acc + gate(1) * jnp.dot(ob_ref[...], w_ref[ka:kb, cols], preferred_element_type=F32)
        acc = acc + gate(2) * jnp.dot(oc_ref[...], w_ref[kb:, cols], preferred_element_type=F32)
        o_ref[:, cols] = acc.astype(o_ref.dtype)


def merge(z, oa, ob, oc, w_all, layer, tm=512):
    m = z.shape[0]
    k, d = w_all.shape[1:]
    return pl.pallas_call(
        _merge_kernel,
        grid=(m // tm,),
        in_specs=[pl.BlockSpec((tm, GMLP_WIDTH), lambda i: (i, 0)),
                  pl.BlockSpec((tm, SWA_WIDTH), lambda i: (i, 0)),
                  pl.BlockSpec((tm, MEM_WIDTH), lambda i: (i, 0)),
                  pl.BlockSpec((None, k, d), lambda i: (layer, 0, 0), pipeline_mode=pl.Buffered(1)),
                  pl.BlockSpec((tm, 3 * d), lambda i: (i, COL_GATE // (3 * d)))],
        out_specs=pl.BlockSpec((tm, d), lambda i: (i, 0)),
        out_shape=jax.ShapeDtypeStruct((m, d), BF16),
        compiler_params=_params("parallel"),
        name="merge",
    )(oa, ob, oc, w_all, z)


def _out_proj_kernel(mg_ref, w_ref, x_ref, gpost_ref, gnext_ref, xo_ref, ho_ref):
    def mix(s):
        return jnp.dot(mg_ref[_sub_rows(s, OUT_PROJ_SUB), :], w_ref[...], preferred_element_type=F32)

    def finish(s, mixed):
        rows = _sub_rows(s, OUT_PROJ_SUB)
        xn = x_ref[rows, :] + _rms(mixed, gpost_ref[...])
        xo_ref[rows, :] = xn
        ho_ref[rows, :] = _rms(xn, gnext_ref[...]).astype(ho_ref.dtype)

    _skewed(mg_ref.shape[0] // OUT_PROJ_SUB, mix, finish)


def out_proj(mg, w_all, x, gpost_all, gnext_all, layer, tm=512):
    m, d = x.shape
    return pl.pallas_call(
        _out_proj_kernel,
        grid=(m // tm,),
        in_specs=[pl.BlockSpec((tm, d), lambda i: (i, 0)),
                  pl.BlockSpec((None, d, d), lambda i: (layer, 0, 0), pipeline_mode=pl.Buffered(1)),
                  pl.BlockSpec((tm, d), lambda i: (i, 0)),
                  pl.BlockSpec((None, 1, d), lambda i: (layer, 0, 0)),
                  pl.BlockSpec((None, 1, d), lambda i: (layer, 0, 0))],
        out_specs=[pl.BlockSpec((tm, d), lambda i: (i, 0)),
                   pl.BlockSpec((tm, d), lambda i: (i, 0))],
        out_shape=[jax.ShapeDtypeStruct((m, d), F32), jax.ShapeDtypeStruct((m, d), BF16)],
        compiler_params=_params("parallel"),
        name="out_proj",
    )(mg, w_all, x, gpost_all, gnext_all)


def _ffn_kernel(h_ref, wu_ref, wd_ref, x_ref, gpost_ref, *rest, emit_next):
    if emit_next:
        gnext_ref, xo_ref, ho_ref = rest
    else:
        (xo_ref,) = rest
    j = pl.program_id(1)

    @pl.when(j == 0)
    def _():
        xo_ref[...] = jnp.zeros_like(xo_ref)

    def partial_sum(rows):
        a = jnp.dot(h_ref[rows, :], wu_ref[...], preferred_element_type=F32)
        a = jnp.square(jnp.maximum(a, 0.0)).astype(BF16)
        return jnp.dot(a, wd_ref[...], preferred_element_type=F32)

    last = pl.num_programs(1) - 1

    @pl.when(j < last)
    def _():
        xo_ref[...] += partial_sum(slice(None))

    @pl.when(j == last)
    def _():
        def finish(s, part):
            rows = _sub_rows(s, FFN_SUB)
            xn = x_ref[rows, :] + _rms(xo_ref[rows, :] + part, gpost_ref[...])
            xo_ref[rows, :] = xn
            if emit_next:
                ho_ref[rows, :] = _rms(xn, gnext_ref[...]).astype(ho_ref.dtype)

        _skewed(h_ref.shape[0] // FFN_SUB, lambda s: partial_sum(_sub_rows(s, FFN_SUB)), finish)


def ffn(h, wu, wd, x, gpost_all, gnext_all, layer, next_layer, tm=512, th=1024):
    m, d = x.shape
    hidden = wu.shape[-1]
    emit_next = next_layer is not None
    in_specs = [pl.BlockSpec((tm, d), lambda i, j: (i, 0)),
                pl.BlockSpec((d, th), lambda i, j: (0, j)),
                pl.BlockSpec((th, d), lambda i, j: (j, 0)),
                pl.BlockSpec((tm, d), lambda i, j: (i, 0)),
                pl.BlockSpec((None, 1, d), lambda i, j: (layer, 0, 0))]
    args = [h, wu, wd, x, gpost_all]
    out_specs = [pl.BlockSpec((tm, d), lambda i, j: (i, 0))]
    out_shape = [jax.ShapeDtypeStruct((m, d), F32)]
    if emit_next:
        in_specs.append(pl.BlockSpec((None, 1, d), lambda i, j: (next_layer, 0, 0)))
        args.append(gnext_all)
        out_specs.append(pl.BlockSpec((tm, d), lambda i, j: (i, 0)))
        out_shape.append(jax.ShapeDtypeStruct((m, d), BF16))
    res = pl.pallas_call(
        functools.partial(_ffn_kernel, emit_next=emit_next),
        grid=(m // tm, hidden // th),
        in_specs=in_specs,
        out_specs=out_specs,
        out_shape=out_shape,
        compiler_params=_params("parallel", "arbitrary"),
        name="ffn",
    )(*args)
    return (res[0], res[1]) if emit_next else (res[0], None)


def _ffn_up_cast_kernel(h_ref, wu_ref, a_ref, wub_ref):
    wub_ref[...] = wu_ref[...].astype(BF16)

    def up(s):
        return jnp.dot(h_ref[_sub_rows(s, FFN_CAST_SUB), :], wub_ref[...], preferred_element_type=F32)

    def finish(s, a):
        a_ref[_sub_rows(s, FFN_CAST_SUB), :] = jnp.square(jnp.maximum(a, 0.0)).astype(a_ref.dtype)

    _skewed(h_ref.shape[0] // FFN_CAST_SUB, up, finish)


def ffn_up_cast(h, wu_all, layer, th=1024):
    m, d = h.shape
    hidden = wu_all.shape[-1]
    return pl.pallas_call(
        _ffn_up_cast_kernel,
        grid=(hidden // th,),
        in_specs=[pl.BlockSpec((m, d), lambda j: (0, 0), pipeline_mode=pl.Buffered(1)),
                  pl.BlockSpec((None, d, th), lambda j: (layer, 0, j))],
        out_specs=[pl.BlockSpec((m, th), lambda j: (0, j)),
                   pl.BlockSpec((d, th), lambda j: (0, j))],
        out_shape=[jax.ShapeDtypeStruct((m, hidden), BF16), jax.ShapeDtypeStruct((d, hidden), BF16)],
        compiler_params=_params("parallel"),
        name="ffn_up_cast",
    )(h, wu_all)


def _ffn_down_cast_kernel(a_ref, wd_ref, acc_ref, wdb_ref):
    @pl.when(pl.program_id(0) == 0)
    def _():
        acc_ref[...] = jnp.zeros_like(acc_ref)

    wdb_ref[...] = wd_ref[...].astype(BF16)
    for s in range(a_ref.shape[0] // FFN_CAST_SUB):
        rows = _sub_rows(s, FFN_CAST_SUB)
        acc_ref[rows, :] += jnp.dot(a_ref[rows, :], wdb_ref[...], preferred_element_type=F32)


def ffn_down_cast(a, wd_all, layer, tk=512):
    m, hidden = a.shape
    d = wd_all.shape[-1]
    return pl.pallas_call(
        _ffn_down_cast_kernel,
        grid=(hidden // tk,),
        in_specs=[pl.BlockSpec((m, tk), lambda k: (0, k)),
                  pl.BlockSpec((None, tk, d), lambda k: (layer, k, 0))],
        out_specs=[pl.BlockSpec((m, d), lambda k: (0, 0)),
                   pl.BlockSpec((tk, d), lambda k: (k, 0))],
        out_shape=[jax.ShapeDtypeStruct((m, d), F32), jax.ShapeDtypeStruct((hidden, d), BF16)],
        compiler_params=_params("arbitrary"),
        name="ffn_down_cast",
    )(a, wd_all)


def _ffn_residual_kernel(acc_ref, x_ref, gpost_ref, *rest):
    xn = x_ref[...] + _rms(acc_ref[...], gpost_ref[...])
    if len(rest) == 3:
        gnext_ref, xo_ref, ho_ref = rest
        ho_ref[...] = _rms(xn, gnext_ref[...]).astype(ho_ref.dtype)
    else:
        (xo_ref,) = rest
    xo_ref[...] = xn


def ffn_residual(acc, x, gpost_all, gnext_all, layer, next_layer, tm=512):
    m, d = x.shape
    emit_next = next_layer is not None
    row_spec = pl.BlockSpec((tm, d), lambda i: (i, 0))
    in_specs = [row_spec, row_spec, pl.BlockSpec((None, 1, d), lambda i: (layer, 0, 0))]
    args = [acc, x, gpost_all]
    out_specs, out_shape = [row_spec], [jax.ShapeDtypeStruct((m, d), F32)]
    if emit_next:
        in_specs.append(pl.BlockSpec((None, 1, d), lambda i: (next_layer, 0, 0)))
        args.append(gnext_all)
        out_specs.append(row_spec)
        out_shape.append(jax.ShapeDtypeStruct((m, d), BF16))
    res = pl.pallas_call(
        _ffn_residual_kernel,
        grid=(m // tm,),
        in_specs=in_specs,
        out_specs=out_specs,
        out_shape=out_shape,
        compiler_params=_params("parallel"),
        name="ffn_residual",
    )(*args)
    return (res[0], res[1]) if emit_next else (res[0], None)


def kernel(x_prompt, x_sample, cache_swa_k, cache_swa_v, cache_mem_k, cache_mem_v, mem_prompt,
           w_in, ln_v_g, ln_v_b, w_s, b_s, sinks, mem_norm, w_mem_k, w_mem_v,
           w_pa, w_pb, w_pc, w_o, norm_mix_pre, norm_mix_post, norm_ffn_pre, norm_ffn_post,
           w_up, w_down):
    batch, seq, d = x_prompt.shape
    dec_batch, dec_seq, _ = x_sample.shape
    depth = w_in.shape[0]
    mem_tokens = mem_prompt.shape[1]

    w_p_b = jnp.concatenate([w_pa, w_pb, w_pc], axis=1).astype(BF16)
    w_o_b = w_o.astype(BF16)
    row = lambda g: g.reshape(depth, 1, g.shape[-1])
    g_mix_pre, g_mix_post = row(norm_mix_pre), row(norm_mix_post)
    g_ffn_pre, g_ffn_post = row(norm_ffn_pre), row(norm_ffn_post)
    g_mem, ln_g, ln_b = row(mem_norm), row(ln_v_g), row(ln_v_b)
    bias_p = jnp.repeat(jnp.swapaxes(b_s[:, :, :GMLP_CHUNK], 1, 2), GMLP_GROUP_DIM, axis=2)
    bias_s = jnp.repeat(jnp.swapaxes(b_s[:, :, :dec_seq], 1, 2), GMLP_GROUP_DIM, axis=2)
    proj_rows = 1024
    tab_p = rope_table(jnp.arange(seq, dtype=jnp.int32))
    tab_s = jnp.tile(rope_table(PAST_LEN + jnp.arange(dec_seq, dtype=jnp.int32)), (1, proj_rows // dec_seq, 1))
    cache_k = cache_swa_k.reshape(depth, dec_batch, WINDOW, SWA_KV_WIDTH)
    cache_v = cache_swa_v.reshape(depth, dec_batch, WINDOW, SWA_KV_WIDTH)

    mk_p, mv_p = memory_kv(mem_prompt, g_mem, w_mem_k, w_mem_v)

    xp = x_prompt.reshape(batch * seq, d)
    xs = x_sample.reshape(dec_batch * dec_seq, d)
    hp = rmsnorm_cast(xp, g_mix_pre, 0)
    hs = rmsnorm_cast(xs, g_mix_pre, 0)

    k_p, v_p, k_s, v_s, gv_s = [], [], [], [], []
    for l in range(depth):
        nxt = l + 1 if l + 1 < depth else None

        def merge_out(z, oa, ob, oc, x):
            mg = merge(z, oa, ob, oc, w_p_b, l)
            return out_proj(mg, w_o_b, x, g_mix_post, g_ffn_pre, l)

        z = proj_in(hs, w_in, tab_s, l, tm=proj_rows)
        oa, v_rows = gmlp(z, ln_g, ln_b, w_s, bias_s, l, dec_seq, True)
        ob = swa_sample(z, cache_k, cache_v, sinks, l, dec_batch, dec_seq)
        oc = mem_attn_cache(z, cache_mem_k, cache_mem_v, l, dec_batch, dec_seq)
        xs, h2 = merge_out(z, oa, ob, oc, xs)
        act, w_up_b = ffn_up_cast(h2, w_up, l)
        acc, w_down_b = ffn_down_cast(act, w_down, l)
        xs, hs = ffn_residual(acc, xs, g_ffn_post, g_mix_pre, l, nxt)
        k_s.append(z[:, COL_K:COL_VV].astype(F32).reshape(dec_batch, dec_seq, SWA_KV_HEADS, SWA_HEAD_DIM))
        v_s.append(z[:, COL_VV:COL_CQ].astype(F32).reshape(dec_batch, dec_seq, SWA_KV_HEADS, SWA_HEAD_DIM))
        gv_s.append(v_rows.reshape(dec_batch, dec_seq, GMLP_WIDTH))

        z = proj_in(hp, w_in, tab_p, l, tm=proj_rows)
        oa, _ = gmlp(z, ln_g, ln_b, w_s, bias_p, l, GMLP_CHUNK, False)
        ob = swa_prompt(z, sinks, l, batch, seq)
        oc = mem_attn(z, mk_p, mv_p, l, batch, seq, 512)
        xp, h2 = merge_out(z, oa, ob, oc, xp)
        xp, hp = ffn(h2, w_up_b, w_down_b, xp, g_ffn_post, g_mix_pre, l, nxt)
        kv_tail = z.reshape(batch, seq, -1)[:, -WINDOW:, COL_K:COL_CQ].astype(F32)
        k_p.append(kv_tail[..., :SWA_KV_WIDTH].reshape(batch, WINDOW, SWA_KV_HEADS, SWA_HEAD_DIM))
        v_p.append(kv_tail[..., SWA_KV_WIDTH:].reshape(batch, WINDOW, SWA_KV_HEADS, SWA_HEAD_DIM))

    mem_shape = (depth, batch, mem_tokens, MEM_HEADS, MEM_HEAD_DIM)
    return (xp.reshape(batch, seq, d), xs.reshape(dec_batch, dec_seq, d),
            jnp.stack(k_p), jnp.stack(v_p), mk_p.reshape(mem_shape), mv_p.reshape(mem_shape),
            jnp.stack(k_s), jnp.stack(v_s), jnp.stack(gv_s))
```

```python
import functools

import jax
import jax.numpy as jnp
from jax import lax
from jax.experimental import pallas as pl
from jax.experimental.pallas import tpu as pltpu

F32 = jnp.float32
BF16 = jnp.bfloat16

CHUNK = 64
GMLP_CHUNK = 128
GMLP_GROUPS = 12
GMLP_GROUP_DIM = 128
GMLP_WIDTH = GMLP_GROUPS * GMLP_GROUP_DIM
SWA_HEADS = 24
SWA_KV_HEADS = 4
SWA_HEAD_DIM = 64
SWA_WIDTH = SWA_HEADS * SWA_HEAD_DIM
SWA_KV_WIDTH = SWA_KV_HEADS * SWA_HEAD_DIM
WINDOW = 128
ROPE_THETA = 500000.0
ROPE_DIM = SWA_HEAD_DIM // 4
MEM_HEADS = 4
MEM_HEAD_DIM = 256
MEM_WIDTH = MEM_HEADS * MEM_HEAD_DIM
PAST_LEN = 1024
EPS = 1e-6
NEG = -1e30

COL_U = 0
COL_V = GMLP_WIDTH
COL_Q = 2 * GMLP_WIDTH
COL_K = COL_Q + SWA_WIDTH
COL_VV = COL_K + SWA_KV_WIDTH
COL_CQ = COL_VV + SWA_KV_WIDTH
COL_GATE = COL_CQ + MEM_WIDTH

LOG2E = 1.4426950408889634
Q_SCALE = SWA_HEAD_DIM ** -0.5 * LOG2E

LANES = 128
BF16_SUBLANES = 16
MERGE_COLS = 512
PROJ_SUB = 256
OUT_PROJ_SUB = 256
FFN_SUB = 256
FFN_CAST_SUB = 512
MERGE_CAST_SUB = 512
SWA_LOOKAHEAD = 3
HEADS_PER_LANE_TILE = LANES // SWA_HEAD_DIM
VMEM_LIMIT = 56 * 1024 * 1024


def _params(*sem):
    return pltpu.CompilerParams(dimension_semantics=sem, vmem_limit_bytes=VMEM_LIMIT)


def _rms(x, g):
    return x * lax.rsqrt(jnp.mean(x * x, axis=-1, keepdims=True) + EPS) * g


def _sub_rows(s, size):
    return slice(s * size, (s + 1) * size)


def _skewed(n, compute, finish):
    nxt = compute(0)
    for s in range(n):
        cur = nxt
        if s + 1 < n:
            nxt = compute(s + 1)
        finish(s, cur)


def _rmsnorm_kernel(x_ref, g_ref, o_ref):
    o_ref[...] = _rms(x_ref[...], g_ref[...]).astype(o_ref.dtype)


def rmsnorm_cast(x, g_all, layer, tm=512):
    m, d = x.shape
    return pl.pallas_call(
        _rmsnorm_kernel,
        grid=(m // tm,),
        in_specs=[pl.BlockSpec((tm, d), lambda i: (i, 0)),
                  pl.BlockSpec((None, 1, d), lambda i: (layer, 0, 0))],
        out_specs=pl.BlockSpec((tm, d), lambda i: (i, 0)),
        out_shape=jax.ShapeDtypeStruct((m, d), BF16),
        compiler_params=_params("parallel"),
        name="rmsnorm_cast",
    )(x, g_all)


def _rope(x, tab):
    return x * tab[0] + pltpu.roll(x, LANES - ROPE_DIM // 2, 1) * tab[1] + pltpu.roll(x, ROPE_DIM // 2, 1) * tab[2]


def rope_table(pos):
    half = ROPE_DIM // 2
    inv = ROPE_THETA ** (-jnp.arange(half, dtype=F32) / half)
    ang = pos.astype(F32)[:, None] * inv[None, :]
    cos, sin = jnp.cos(ang), jnp.sin(ang)
    s = pos.shape[0]
    rest = SWA_HEAD_DIM - ROPE_DIM
    c_head = jnp.concatenate([cos, cos, jnp.ones((s, rest), F32)], axis=1)
    up_head = jnp.concatenate([-sin, jnp.zeros((s, half + rest), F32)], axis=1)
    dn_head = jnp.concatenate([jnp.zeros((s, half), F32), sin, jnp.zeros((s, rest), F32)], axis=1)
    return jnp.stack([jnp.tile(t, (1, HEADS_PER_LANE_TILE)) for t in (c_head, up_head, dn_head)])


def _proj_in_kernel(h_ref, w_ref, tab_ref, o_ref, wb_ref, *, tn):
    j = pl.program_id(0)

    @pl.when(pl.program_id(1) == 0)
    def _():
        wb_ref[...] = w_ref[...].astype(BF16)

    def project(rows):
        return jnp.dot(h_ref[rows, :], wb_ref[...], preferred_element_type=F32)

    def rotary_tile(jj):
        def finish(acc, rows):
            for t in range(tn // LANES):
                col = jj * tn + t * LANES
                x = acc[:, t * LANES:(t + 1) * LANES]
                if col < COL_K:
                    x = _rope(x, tab_ref[:, rows, :]) * Q_SCALE
                elif col < COL_VV:
                    x = _rope(x, tab_ref[:, rows, :])
                o_ref[rows, t * LANES:(t + 1) * LANES] = x.astype(o_ref.dtype)
        return finish

    def plain(acc, rows):
        o_ref[rows, :] = acc.astype(o_ref.dtype)

    def gelu(acc, rows):
        o_ref[rows, :] = jax.nn.gelu(acc).astype(o_ref.dtype)

    def sigmoid(acc, rows):
        o_ref[rows, :] = jax.nn.sigmoid(acc).astype(o_ref.dtype)

    def run(finish):
        _skewed(h_ref.shape[0] // PROJ_SUB, lambda s: project(_sub_rows(s, PROJ_SUB)),
                lambda s, acc: finish(acc, _sub_rows(s, PROJ_SUB)))

    pl.when(j < COL_Q // tn)(lambda: run(gelu))
    for jj in range(COL_Q // tn, COL_GATE // tn):
        pl.when(j == jj)(functools.partial(run, rotary_tile(jj) if jj * tn < COL_VV else plain))
    pl.when(j >= COL_GATE // tn)(lambda: run(sigmoid))


def proj_in(h, w_all, tab, layer, tm=1024, tn=1024):
    m, d = h.shape
    n = w_all.shape[-1]
    tab_blocks = tab.shape[1] // tm

    def uses_tab(j):
        return jnp.logical_and(j >= COL_Q // tn, j * tn < COL_VV)

    return pl.pallas_call(
        functools.partial(_proj_in_kernel, tn=tn),
        grid=(n // tn, m // tm),
        in_specs=[pl.BlockSpec((tm, d), lambda j, i: (i, 0)),
                  pl.BlockSpec((None, d, tn), lambda j, i: (layer, 0, j)),
                  pl.BlockSpec((3, tm, LANES), lambda j, i: (0, jnp.where(uses_tab(j), i % tab_blocks, 0), 0))],
        out_specs=pl.BlockSpec((tm, tn), lambda j, i: (i, j)),
        out_shape=jax.ShapeDtypeStruct((m, n), BF16),
        scratch_shapes=[pltpu.VMEM((d, tn), BF16)],
        compiler_params=_params("arbitrary", "arbitrary"),
        name="proj_in",
    )(h, w_all, tab)


def _gmlp_kernel(u_ref, gv_ref, lng_ref, lnb_ref, ws_ref, bias_ref, o_ref, *v_refs, chunk):
    rows = u_ref.shape[0]
    n_chunks = rows // chunk
    gv = gv_ref[...].astype(F32)
    mu = jnp.mean(gv, axis=-1, keepdims=True)
    dev = gv - mu
    var = jnp.mean(dev * dev, axis=-1, keepdims=True)
    v = dev * lax.rsqrt(var + EPS) * lng_ref[...] + lnb_ref[...]
    if v_refs:
        v_refs[0][...] = v
    vb = v.astype(BF16)
    r_io = lax.broadcasted_iota(jnp.int32, (chunk, chunk), 0)
    c_io = lax.broadcasted_iota(jnp.int32, (chunk, chunk), 1)
    causal = c_io <= r_io
    for g in range(GMLP_GROUPS):
        cols = slice(g * GMLP_GROUP_DIM, (g + 1) * GMLP_GROUP_DIM)
        w = jnp.where(causal, ws_ref[g, :chunk, :chunk], 0.0).astype(BF16)
        vg = jnp.concatenate([vb[c * chunk:(c + 1) * chunk, cols] for c in range(n_chunks)], axis=1)
        s = jnp.dot(w, vg, preferred_element_type=F32)
        bias = bias_ref[:, cols]
        for c in range(n_chunks):
            rs = slice(c * chunk, (c + 1) * chunk)
            sc = s[:, c * GMLP_GROUP_DIM:(c + 1) * GMLP_GROUP_DIM] + bias
            o_ref[rs, cols] = (u_ref[rs, cols].astype(F32) * sc).astype(o_ref.dtype)


def gmlp(z, ln_g, ln_b, w_s, bias_tab, layer, chunk, emit_v, rows=512):
    m = z.shape[0]
    wdt = GMLP_WIDTH
    out_shape = [jax.ShapeDtypeStruct((m, wdt), BF16)]
    out_specs = [pl.BlockSpec((rows, wdt), lambda i: (i, 0))]
    if emit_v:
        out_shape.append(jax.ShapeDtypeStruct((m, wdt), F32))
        out_specs.append(pl.BlockSpec((rows, wdt), lambda i: (i, 0)))
    res = pl.pallas_call(
        functools.partial(_gmlp_kernel, chunk=chunk),
        grid=(m // rows,),
        in_specs=[pl.BlockSpec((rows, wdt), lambda i: (i, COL_U // wdt)),
                  pl.BlockSpec((rows, wdt), lambda i: (i, COL_V // wdt)),
                  pl.BlockSpec((None, 1, wdt), lambda i: (layer, 0, 0)),
                  pl.BlockSpec((None, 1, wdt), lambda i: (layer, 0, 0)),
                  pl.BlockSpec((None, GMLP_GROUPS, GMLP_CHUNK, GMLP_CHUNK), lambda i: (layer, 0, 0, 0)),
                  pl.BlockSpec((None, chunk, wdt), lambda i: (layer, 0, 0))],
        out_specs=out_specs,
        out_shape=out_shape,
        compiler_params=_params("parallel"),
        name="gmlp_c%d" % chunk,
    )(z, z, ln_g, ln_b, w_s, bias_tab)
    return res if emit_v else (res[0], None)


def _swa_core(q_ref, k_all, v_all, valid, sink_ref, o_ref):
    rows = q_ref.shape[0]
    nk = k_all.shape[0]
    tiles_per_group = SWA_HEADS // SWA_KV_HEADS // HEADS_PER_LANE_TILE
    low_half = lax.broadcasted_iota(jnp.int32, (nk, LANES), 1) < SWA_HEAD_DIM
    v_t = v_all.T
    ones_rows = jnp.ones((BF16_SUBLANES, nk), BF16)

    def operands(kvh):
        c0 = (kvh // HEADS_PER_LANE_TILE) * LANES
        kk = k_all[:, c0:c0 + LANES]
        kk_sw = pltpu.roll(kk, SWA_HEAD_DIM, 1)
        if kvh % HEADS_PER_LANE_TILE == 0:
            k_ext = [jnp.where(low_half, kk, 0.0), jnp.where(low_half, 0.0, kk_sw)]
        else:
            k_ext = [jnp.where(low_half, kk_sw, 0.0), jnp.where(low_half, 0.0, kk)]
        vt_h = jnp.concatenate([v_t[kvh * SWA_HEAD_DIM:(kvh + 1) * SWA_HEAD_DIM, :].astype(BF16), ones_rows],
                               axis=0)
        q_tiles = []
        for t in range(kvh * tiles_per_group, (kvh + 1) * tiles_per_group):
            qt = q_ref[:, t * LANES:(t + 1) * LANES]
            if rows < LANES:
                qt = jnp.concatenate([qt, jnp.zeros((LANES - rows, LANES), BF16)], axis=0)
            q_tiles.append(qt)
        q_stack = jnp.concatenate(q_tiles, axis=0)
        return [k.astype(BF16) for k in k_ext], vt_h, q_stack

    def scores(ops, e):
        return lax.dot_general(ops[0][e], ops[2], (((1,), (1,)), ((), ())),
                               preferred_element_type=F32)

    def attend(ops, kvh, e, s):
        probs, sink_p = [], []
        for t in range(tiles_per_group):
            sink = sink_ref[HEADS_PER_LANE_TILE * (kvh * tiles_per_group + t) + e] * LOG2E
            st = jnp.where(valid, s[:, t * LANES:(t + 1) * LANES], NEG)
            mx = jnp.maximum(jnp.max(st, axis=0, keepdims=True), sink)
            probs.append(jnp.exp2(st - mx).astype(BF16))
            sink_p.append(jnp.exp2(sink - mx))
        pv = jnp.dot(ops[1], jnp.concatenate(probs, axis=1), preferred_element_type=F32)
        den = pv[SWA_HEAD_DIM:SWA_HEAD_DIM + 1] + jnp.concatenate(sink_p, axis=1)
        return pv[:SWA_HEAD_DIM] * (1.0 / den)

    items = [(kvh, e) for kvh in range(SWA_KV_HEADS) for e in range(HEADS_PER_LANE_TILE)]
    ops_of, pending = {}, []

    def issue(idx):
        kvh, e = items[idx]
        if kvh not in ops_of:
            ops_of[kvh] = operands(kvh)
        pending.append(scores(ops_of[kvh], e))

    for idx in range(min(SWA_LOOKAHEAD, len(items))):
        issue(idx)
    halves = []
    for idx, (kvh, e) in enumerate(items):
        if idx + SWA_LOOKAHEAD < len(items):
            issue(idx + SWA_LOOKAHEAD)
        halves.append(attend(ops_of[kvh], kvh, e, pending.pop(0)))
        if e == HEADS_PER_LANE_TILE - 1:
            o_t = jnp.concatenate(halves, axis=0)
            halves = []
            for t in range(tiles_per_group):
                tile = kvh * tiles_per_group + t
                o_tile = o_t[:, t * LANES:(t + 1) * LANES].T
                o_ref[:, tile * LANES:(tile + 1) * LANES] = o_tile[:rows].astype(o_ref.dtype)


def _swa_prompt_kernel(sink_ref, q_ref, kc_ref, kp_ref, vc_ref, vp_ref, o_ref):
    tq = q_ref.shape[0]
    k_all = jnp.concatenate([kp_ref[...], kc_ref[...]], axis=0).astype(F32)
    v_all = jnp.concatenate([vp_ref[...], vc_ref[...]], axis=0).astype(F32)
    nk = k_all.shape[0]
    k_row = lax.broadcasted_iota(jnp.int32, (nk, tq), 0)
    k_chunk = k_row // CHUNK
    q_chunk = lax.broadcasted_iota(jnp.int32, (nk, tq), 1) // CHUNK
    first_valid = jnp.where(pl.program_id(1) == 0, WINDOW, 0)
    valid = (k_chunk >= q_chunk) & (k_chunk <= q_chunk + WINDOW // CHUNK) & (k_row >= first_valid)
    _swa_core(q_ref, k_all, v_all, valid, sink_ref, o_ref)


def swa_prompt(z, sinks, layer, batch, seq):
    tq = WINDOW
    nqb = seq // tq
    m = z.shape[0]

    def prev(i):
        return jnp.maximum(i - 1, 0)

    return pl.pallas_call(
        _swa_prompt_kernel,
        grid=(batch, nqb),
        in_specs=[pl.BlockSpec(memory_space=pltpu.SMEM),
                  pl.BlockSpec((tq, SWA_WIDTH), lambda b, i: (b * nqb + i, COL_Q // SWA_WIDTH)),
                  pl.BlockSpec((tq, SWA_KV_WIDTH), lambda b, i: (b * nqb + i, COL_K // SWA_KV_WIDTH)),
                  pl.BlockSpec((tq, SWA_KV_WIDTH), lambda b, i: (b * nqb + prev(i), COL_K // SWA_KV_WIDTH)),
                  pl.BlockSpec((tq, SWA_KV_WIDTH), lambda b, i: (b * nqb + i, COL_VV // SWA_KV_WIDTH)),
                  pl.BlockSpec((tq, SWA_KV_WIDTH), lambda b, i: (b * nqb + prev(i), COL_VV // SWA_KV_WIDTH))],
        out_specs=pl.BlockSpec((tq, SWA_WIDTH), lambda b, i: (b * nqb + i, 0)),
        out_shape=jax.ShapeDtypeStruct((m, SWA_WIDTH), BF16),
        compiler_params=_params("parallel", "arbitrary"),
        name="swa_prompt",
    )(sinks[layer], z, z, z, z, z)


def _swa_sample_kernel(sink_ref, q_ref, kn_ref, vn_ref, ck_ref, cv_ref, o_ref):
    tq = q_ref.shape[0]
    pad = jnp.zeros((WINDOW - tq, SWA_KV_WIDTH), F32)
    k_all = jnp.concatenate([ck_ref[...], kn_ref[...].astype(F32), pad], axis=0)
    v_all = jnp.concatenate([cv_ref[...], vn_ref[...].astype(F32), pad], axis=0)
    valid = lax.broadcasted_iota(jnp.int32, (k_all.shape[0], LANES), 0) < WINDOW + tq
    _swa_core(q_ref, k_all, v_all, valid, sink_ref, o_ref)


def swa_sample(z, cache_k, cache_v, sinks, layer, batch, seq):
    m = z.shape[0]
    return pl.pallas_call(
        _swa_sample_kernel,
        grid=(batch,),
        in_specs=[pl.BlockSpec(memory_space=pltpu.SMEM),
                  pl.BlockSpec((seq, SWA_WIDTH), lambda b: (b, COL_Q // SWA_WIDTH)),
                  pl.BlockSpec((seq, SWA_KV_WIDTH), lambda b: (b, COL_K // SWA_KV_WIDTH)),
                  pl.BlockSpec((seq, SWA_KV_WIDTH), lambda b: (b, COL_VV // SWA_KV_WIDTH)),
                  pl.BlockSpec((None, None, WINDOW, SWA_KV_WIDTH), lambda b: (layer, b, 0, 0)),
                  pl.BlockSpec((None, None, WINDOW, SWA_KV_WIDTH), lambda b: (layer, b, 0, 0))],
        out_specs=pl.BlockSpec((seq, SWA_WIDTH), lambda b: (b, 0)),
        out_shape=jax.ShapeDtypeStruct((m, SWA_WIDTH), BF16),
        compiler_params=_params("parallel"),
        name="swa_sample",
    )(sinks[layer], z, z, z, cache_k, cache_v)


def _memory_kv_kernel(mem_ref, g_ref, wk_ref, wv_ref, k_ref, v_ref, wkb_ref, wvb_ref):
    @pl.when(pl.program_id(1) == 0)
    def _():
        wkb_ref[...] = wk_ref[...].astype(BF16)
        wvb_ref[...] = wv_ref[...].astype(BF16)

    h = _rms(mem_ref[...], g_ref[...]).astype(BF16)
    k_ref[...] = jnp.dot(h, wkb_ref[...], preferred_element_type=F32)
    v_ref[...] = jnp.dot(h, wvb_ref[...], preferred_element_type=F32)


def memory_kv(mem, g_all, wk_all, wv_all):
    batch, tokens, d = mem.shape
    depth = wk_all.shape[0]
    shp = jax.ShapeDtypeStruct((depth, batch, tokens, MEM_WIDTH), F32)
    return pl.pallas_call(
        _memory_kv_kernel,
        grid=(depth, batch),
        in_specs=[pl.BlockSpec((None, tokens, d), lambda l, b: (b, 0, 0)),
                  pl.BlockSpec((None, 1, d), lambda l, b: (l, 0, 0)),
                  pl.BlockSpec((None, d, MEM_WIDTH), lambda l, b: (l, 0, 0)),
                  pl.BlockSpec((None, d, MEM_WIDTH), lambda l, b: (l, 0, 0))],
        out_specs=[pl.BlockSpec((None, None, tokens, MEM_WIDTH), lambda l, b: (l, b, 0, 0)),
                   pl.BlockSpec((None, None, tokens, MEM_WIDTH), lambda l, b: (l, b, 0, 0))],
        out_shape=[shp, shp],
        scratch_shapes=[pltpu.VMEM((d, MEM_WIDTH), BF16), pltpu.VMEM((d, MEM_WIDTH), BF16)],
        compiler_params=_params("arbitrary", "arbitrary"),
        name="memory_kv",
    )(mem, g_all, wk_all, wv_all)


def _head_cols(h):
    return slice(h * MEM_HEAD_DIM, (h + 1) * MEM_HEAD_DIM)


def _mem_attention(cq_ref, head_k, head_v, o_ref):
    scale = MEM_HEAD_DIM ** -0.5

    def scores(h):
        return lax.dot_general(cq_ref[:, _head_cols(h)], head_k(h).astype(BF16), (((1,), (1,)), ((), ())),
                               preferred_element_type=F32) * scale

    def attend(h, s):
        mx = jnp.max(s, axis=-1, keepdims=True)
        p = jnp.exp(s - mx)
        den = jnp.sum(p, axis=-1, keepdims=True)
        p = (p * (1.0 / den)).astype(BF16)
        o_ref[:, _head_cols(h)] = jnp.dot(p, head_v(h).astype(BF16),
                                          preferred_element_type=F32).astype(o_ref.dtype)

    _skewed(MEM_HEADS, scores, attend)


def _mem_attn_kernel(cq_ref, mk_ref, mv_ref, o_ref):
    _mem_attention(cq_ref, lambda h: mk_ref[:, _head_cols(h)], lambda h: mv_ref[:, _head_cols(h)], o_ref)


def mem_attn(z, mk_all, mv_all, layer, batch, seq, tq):
    m = z.shape[0]
    nqb = seq // tq
    tokens = mk_all.shape[2]
    kv_spec = pl.BlockSpec((None, None, tokens, MEM_WIDTH), lambda b, i: (layer, b, 0, 0))
    return pl.pallas_call(
        _mem_attn_kernel,
        grid=(batch, nqb),
        in_specs=[pl.BlockSpec((tq, MEM_WIDTH), lambda b, i: (b * nqb + i, COL_CQ // MEM_WIDTH)),
                  kv_spec, kv_spec],
        out_specs=pl.BlockSpec((tq, MEM_WIDTH), lambda b, i: (b * nqb + i, 0)),
        out_shape=jax.ShapeDtypeStruct((m, MEM_WIDTH), BF16),
        compiler_params=_params("parallel", "arbitrary"),
        name="mem_attn",
    )(z, mk_all, mv_all)


def _cache_head_copies(k_hbm, v_hbm, k_buf, v_buf, sems, layer, b, slot):
    copies = []
    for h in range(MEM_HEADS):
        copies.append(pltpu.make_async_copy(k_hbm.at[layer, b, :, h, :], k_buf.at[slot, h], sems.at[slot, h]))
        copies.append(pltpu.make_async_copy(v_hbm.at[layer, b, :, h, :], v_buf.at[slot, h],
                                            sems.at[slot, MEM_HEADS + h]))
    return copies


def _mem_attn_cache_kernel(cq_ref, k_hbm, v_hbm, o_ref, k_buf, v_buf, sems, *, layer):
    b = pl.program_id(0)
    slot = b % 2

    def copies(batch_index, into):
        return _cache_head_copies(k_hbm, v_hbm, k_buf, v_buf, sems, layer, batch_index, into)

    @pl.when(b == 0)
    def _():
        for c in copies(0, 0):
            c.start()

    @pl.when(b + 1 < pl.num_programs(0))
    def _():
        for c in copies(b + 1, 1 - slot):
            c.start()

    for c in copies(b, slot):
        c.wait()
    _mem_attention(cq_ref, lambda h: k_buf[slot, h], lambda h: v_buf[slot, h], o_ref)


def mem_attn_cache(z, k_cache, v_cache, layer, batch, seq):
    m = z.shape[0]
    tokens = k_cache.shape[2]
    buf = pltpu.VMEM((2, MEM_HEADS, tokens, MEM_HEAD_DIM), F32)
    return pl.pallas_call(
        functools.partial(_mem_attn_cache_kernel, layer=layer),
        grid=(batch,),
        in_specs=[pl.BlockSpec((seq, MEM_WIDTH), lambda b: (b, COL_CQ // MEM_WIDTH)),
                  pl.BlockSpec(memory_space=pl.ANY),
                  pl.BlockSpec(memory_space=pl.ANY)],
        out_specs=pl.BlockSpec((seq, MEM_WIDTH), lambda b: (b, 0)),
        out_shape=jax.ShapeDtypeStruct((m, MEM_WIDTH), BF16),
        scratch_shapes=[buf, buf, pltpu.SemaphoreType.DMA((2, 2 * MEM_HEADS))],
        compiler_params=_params("arbitrary"),
        name="mem_attn_cache",
    )(z, k_cache, v_cache)


def _merge_kernel(oa_ref, ob_ref, oc_ref, w_ref, g_ref, o_ref):
    d = o_ref.shape[1]
    ka, kb = GMLP_WIDTH, GMLP_WIDTH + SWA_WIDTH
    for c in range(d // MERGE_COLS):
        cols = slice(c * MERGE_COLS, (c + 1) * MERGE_COLS)

        def gate(k):
            return g_ref[:, k * d + c * MERGE_COLS:k * d + (c + 1) * MERGE_COLS].astype(F32)

        acc = gate(0) * jnp.dot(oa_ref[...], w_ref[0:ka, cols], preferred_element_type=F32)
        acc = acc + gate(1) * jnp.dot(ob_ref[...], w_ref[ka:kb, cols], preferred_element_type=F32)
        acc = acc + gate(2) * jnp.dot(oc_ref[...], w_ref[kb:, cols], preferred_element_type=F32)
        o_ref[:, cols] = acc.astype(o_ref.dtype)


def merge(z, oa, ob, oc, w, tm=512):
    m = z.shape[0]
    k, d = w.shape
    return pl.pallas_call(
        _merge_kernel,
        grid=(m // tm,),
        in_specs=[pl.BlockSpec((tm, GMLP_WIDTH), lambda i: (i, 0)),
                  pl.BlockSpec((tm, SWA_WIDTH), lambda i: (i, 0)),
                  pl.BlockSpec((tm, MEM_WIDTH), lambda i: (i, 0)),
                  pl.BlockSpec((k, d), lambda i: (0, 0), pipeline_mode=pl.Buffered(1)),
                  pl.BlockSpec((tm, 3 * d), lambda i: (i, COL_GATE // (3 * d)))],
        out_specs=pl.BlockSpec((tm, d), lambda i: (i, 0)),
        out_shape=jax.ShapeDtypeStruct((m, d), BF16),
        compiler_params=_params("parallel"),
        name="merge",
    )(oa, ob, oc, w, z)


def _merge_cast_kernel(oa_ref, ob_ref, oc_ref, wa_ref, wb_ref, wc_ref, wo_ref, ga_ref, gb_ref, gc_ref,
                       o_ref, wp_out_ref, wo_out_ref):
    ka, kb = GMLP_WIDTH, GMLP_WIDTH + SWA_WIDTH
    wp_out_ref[0:ka, :] = wa_ref[...].astype(BF16)
    wp_out_ref[ka:kb, :] = wb_ref[...].astype(BF16)
    wp_out_ref[kb:, :] = wc_ref[...].astype(BF16)
    wo_out_ref[...] = wo_ref[...].astype(BF16)
    for s in range(oa_ref.shape[0] // MERGE_CAST_SUB):
        rows = _sub_rows(s, MERGE_CAST_SUB)
        acc = ga_ref[rows, :].astype(F32) * jnp.dot(oa_ref[rows, :], wp_out_ref[0:ka, :],
                                                    preferred_element_type=F32)
        acc = acc + gb_ref[rows, :].astype(F32) * jnp.dot(ob_ref[rows, :], wp_out_ref[ka:kb, :],
                                                          preferred_element_type=F32)
        acc = acc + gc_ref[rows, :].astype(F32) * jnp.dot(oc_ref[rows, :], wp_out_ref[kb:, :],
                                                          preferred_element_type=F32)
        o_ref[rows, :] = acc.astype(o_ref.dtype)


def merge_cast(z, oa, ob, oc, wa_all, wb_all, wc_all, wo_all, layer, tc=256):
    m = z.shape[0]
    d = wa_all.shape[-1]
    g0 = COL_GATE // tc
    gs = d // tc
    resident = lambda width: pl.BlockSpec((m, width), lambda c: (0, 0), pipeline_mode=pl.Buffered(1))
    weight = lambda k: pl.BlockSpec((None, k, tc), lambda c: (layer, 0, c))
    return pl.pallas_call(
        _merge_cast_kernel,
        grid=(d // tc,),
        in_specs=[resident(GMLP_WIDTH), resident(SWA_WIDTH), resident(MEM_WIDTH),
                  weight(GMLP_WIDTH), weight(SWA_WIDTH), weight(MEM_WIDTH), weight(d),
                  pl.BlockSpec((m, tc), lambda c: (0, g0 + c)),
                  pl.BlockSpec((m, tc), lambda c: (0, g0 + gs + c)),
                  pl.BlockSpec((m, tc), lambda c: (0, g0 + 2 * gs + c))],
        out_specs=[pl.BlockSpec((m, tc), lambda c: (0, c)),
                   pl.BlockSpec((GMLP_WIDTH + SWA_WIDTH + MEM_WIDTH, tc), lambda c: (0, c)),
                   pl.BlockSpec((d, tc), lambda c: (0, c))],
        out_shape=[jax.ShapeDtypeStruct((m, d), BF16),
                   jax.ShapeDtypeStruct((GMLP_WIDTH + SWA_WIDTH + MEM_WIDTH, d), BF16),
                   jax.ShapeDtypeStruct((d, d), BF16)],
        compiler_params=_params("parallel"),
        name="merge_cast",
    )(oa, ob, oc, wa_all, wb_all, wc_all, wo_all, z, z, z)


def _out_proj_kernel(mg_ref, w_ref, x_ref, gpost_ref, gnext_ref, xo_ref, ho_ref):
    def mix(s):
        return jnp.dot(mg_ref[_sub_rows(s, OUT_PROJ_SUB), :], w_ref[...], preferred_element_type=F32)

    def finish(s, mixed):
        rows = _sub_rows(s, OUT_PROJ_SUB)
        xn = x_ref[rows, :] + _rms(mixed, gpost_ref[...])
        xo_ref[rows, :] = xn
        ho_ref[rows, :] = _rms(xn, gnext_ref[...]).astype(ho_ref.dtype)

    _skewed(mg_ref.shape[0] // OUT_PROJ_SUB, mix, finish)


def out_proj(mg, w, x, gpost_all, gnext_all, layer, tm=512):
    m, d = x.shape
    return pl.pallas_call(
        _out_proj_kernel,
        grid=(m // tm,),
        in_specs=[pl.BlockSpec((tm, d), lambda i: (i, 0)),
                  pl.BlockSpec((d, d), lambda i: (0, 0), pipeline_mode=pl.Buffered(1)),
                  pl.BlockSpec((tm, d), lambda i: (i, 0)),
                  pl.BlockSpec((None, 1, d), lambda i: (layer, 0, 0)),
                  pl.BlockSpec((None, 1, d), lambda i: (layer, 0, 0))],
        out_specs=[pl.BlockSpec((tm, d), lambda i: (i, 0)),
                   pl.BlockSpec((tm, d), lambda i: (i, 0))],
        out_shape=[jax.ShapeDtypeStruct((m, d), F32), jax.ShapeDtypeStruct((m, d), BF16)],
        compiler_params=_params("parallel"),
        name="out_proj",
    )(mg, w, x, gpost_all, gnext_all)


def _ffn_kernel(h_ref, wu_ref, wd_ref, x_ref, gpost_ref, *rest, emit_next):
    if emit_next:
        gnext_ref, xo_ref, ho_ref = rest
    else:
        (xo_ref,) = rest
    j = pl.program_id(1)

    @pl.when(j == 0)
    def _():
        xo_ref[...] = jnp.zeros_like(xo_ref)

    def partial_sum(rows):
        a = jnp.dot(h_ref[rows, :], wu_ref[...], preferred_element_type=F32)
        a = jnp.square(jnp.maximum(a, 0.0)).astype(BF16)
        return jnp.dot(a, wd_ref[...], preferred_element_type=F32)

    last = pl.num_programs(1) - 1

    @pl.when(j < last)
    def _():
        xo_ref[...] += partial_sum(slice(None))

    @pl.when(j == last)
    def _():
        def finish(s, part):
            rows = _sub_rows(s, FFN_SUB)
            xn = x_ref[rows, :] + _rms(xo_ref[rows, :] + part, gpost_ref[...])
            xo_ref[rows, :] = xn
            if emit_next:
                ho_ref[rows, :] = _rms(xn, gnext_ref[...]).astype(ho_ref.dtype)

        _skewed(h_ref.shape[0] // FFN_SUB, lambda s: partial_sum(_sub_rows(s, FFN_SUB)), finish)


def ffn(h, wu, wd, x, gpost_all, gnext_all, layer, next_layer, tm=512, th=1024):
    m, d = x.shape
    hidden = wu.shape[-1]
    emit_next = next_layer is not None
    in_specs = [pl.BlockSpec((tm, d), lambda i, j: (i, 0)),
                pl.BlockSpec((d, th), lambda i, j: (0, j)),
                pl.BlockSpec((th, d), lambda i, j: (j, 0)),
                pl.BlockSpec((tm, d), lambda i, j: (i, 0)),
                pl.BlockSpec((None, 1, d), lambda i, j: (layer, 0, 0))]
    args = [h, wu, wd, x, gpost_all]
    out_specs = [pl.BlockSpec((tm, d), lambda i, j: (i, 0))]
    out_shape = [jax.ShapeDtypeStruct((m, d), F32)]
    if emit_next:
        in_specs.append(pl.BlockSpec((None, 1, d), lambda i, j: (next_layer, 0, 0)))
        args.append(gnext_all)
        out_specs.append(pl.BlockSpec((tm, d), lambda i, j: (i, 0)))
        out_shape.append(jax.ShapeDtypeStruct((m, d), BF16))
    res = pl.pallas_call(
        functools.partial(_ffn_kernel, emit_next=emit_next),
        grid=(m // tm, hidden // th),
        in_specs=in_specs,
        out_specs=out_specs,
        out_shape=out_shape,
        compiler_params=_params("parallel", "arbitrary"),
        name="ffn",
    )(*args)
    return (res[0], res[1]) if emit_next else (res[0], None)


def _ffn_up_cast_kernel(h_ref, wu_ref, a_ref, wub_ref):
    wub_ref[...] = wu_ref[...].astype(BF16)

    def up(s):
        return jnp.dot(h_ref[_sub_rows(s, FFN_CAST_SUB), :], wub_ref[...], preferred_element_type=F32)

    def finish(s, a):
        a_ref[_sub_rows(s, FFN_CAST_SUB), :] = jnp.square(jnp.maximum(a, 0.0)).astype(a_ref.dtype)

    _skewed(h_ref.shape[0] // FFN_CAST_SUB, up, finish)


def ffn_up_cast(h, wu_all, layer, th=1024):
    m, d = h.shape
    hidden = wu_all.shape[-1]
    return pl.pallas_call(
        _ffn_up_cast_kernel,
        grid=(hidden // th,),
        in_specs=[pl.BlockSpec((m, d), lambda j: (0, 0), pipeline_mode=pl.Buffered(1)),
                  pl.BlockSpec((None, d, th), lambda j: (layer, 0, j))],
        out_specs=[pl.BlockSpec((m, th), lambda j: (0, j)),
                   pl.BlockSpec((d, th), lambda j: (0, j))],
        out_shape=[jax.ShapeDtypeStruct((m, hidden), BF16), jax.ShapeDtypeStruct((d, hidden), BF16)],
        compiler_params=_params("parallel"),
        name="ffn_up_cast",
    )(h, wu_all)


def _ffn_down_cast_kernel(a_ref, wd_ref, acc_ref, wdb_ref):
    @pl.when(pl.program_id(0) == 0)
    def _():
        acc_ref[...] = jnp.zeros_like(acc_ref)

    wdb_ref[...] = wd_ref[...].astype(BF16)
    for s in range(a_ref.shape[0] // FFN_CAST_SUB):
        rows = _sub_rows(s, FFN_CAST_SUB)
        acc_ref[rows, :] += jnp.dot(a_ref[rows, :], wdb_ref[...], preferred_element_type=F32)


def ffn_down_cast(a, wd_all, layer, tk=512):
    m, hidden = a.shape
    d = wd_all.shape[-1]
    return pl.pallas_call(
        _ffn_down_cast_kernel,
        grid=(hidden // tk,),
        in_specs=[pl.BlockSpec((m, tk), lambda k: (0, k)),
                  pl.BlockSpec((None, tk, d), lambda k: (layer, k, 0))],
        out_specs=[pl.BlockSpec((m, d), lambda k: (0, 0)),
                   pl.BlockSpec((tk, d), lambda k: (k, 0))],
        out_shape=[jax.ShapeDtypeStruct((m, d), F32), jax.ShapeDtypeStruct((hidden, d), BF16)],
        compiler_params=_params("arbitrary"),
        name="ffn_down_cast",
    )(a, wd_all)


def _ffn_residual_kernel(acc_ref, x_ref, gpost_ref, *rest):
    xn = x_ref[...] + _rms(acc_ref[...], gpost_ref[...])
    if len(rest) == 3:
        gnext_ref, xo_ref, ho_ref = rest
        ho_ref[...] = _rms(xn, gnext_ref[...]).astype(ho_ref.dtype)
    else:
        (xo_ref,) = rest
    xo_ref[...] = xn


def ffn_residual(acc, x, gpost_all, gnext_all, layer, next_layer, tm=512):
    m, d = x.shape
    emit_next = next_layer is not None
    row_spec = pl.BlockSpec((tm, d), lambda i: (i, 0))
    in_specs = [row_spec, row_spec, pl.BlockSpec((None, 1, d), lambda i: (layer, 0, 0))]
    args = [acc, x, gpost_all]
    out_specs, out_shape = [row_spec], [jax.ShapeDtypeStruct((m, d), F32)]
    if emit_next:
        in_specs.append(pl.BlockSpec((None, 1, d), lambda i: (next_layer, 0, 0)))
        args.append(gnext_all)
        out_specs.append(row_spec)
        out_shape.append(jax.ShapeDtypeStruct((m, d), BF16))
    res = pl.pallas_call(
        _ffn_residual_kernel,
        grid=(m // tm,),
        in_specs=in_specs,
        out_specs=out_specs,
        out_shape=out_shape,
        compiler_params=_params("parallel"),
        name="ffn_residual",
    )(*args)
    return (res[0], res[1]) if emit_next else (res[0], None)


def kernel(x_prompt, x_sample, cache_swa_k, cache_swa_v, cache_mem_k, cache_mem_v, mem_prompt,
           w_in, ln_v_g, ln_v_b, w_s, b_s, sinks, mem_norm, w_mem_k, w_mem_v,
           w_pa, w_pb, w_pc, w_o, norm_mix_pre, norm_mix_post, norm_ffn_pre, norm_ffn_post,
           w_up, w_down):
    batch, seq, d = x_prompt.shape
    dec_batch, dec_seq, _ = x_sample.shape
    depth = w_in.shape[0]
    mem_tokens = mem_prompt.shape[1]

    row = lambda g: g.reshape(depth, 1, g.shape[-1])
    g_mix_pre, g_mix_post = row(norm_mix_pre), row(norm_mix_post)
    g_ffn_pre, g_ffn_post = row(norm_ffn_pre), row(norm_ffn_post)
    g_mem, ln_g, ln_b = row(mem_norm), row(ln_v_g), row(ln_v_b)
    bias_p = jnp.repeat(jnp.swapaxes(b_s[:, :, :GMLP_CHUNK], 1, 2), GMLP_GROUP_DIM, axis=2)
    bias_s = jnp.repeat(jnp.swapaxes(b_s[:, :, :dec_seq], 1, 2), GMLP_GROUP_DIM, axis=2)
    proj_rows = 2048
    tab_p = rope_table(jnp.arange(seq, dtype=jnp.int32))
    tab_s = jnp.tile(rope_table(PAST_LEN + jnp.arange(dec_seq, dtype=jnp.int32)), (1, proj_rows // dec_seq, 1))
    cache_k = cache_swa_k.reshape(depth, dec_batch, WINDOW, SWA_KV_WIDTH)
    cache_v = cache_swa_v.reshape(depth, dec_batch, WINDOW, SWA_KV_WIDTH)

    mk_p, mv_p = memory_kv(mem_prompt, g_mem, w_mem_k, w_mem_v)

    xp = x_prompt.reshape(batch * seq, d)
    xs = x_sample.reshape(dec_batch * dec_seq, d)
    hp = rmsnorm_cast(xp, g_mix_pre, 0)
    hs = rmsnorm_cast(xs, g_mix_pre, 0)

    k_p, v_p, k_s, v_s, gv_s = [], [], [], [], []
    for l in range(depth):
        nxt = l + 1 if l + 1 < depth else None

        z = proj_in(hs, w_in, tab_s, l, tm=proj_rows)
        oa, v_rows = gmlp(z, ln_g, ln_b, w_s, bias_s, l, dec_seq, True)
        ob = swa_sample(z, cache_k, cache_v, sinks, l, dec_batch, dec_seq)
        oc = mem_attn_cache(z, cache_mem_k, cache_mem_v, l, dec_batch, dec_seq)
        mg, w_p_b, w_o_b = merge_cast(z, oa, ob, oc, w_pa, w_pb, w_pc, w_o, l)
        xs, h2 = out_proj(mg, w_o_b, xs, g_mix_post, g_ffn_pre, l)
        act, w_up_b = ffn_up_cast(h2, w_up, l)
        acc, w_down_b = ffn_down_cast(act, w_down, l)
        xs, hs = ffn_residual(acc, xs, g_ffn_post, g_mix_pre, l, nxt)
        k_s.append(z[:, COL_K:COL_VV].astype(F32).reshape(dec_batch, dec_seq, SWA_KV_HEADS, SWA_HEAD_DIM))
        v_s.append(z[:, COL_VV:COL_CQ].astype(F32).reshape(dec_batch, dec_seq, SWA_KV_HEADS, SWA_HEAD_DIM))
        gv_s.append(v_rows.reshape(dec_batch, dec_seq, GMLP_WIDTH))

        z = proj_in(hp, w_in, tab_p, l, tm=proj_rows)
        oa, _ = gmlp(z, ln_g, ln_b, w_s, bias_p, l, GMLP_CHUNK, False)
        ob = swa_prompt(z, sinks, l, batch, seq)
        oc = mem_attn(z, mk_p, mv_p, l, batch, seq, 512)
        mg = merge(z, oa, ob, oc, w_p_b)
        xp, h2 = out_proj(mg, w_o_b, xp, g_mix_post, g_ffn_pre, l)
        xp, hp = ffn(h2, w_up_b, w_down_b, xp, g_ffn_post, g_mix_pre, l, nxt)
        kv_tail = z.reshape(batch, seq, -1)[:, -WINDOW:, COL_K:COL_CQ].astype(F32)
        k_p.append(kv_tail[..., :SWA_KV_WIDTH].reshape(batch, WINDOW, SWA_KV_HEADS, SWA_HEAD_DIM))
        v_p.append(kv_tail[..., SWA_KV_WIDTH:].reshape(batch, WINDOW, SWA_KV_HEADS, SWA_HEAD_DIM))

    mem_shape = (depth, batch, mem_tokens, MEM_HEADS, MEM_HEAD_DIM)
    return (xp.reshape(batch, seq, d), xs.reshape(dec_batch, dec_seq, d),
            jnp.stack(k_p), jnp.stack(v_p), mk_p.reshape(mem_shape), mv_p.reshape(mem_shape),
            jnp.stack(k_s), jnp.stack(v_s), jnp.stack(gv_s))
```

```python
import functools

import jax
import jax.numpy as jnp
from jax import lax
from jax.experimental import pallas as pl
from jax.experimental.pallas import tpu as pltpu

F32 = jnp.float32
BF16 = jnp.bfloat16

CHUNK = 64
GMLP_CHUNK = 128
GMLP_GROUPS = 12
GMLP_GROUP_DIM = 128
GMLP_WIDTH = GMLP_GROUPS * GMLP_GROUP_DIM
SWA_HEADS = 24
SWA_KV_HEADS = 4
SWA_HEAD_DIM = 64
SWA_WIDTH = SWA_HEADS * SWA_HEAD_DIM
SWA_KV_WIDTH = SWA_KV_HEADS * SWA_HEAD_DIM
WINDOW = 128
ROPE_THETA = 500000.0
ROPE_DIM = SWA_HEAD_DIM // 4
MEM_HEADS = 4
MEM_HEAD_DIM = 256
MEM_WIDTH = MEM_HEADS * MEM_HEAD_DIM
PAST_LEN = 1024
EPS = 1e-6
NEG = -1e30

COL_U = 0
COL_V = GMLP_WIDTH
COL_Q = 2 * GMLP_WIDTH
COL_K = COL_Q + SWA_WIDTH
COL_VV = COL_K + SWA_KV_WIDTH
COL_CQ = COL_VV + SWA_KV_WIDTH
COL_GATE = COL_CQ + MEM_WIDTH

LOG2E = 1.4426950408889634
Q_SCALE = SWA_HEAD_DIM ** -0.5 * LOG2E

LANES = 128
BF16_SUBLANES = 16
MERGE_COLS = 512
PROJ_SUB = 256
OUT_PROJ_SUB = 256
FFN_SUB = 256
FFN_CAST_SUB = 512
MERGE_CAST_SUB = 512
SWA_LOOKAHEAD = 3
HEADS_PER_LANE_TILE = LANES // SWA_HEAD_DIM
VMEM_LIMIT = 56 * 1024 * 1024


def _params(*sem):
    return pltpu.CompilerParams(dimension_semantics=sem, vmem_limit_bytes=VMEM_LIMIT)


def _rms(x, g):
    return x * lax.rsqrt(jnp.mean(x * x, axis=-1, keepdims=True) + EPS) * g


def _sub_rows(s, size):
    return slice(s * size, (s + 1) * size)


def _skewed(n, compute, finish):
    nxt = compute(0)
    for s in range(n):
        cur = nxt
        if s + 1 < n:
            nxt = compute(s + 1)
        finish(s, cur)


def _rmsnorm_kernel(x_ref, g_ref, o_ref):
    o_ref[...] = _rms(x_ref[...], g_ref[...]).astype(o_ref.dtype)


def rmsnorm_cast(x, g_all, layer, tm=512):
    m, d = x.shape
    return pl.pallas_call(
        _rmsnorm_kernel,
        grid=(m // tm,),
        in_specs=[pl.BlockSpec((tm, d), lambda i: (i, 0)),
                  pl.BlockSpec((None, 1, d), lambda i: (layer, 0, 0))],
        out_specs=pl.BlockSpec((tm, d), lambda i: (i, 0)),
        out_shape=jax.ShapeDtypeStruct((m, d), BF16),
        compiler_params=_params("parallel"),
        name="rmsnorm_cast",
    )(x, g_all)


def _rope(x, tab):
    return x * tab[0] + pltpu.roll(x, LANES - ROPE_DIM // 2, 1) * tab[1] + pltpu.roll(x, ROPE_DIM // 2, 1) * tab[2]


def rope_table(pos):
    half = ROPE_DIM // 2
    inv = ROPE_THETA ** (-jnp.arange(half, dtype=F32) / half)
    ang = pos.astype(F32)[:, None] * inv[None, :]
    cos, sin = jnp.cos(ang), jnp.sin(ang)
    s = pos.shape[0]
    rest = SWA_HEAD_DIM - ROPE_DIM
    c_head = jnp.concatenate([cos, cos, jnp.ones((s, rest), F32)], axis=1)
    up_head = jnp.concatenate([-sin, jnp.zeros((s, half + rest), F32)], axis=1)
    dn_head = jnp.concatenate([jnp.zeros((s, half), F32), sin, jnp.zeros((s, rest), F32)], axis=1)
    return jnp.stack([jnp.tile(t, (1, HEADS_PER_LANE_TILE)) for t in (c_head, up_head, dn_head)])


def _proj_in_kernel(h_ref, w_ref, tab_ref, o_ref, wb_ref, *, tn):
    j = pl.program_id(0)

    @pl.when(pl.program_id(1) == 0)
    def _():
        wb_ref[...] = w_ref[...].astype(BF16)

    def project(rows):
        return jnp.dot(h_ref[rows, :], wb_ref[...], preferred_element_type=F32)

    def rotary_tile(jj):
        def finish(acc, rows):
            for t in range(tn // LANES):
                col = jj * tn + t * LANES
                x = acc[:, t * LANES:(t + 1) * LANES]
                if col < COL_K:
                    x = _rope(x, tab_ref[:, rows, :]) * Q_SCALE
                elif col < COL_VV:
                    x = _rope(x, tab_ref[:, rows, :])
                o_ref[rows, t * LANES:(t + 1) * LANES] = x.astype(o_ref.dtype)
        return finish

    def plain(acc, rows):
        o_ref[rows, :] = acc.astype(o_ref.dtype)

    def gelu(acc, rows):
        o_ref[rows, :] = jax.nn.gelu(acc).astype(o_ref.dtype)

    def sigmoid(acc, rows):
        o_ref[rows, :] = jax.nn.sigmoid(acc).astype(o_ref.dtype)

    def run(finish):
        _skewed(h_ref.shape[0] // PROJ_SUB, lambda s: project(_sub_rows(s, PROJ_SUB)),
                lambda s, acc: finish(acc, _sub_rows(s, PROJ_SUB)))

    pl.when(j < COL_Q // tn)(lambda: run(gelu))
    for jj in range(COL_Q // tn, COL_GATE // tn):
        pl.when(j == jj)(functools.partial(run, rotary_tile(jj) if jj * tn < COL_VV else plain))
    pl.when(j >= COL_GATE // tn)(lambda: run(sigmoid))


def proj_in(h, w_all, tab, layer, tm=1024, tn=1024):
    m, d = h.shape
    n = w_all.shape[-1]
    tab_blocks = tab.shape[1] // tm

    def uses_tab(j):
        return jnp.logical_and(j >= COL_Q // tn, j * tn < COL_VV)

    return pl.pallas_call(
        functools.partial(_proj_in_kernel, tn=tn),
        grid=(n // tn, m // tm),
        in_specs=[pl.BlockSpec((tm, d), lambda j, i: (i, 0)),
                  pl.BlockSpec((None, d, tn), lambda j, i: (layer, 0, j)),
                  pl.BlockSpec((3, tm, LANES), lambda j, i: (0, jnp.where(uses_tab(j), i % tab_blocks, 0), 0))],
        out_specs=pl.BlockSpec((tm, tn), lambda j, i: (i, j)),
        out_shape=jax.ShapeDtypeStruct((m, n), BF16),
        scratch_shapes=[pltpu.VMEM((d, tn), BF16)],
        compiler_params=_params("arbitrary", "arbitrary"),
        name="proj_in",
    )(h, w_all, tab)


def _gmlp_kernel(u_ref, gv_ref, lng_ref, lnb_ref, ws_ref, bias_ref, o_ref, *v_refs, chunk):
    rows = u_ref.shape[0]
    n_chunks = rows // chunk
    gv = gv_ref[...].astype(F32)
    mu = jnp.mean(gv, axis=-1, keepdims=True)
    dev = gv - mu
    var = jnp.mean(dev * dev, axis=-1, keepdims=True)
    v = dev * lax.rsqrt(var + EPS) * lng_ref[...] + lnb_ref[...]
    if v_refs:
        v_refs[0][...] = v
    vb = v.astype(BF16)
    r_io = lax.broadcasted_iota(jnp.int32, (chunk, chunk), 0)
    c_io = lax.broadcasted_iota(jnp.int32, (chunk, chunk), 1)
    causal = c_io <= r_io
    for g in range(GMLP_GROUPS):
        cols = slice(g * GMLP_GROUP_DIM, (g + 1) * GMLP_GROUP_DIM)
        w = jnp.where(causal, ws_ref[g, :chunk, :chunk], 0.0).astype(BF16)
        vg = jnp.concatenate([vb[c * chunk:(c + 1) * chunk, cols] for c in range(n_chunks)], axis=1)
        s = jnp.dot(w, vg, preferred_element_type=F32)
        bias = bias_ref[:, cols]
        for c in range(n_chunks):
            rs = slice(c * chunk, (c + 1) * chunk)
            sc = s[:, c * GMLP_GROUP_DIM:(c + 1) * GMLP_GROUP_DIM] + bias
            o_ref[rs, cols] = (u_ref[rs, cols].astype(F32) * sc).astype(o_ref.dtype)


def gmlp(z, ln_g, ln_b, w_s, bias_tab, layer, chunk, emit_v, rows=512):
    m = z.shape[0]
    wdt = GMLP_WIDTH
    out_shape = [jax.ShapeDtypeStruct((m, wdt), BF16)]
    out_specs = [pl.BlockSpec((rows, wdt), lambda i: (i, 0))]
    if emit_v:
        out_shape.append(jax.ShapeDtypeStruct((m, wdt), F32))
        out_specs.append(pl.BlockSpec((rows, wdt), lambda i: (i, 0)))
    res = pl.pallas_call(
        functools.partial(_gmlp_kernel, chunk=chunk),
        grid=(m // rows,),
        in_specs=[pl.BlockSpec((rows, wdt), lambda i: (i, COL_U // wdt)),
                  pl.BlockSpec((rows, wdt), lambda i: (i, COL_V // wdt)),
                  pl.BlockSpec((None, 1, wdt), lambda i: (layer, 0, 0)),
                  pl.BlockSpec((None, 1, wdt), lambda i: (layer, 0, 0)),
                  pl.BlockSpec((None, GMLP_GROUPS, GMLP_CHUNK, GMLP_CHUNK), lambda i: (layer, 0, 0, 0)),
                  pl.BlockSpec((None, chunk, wdt), lambda i: (layer, 0, 0))],
        out_specs=out_specs,
        out_shape=out_shape,
        compiler_params=_params("parallel"),
        name="gmlp_c%d" % chunk,
    )(z, z, ln_g, ln_b, w_s, bias_tab)
    return res if emit_v else (res[0], None)


def _swa_core(q_ref, k_all, v_all, valid, sink_ref, o_ref):
    rows = q_ref.shape[0]
    nk = k_all.shape[0]
    tiles_per_group = SWA_HEADS // SWA_KV_HEADS // HEADS_PER_LANE_TILE
    low_half = lax.broadcasted_iota(jnp.int32, (nk, LANES), 1) < SWA_HEAD_DIM
    v_t = v_all.T
    ones_rows = jnp.ones((BF16_SUBLANES, nk), BF16)

    def operands(kvh):
        c0 = (kvh // HEADS_PER_LANE_TILE) * LANES
        kk = k_all[:, c0:c0 + LANES]
        kk_sw = pltpu.roll(kk, SWA_HEAD_DIM, 1)
        if kvh % HEADS_PER_LANE_TILE == 0:
            k_ext = [jnp.where(low_half, kk, 0.0), jnp.where(low_half, 0.0, kk_sw)]
        else:
            k_ext = [jnp.where(low_half, kk_sw, 0.0), jnp.where(low_half, 0.0, kk)]
        vt_h = jnp.concatenate([v_t[kvh * SWA_HEAD_DIM:(kvh + 1) * SWA_HEAD_DIM, :].astype(BF16), ones_rows],
                               axis=0)
        q_tiles = []
        for t in range(kvh * tiles_per_group, (kvh + 1) * tiles_per_group):
            qt = q_ref[:, t * LANES:(t + 1) * LANES]
            if rows < LANES:
                qt = jnp.concatenate([qt, jnp.zeros((LANES - rows, LANES), BF16)], axis=0)
            q_tiles.append(qt)
        q_stack = jnp.concatenate(q_tiles, axis=0)
        return [k.astype(BF16) for k in k_ext], vt_h, q_stack

    def scores(ops, e):
        return lax.dot_general(ops[0][e], ops[2], (((1,), (1,)), ((), ())),
                               preferred_element_type=F32)

    def attend(ops, kvh, e, s):
        probs, sink_p = [], []
        for t in range(tiles_per_group):
            sink = sink_ref[HEADS_PER_LANE_TILE * (kvh * tiles_per_group + t) + e] * LOG2E
            st = jnp.where(valid, s[:, t * LANES:(t + 1) * LANES], NEG)
            mx = jnp.maximum(jnp.max(st, axis=0, keepdims=True), sink)
            probs.append(jnp.exp2(st - mx).astype(BF16))
            sink_p.append(jnp.exp2(sink - mx))
        pv = jnp.dot(ops[1], jnp.concatenate(probs, axis=1), preferred_element_type=F32)
        den = pv[SWA_HEAD_DIM:SWA_HEAD_DIM + 1] + jnp.concatenate(sink_p, axis=1)
        return pv[:SWA_HEAD_DIM] * (1.0 / den)

    items = [(kvh, e) for kvh in range(SWA_KV_HEADS) for e in range(HEADS_PER_LANE_TILE)]
    ops_of, pending = {}, []

    def issue(idx):
        kvh, e = items[idx]
        if kvh not in ops_of:
            ops_of[kvh] = operands(kvh)
        pending.append(scores(ops_of[kvh], e))

    for idx in range(min(SWA_LOOKAHEAD, len(items))):
        issue(idx)
    halves = []
    for idx, (kvh, e) in enumerate(items):
        if idx + SWA_LOOKAHEAD < len(items):
            issue(idx + SWA_LOOKAHEAD)
        halves.append(attend(ops_of[kvh], kvh, e, pending.pop(0)))
        if e == HEADS_PER_LANE_TILE - 1:
            o_t = jnp.concatenate(halves, axis=0)
            halves = []
            for t in range(tiles_per_group):
                tile = kvh * tiles_per_group + t
                o_tile = o_t[:, t * LANES:(t + 1) * LANES].T
                o_ref[:, tile * LANES:(tile + 1) * LANES] = o_tile[:rows].astype(o_ref.dtype)


def _swa_prompt_kernel(sink_ref, q_ref, kc_ref, kp_ref, vc_ref, vp_ref, o_ref):
    tq = q_ref.shape[0]
    k_all = jnp.concatenate([kp_ref[...], kc_ref[...]], axis=0).astype(F32)
    v_all = jnp.concatenate([vp_ref[...], vc_ref[...]], axis=0).astype(F32)
    nk = k_all.shape[0]
    k_row = lax.broadcasted_iota(jnp.int32, (nk, tq), 0)
    k_chunk = k_row // CHUNK
    q_chunk = lax.broadcasted_iota(jnp.int32, (nk, tq), 1) // CHUNK
    first_valid = jnp.where(pl.program_id(1) == 0, WINDOW, 0)
    valid = (k_chunk >= q_chunk) & (k_chunk <= q_chunk + WINDOW // CHUNK) & (k_row >= first_valid)
    _swa_core(q_ref, k_all, v_all, valid, sink_ref, o_ref)


def swa_prompt(z, sinks, layer, batch, seq):
    tq = WINDOW
    nqb = seq // tq
    m = z.shape[0]

    def prev(i):
        return jnp.maximum(i - 1, 0)

    return pl.pallas_call(
        _swa_prompt_kernel,
        grid=(batch, nqb),
        in_specs=[pl.BlockSpec(memory_space=pltpu.SMEM),
                  pl.BlockSpec((tq, SWA_WIDTH), lambda b, i: (b * nqb + i, COL_Q // SWA_WIDTH)),
                  pl.BlockSpec((tq, SWA_KV_WIDTH), lambda b, i: (b * nqb + i, COL_K // SWA_KV_WIDTH)),
                  pl.BlockSpec((tq, SWA_KV_WIDTH), lambda b, i: (b * nqb + prev(i), COL_K // SWA_KV_WIDTH)),
                  pl.BlockSpec((tq, SWA_KV_WIDTH), lambda b, i: (b * nqb + i, COL_VV // SWA_KV_WIDTH)),
                  pl.BlockSpec((tq, SWA_KV_WIDTH), lambda b, i: (b * nqb + prev(i), COL_VV // SWA_KV_WIDTH))],
        out_specs=pl.BlockSpec((tq, SWA_WIDTH), lambda b, i: (b * nqb + i, 0)),
        out_shape=jax.ShapeDtypeStruct((m, SWA_WIDTH), BF16),
        compiler_params=_params("parallel", "arbitrary"),
        name="swa_prompt",
    )(sinks[layer], z, z, z, z, z)


def _swa_sample_kernel(sink_ref, q_ref, kn_ref, vn_ref, ck_ref, cv_ref, o_ref):
    tq = q_ref.shape[0]
    pad = jnp.zeros((WINDOW - tq, SWA_KV_WIDTH), F32)
    k_all = jnp.concatenate([ck_ref[...], kn_ref[...].astype(F32), pad], axis=0)
    v_all = jnp.concatenate([cv_ref[...], vn_ref[...].astype(F32), pad], axis=0)
    valid = lax.broadcasted_iota(jnp.int32, (k_all.shape[0], LANES), 0) < WINDOW + tq
    _swa_core(q_ref, k_all, v_all, valid, sink_ref, o_ref)


def swa_sample(z, cache_k, cache_v, sinks, layer, batch, seq):
    m = z.shape[0]
    return pl.pallas_call(
        _swa_sample_kernel,
        grid=(batch,),
        in_specs=[pl.BlockSpec(memory_space=pltpu.SMEM),
                  pl.BlockSpec((seq, SWA_WIDTH), lambda b: (b, COL_Q // SWA_WIDTH)),
                  pl.BlockSpec((seq, SWA_KV_WIDTH), lambda b: (b, COL_K // SWA_KV_WIDTH)),
                  pl.BlockSpec((seq, SWA_KV_WIDTH), lambda b: (b, COL_VV // SWA_KV_WIDTH)),
                  pl.BlockSpec((None, None, WINDOW, SWA_KV_WIDTH), lambda b: (layer, b, 0, 0)),
                  pl.BlockSpec((None, None, WINDOW, SWA_KV_WIDTH), lambda b: (layer, b, 0, 0))],
        out_specs=pl.BlockSpec((seq, SWA_WIDTH), lambda b: (b, 0)),
        out_shape=jax.ShapeDtypeStruct((m, SWA_WIDTH), BF16),
        compiler_params=_params("parallel"),
        name="swa_sample",
    )(sinks[layer], z, z, z, cache_k, cache_v)


def _memory_kv_kernel(mem_ref, g_ref, wk_ref, wv_ref, k_ref, v_ref, wkb_ref, wvb_ref):
    @pl.when(pl.program_id(1) == 0)
    def _():
        wkb_ref[...] = wk_ref[...].astype(BF16)
        wvb_ref[...] = wv_ref[...].astype(BF16)

    h = _rms(mem_ref[...], g_ref[...]).astype(BF16)
    k_ref[...] = jnp.dot(h, wkb_ref[...], preferred_element_type=F32)
    v_ref[...] = jnp.dot(h, wvb_ref[...], preferred_element_type=F32)


def memory_kv(mem, g_all, wk_all, wv_all):
    batch, tokens, d = mem.shape
    depth = wk_all.shape[0]
    shp = jax.ShapeDtypeStruct((depth, batch, tokens, MEM_WIDTH), F32)
    return pl.pallas_call(
        _memory_kv_kernel,
        grid=(depth, batch),
        in_specs=[pl.BlockSpec((None, tokens, d), lambda l, b: (b, 0, 0)),
                  pl.BlockSpec((None, 1, d), lambda l, b: (l, 0, 0)),
                  pl.BlockSpec((None, d, MEM_WIDTH), lambda l, b: (l, 0, 0)),
                  pl.BlockSpec((None, d, MEM_WIDTH), lambda l, b: (l, 0, 0))],
        out_specs=[pl.BlockSpec((None, None, tokens, MEM_WIDTH), lambda l, b: (l, b, 0, 0)),
                   pl.BlockSpec((None, None, tokens, MEM_WIDTH), lambda l, b: (l, b, 0, 0))],
        out_shape=[shp, shp],
        scratch_shapes=[pltpu.VMEM((d, MEM_WIDTH), BF16), pltpu.VMEM((d, MEM_WIDTH), BF16)],
        compiler_params=_params("arbitrary", "arbitrary"),
        name="memory_kv",
    )(mem, g_all, wk_all, wv_all)


def _head_cols(h):
    return slice(h * MEM_HEAD_DIM, (h + 1) * MEM_HEAD_DIM)


def _mem_attention(cq_ref, head_k, head_v, o_ref):
    scale = MEM_HEAD_DIM ** -0.5

    def scores(h):
        return lax.dot_general(cq_ref[:, _head_cols(h)], head_k(h).astype(BF16), (((1,), (1,)), ((), ())),
                               preferred_element_type=F32) * scale

    def attend(h, s):
        mx = jnp.max(s, axis=-1, keepdims=True)
        p = jnp.exp(s - mx)
        den = jnp.sum(p, axis=-1, keepdims=True)
        p = (p * (1.0 / den)).astype(BF16)
        o_ref[:, _head_cols(h)] = jnp.dot(p, head_v(h).astype(BF16),
                                          preferred_element_type=F32).astype(o_ref.dtype)

    _skewed(MEM_HEADS, scores, attend)


def _mem_attn_kernel(cq_ref, mk_ref, mv_ref, o_ref):
    _mem_attention(cq_ref, lambda h: mk_ref[:, _head_cols(h)], lambda h: mv_ref[:, _head_cols(h)], o_ref)


def mem_attn(z, mk_all, mv_all, layer, batch, seq, tq):
    m = z.shape[0]
    nqb = seq // tq
    tokens = mk_all.shape[2]
    kv_spec = pl.BlockSpec((None, None, tokens, MEM_WIDTH), lambda b, i: (layer, b, 0, 0))
    return pl.pallas_call(
        _mem_attn_kernel,
        grid=(batch, nqb),
        in_specs=[pl.BlockSpec((tq, MEM_WIDTH), lambda b, i: (b * nqb + i, COL_CQ // MEM_WIDTH)),
                  kv_spec, kv_spec],
        out_specs=pl.BlockSpec((tq, MEM_WIDTH), lambda b, i: (b * nqb + i, 0)),
        out_shape=jax.ShapeDtypeStruct((m, MEM_WIDTH), BF16),
        compiler_params=_params("parallel", "arbitrary"),
        name="mem_attn",
    )(z, mk_all, mv_all)


def _cache_head_copies(k_hbm, v_hbm, k_buf, v_buf, sems, layer, b, slot):
    copies = []
    for h in range(MEM_HEADS):
        copies.append(pltpu.make_async_copy(k_hbm.at[layer, b, :, h, :], k_buf.at[slot, h], sems.at[slot, h]))
        copies.append(pltpu.make_async_copy(v_hbm.at[layer, b, :, h, :], v_buf.at[slot, h],
                                            sems.at[slot, MEM_HEADS + h]))
    return copies


def _mem_attn_cache_kernel(cq_ref, k_hbm, v_hbm, o_ref, k_buf, v_buf, sems, *, layer):
    b = pl.program_id(0)
    slot = b % 2

    def copies(batch_index, into):
        return _cache_head_copies(k_hbm, v_hbm, k_buf, v_buf, sems, layer, batch_index, into)

    @pl.when(b == 0)
    def _():
        for c in copies(0, 0):
            c.start()

    @pl.when(b + 1 < pl.num_programs(0))
    def _():
        for c in copies(b + 1, 1 - slot):
            c.start()

    for c in copies(b, slot):
        c.wait()
    _mem_attention(cq_ref, lambda h: k_buf[slot, h], lambda h: v_buf[slot, h], o_ref)


def mem_attn_cache(z, k_cache, v_cache, layer, batch, seq):
    m = z.shape[0]
    tokens = k_cache.shape[2]
    buf = pltpu.VMEM((2, MEM_HEADS, tokens, MEM_HEAD_DIM), F32)
    return pl.pallas_call(
        functools.partial(_mem_attn_cache_kernel, layer=layer),
        grid=(batch,),
        in_specs=[pl.BlockSpec((seq, MEM_WIDTH), lambda b: (b, COL_CQ // MEM_WIDTH)),
                  pl.BlockSpec(memory_space=pl.ANY),
                  pl.BlockSpec(memory_space=pl.ANY)],
        out_specs=pl.BlockSpec((seq, MEM_WIDTH), lambda b: (b, 0)),
        out_shape=jax.ShapeDtypeStruct((m, MEM_WIDTH), BF16),
        scratch_shapes=[buf, buf, pltpu.SemaphoreType.DMA((2, 2 * MEM_HEADS))],
        compiler_params=_params("arbitrary"),
        name="mem_attn_cache",
    )(z, k_cache, v_cache)


def _merge_kernel(oa_ref, ob_ref, oc_ref, w_ref, g_ref, o_ref):
    d = o_ref.shape[1]
    ka, kb = GMLP_WIDTH, GMLP_WIDTH + SWA_WIDTH
    for c in range(d // MERGE_COLS):
        cols = slice(c * MERGE_COLS, (c + 1) * MERGE_COLS)

        def gate(k):
            return g_ref[:, k * d + c * MERGE_COLS:k * d + (c + 1) * MERGE_COLS].astype(F32)

        acc = gate(0) * jnp.dot(oa_ref[...], w_ref[0:ka, cols], preferred_element_type=F32)
        acc = acc + gate(1) * jnp.dot(ob_ref[...], w_ref[ka:kb, cols], preferred_element_type=F32)
        acc = acc + gate(2) * jnp.dot(oc_ref[...], w_ref[kb:, cols], preferred_element_type=F32)
        o_ref[:, cols] = acc.astype(o_ref.dtype)


def merge(z, oa, ob, oc, w, tm=512):
    m = z.shape[0]
    k, d = w.shape
    return pl.pallas_call(
        _merge_kernel,
        grid=(m // tm,),
        in_specs=[pl.BlockSpec((tm, GMLP_WIDTH), lambda i: (i, 0)),
                  pl.BlockSpec((tm, SWA_WIDTH), lambda i: (i, 0)),
                  pl.BlockSpec((tm, MEM_WIDTH), lambda i: (i, 0)),
                  pl.BlockSpec((k, d), lambda i: (0, 0), pipeline_mode=pl.Buffered(1)),
                  pl.BlockSpec((tm, 3 * d), lambda i: (i, COL_GATE // (3 * d)))],
        out_specs=pl.BlockSpec((tm, d), lambda i: (i, 0)),
        out_shape=jax.ShapeDtypeStruct((m, d), BF16),
        compiler_params=_params("parallel"),
        name="merge",
    )(oa, ob, oc, w, z)


def _merge_cast_kernel(oa_ref, ob_ref, oc_ref, wa_ref, wb_ref, wc_ref, wo_ref, ga_ref, gb_ref, gc_ref,
                       o_ref, wp_out_ref, wo_out_ref):
    ka, kb = GMLP_WIDTH, GMLP_WIDTH + SWA_WIDTH
    wp_out_ref[0:ka, :] = wa_ref[...].astype(BF16)
    wp_out_ref[ka:kb, :] = wb_ref[...].astype(BF16)
    wp_out_ref[kb:, :] = wc_ref[...].astype(BF16)
    wo_out_ref[...] = wo_ref[...].astype(BF16)
    for s in range(oa_ref.shape[0] // MERGE_CAST_SUB):
        rows = _sub_rows(s, MERGE_CAST_SUB)
        acc = ga_ref[rows, :].astype(F32) * jnp.dot(oa_ref[rows, :], wp_out_ref[0:ka, :],
                                                    preferred_element_type=F32)
        acc = acc + gb_ref[rows, :].astype(F32) * jnp.dot(ob_ref[rows, :], wp_out_ref[ka:kb, :],
                                                          preferred_element_type=F32)
        acc = acc + gc_ref[rows, :].astype(F32) * jnp.dot(oc_ref[rows, :], wp_out_ref[kb:, :],
                                                          preferred_element_type=F32)
        o_ref[rows, :] = acc.astype(o_ref.dtype)


def merge_cast(z, oa, ob, oc, wa_all, wb_all, wc_all, wo_all, layer, tc=256):
    m = z.shape[0]
    d = wa_all.shape[-1]
    g0 = COL_GATE // tc
    gs = d // tc
    resident = lambda width: pl.BlockSpec((m, width), lambda c: (0, 0), pipeline_mode=pl.Buffered(1))
    weight = lambda k: pl.BlockSpec((None, k, tc), lambda c: (layer, 0, c))
    return pl.pallas_call(
        _merge_cast_kernel,
        grid=(d // tc,),
        in_specs=[resident(GMLP_WIDTH), resident(SWA_WIDTH), resident(MEM_WIDTH),
                  weight(GMLP_WIDTH), weight(SWA_WIDTH), weight(MEM_WIDTH), weight(d),
                  pl.BlockSpec((m, tc), lambda c: (0, g0 + c)),
                  pl.BlockSpec((m, tc), lambda c: (0, g0 + gs + c)),
                  pl.BlockSpec((m, tc), lambda c: (0, g0 + 2 * gs + c))],
        out_specs=[pl.BlockSpec((m, tc), lambda c: (0, c)),
                   pl.BlockSpec((GMLP_WIDTH + SWA_WIDTH + MEM_WIDTH, tc), lambda c: (0, c)),
                   pl.BlockSpec((d, tc), lambda c: (0, c))],
        out_shape=[jax.ShapeDtypeStruct((m, d), BF16),
                   jax.ShapeDtypeStruct((GMLP_WIDTH + SWA_WIDTH + MEM_WIDTH, d), BF16),
                   jax.ShapeDtypeStruct((d, d), BF16)],
        compiler_params=_params("parallel"),
        name="merge_cast",
    )(oa, ob, oc, wa_all, wb_all, wc_all, wo_all, z, z, z)


def _out_proj_kernel(mg_ref, w_ref, x_ref, gpost_ref, gnext_ref, xo_ref, ho_ref):
    def mix(s):
        return jnp.dot(mg_ref[_sub_rows(s, OUT_PROJ_SUB), :], w_ref[...], preferred_element_type=F32)

    def finish(s, mixed):
        rows = _sub_rows(s, OUT_PROJ_SUB)
        xn = x_ref[rows, :] + _rms(mixed, gpost_ref[...])
        xo_ref[rows, :] = xn
        ho_ref[rows, :] = _rms(xn, gnext_ref[...]).astype(ho_ref.dtype)

    _skewed(mg_ref.shape[0] // OUT_PROJ_SUB, mix, finish)


def out_proj(mg, w, x, gpost_all, gnext_all, layer, tm=512):
    m, d = x.shape
    return pl.pallas_call(
        _out_proj_kernel,
        grid=(m // tm,),
        in_specs=[pl.BlockSpec((tm, d), lambda i: (i, 0)),
                  pl.BlockSpec((d, d), lambda i: (0, 0), pipeline_mode=pl.Buffered(1)),
                  pl.BlockSpec((tm, d), lambda i: (i, 0)),
                  pl.BlockSpec((None, 1, d), lambda i: (layer, 0, 0)),
                  pl.BlockSpec((None, 1, d), lambda i: (layer, 0, 0))],
        out_specs=[pl.BlockSpec((tm, d), lambda i: (i, 0)),
                   pl.BlockSpec((tm, d), lambda i: (i, 0))],
        out_shape=[jax.ShapeDtypeStruct((m, d), F32), jax.ShapeDtypeStruct((m, d), BF16)],
        compiler_params=_params("parallel"),
        name="out_proj",
    )(mg, w, x, gpost_all, gnext_all)


def _ffn_kernel(h_ref, wu_ref, wd_ref, x_ref, gpost_ref, *rest, emit_next):
    if emit_next:
        gnext_ref, xo_ref, ho_ref = rest
    else:
        (xo_ref,) = rest
    j = pl.program_id(1)

    @pl.when(j == 0)
    def _():
        xo_ref[...] = jnp.zeros_like(xo_ref)

    def partial_sum(rows):
        a = jnp.dot(h_ref[rows, :], wu_ref[...], preferred_element_type=F32)
        a = jnp.square(jnp.maximum(a, 0.0)).astype(BF16)
        return jnp.dot(a, wd_ref[...], preferred_element_type=F32)

    last = pl.num_programs(1) - 1

    @pl.when(j < last)
    def _():
        xo_ref[...] += partial_sum(slice(None))

    @pl.when(j == last)
    def _():
        def finish(s, part):
            rows = _sub_rows(s, FFN_SUB)
            xn = x_ref[rows, :] + _rms(xo_ref[rows, :] + part, gpost_ref[...])
            xo_ref[rows, :] = xn
            if emit_next:
                ho_ref[rows, :] = _rms(xn, gnext_ref[...]).astype(ho_ref.dtype)

        _skewed(h_ref.shape[0] // FFN_SUB, lambda s: partial_sum(_sub_rows(s, FFN_SUB)), finish)


def ffn(h, wu, wd, x, gpost_all, gnext_all, layer, next_layer, tm=512, th=1024):
    m, d = x.shape
    hidden = wu.shape[-1]
    emit_next = next_layer is not None
    in_specs = [pl.BlockSpec((tm, d), lambda i, j: (i, 0)),
                pl.BlockSpec((d, th), lambda i, j: (0, j)),
                pl.BlockSpec((th, d), lambda i, j: (j, 0)),
                pl.BlockSpec((tm, d), lambda i, j: (i, 0)),
                pl.BlockSpec((None, 1, d), lambda i, j: (layer, 0, 0))]
    args = [h, wu, wd, x, gpost_all]
    out_specs = [pl.BlockSpec((tm, d), lambda i, j: (i, 0))]
    out_shape = [jax.ShapeDtypeStruct((m, d), F32)]
    if emit_next:
        in_specs.append(pl.BlockSpec((None, 1, d), lambda i, j: (next_layer, 0, 0)))
        args.append(gnext_all)
        out_specs.append(pl.BlockSpec((tm, d), lambda i, j: (i, 0)))
        out_shape.append(jax.ShapeDtypeStruct((m, d), BF16))
    res = pl.pallas_call(
        functools.partial(_ffn_kernel, emit_next=emit_next),
        grid=(m // tm, hidden // th),
        in_specs=in_specs,
        out_specs=out_specs,
        out_shape=out_shape,
        compiler_params=_params("parallel", "arbitrary"),
        name="ffn",
    )(*args)
    return (res[0], res[1]) if emit_next else (res[0], None)


def _ffn_up_cast_kernel(h_ref, wu_ref, a_ref, wub_ref):
    wub_ref[...] = wu_ref[...].astype(BF16)

    def up(s):
        return jnp.dot(h_ref[_sub_rows(s, FFN_CAST_SUB), :], wub_ref[...], preferred_element_type=F32)

    def finish(s, a):
        a_ref[_sub_rows(s, FFN_CAST_SUB), :] = jnp.square(jnp.maximum(a, 0.0)).astype(a_ref.dtype)

    _skewed(h_ref.shape[0] // FFN_CAST_SUB, up, finish)


def ffn_up_cast(h, wu_all, layer, th=1024):
    m, d = h.shape
    hidden = wu_all.shape[-1]
    return pl.pallas_call(
        _ffn_up_cast_kernel,
        grid=(hidden // th,),
        in_specs=[pl.BlockSpec((m, d), lambda j: (0, 0), pipeline_mode=pl.Buffered(1)),
                  pl.BlockSpec((None, d, th), lambda j: (layer, 0, j))],
        out_specs=[pl.BlockSpec((m, th), lambda j: (0, j)),
                   pl.BlockSpec((d, th), lambda j: (0, j))],
        out_shape=[jax.ShapeDtypeStruct((m, hidden), BF16), jax.ShapeDtypeStruct((d, hidden), BF16)],
        compiler_params=_params("parallel"),
        name="ffn_up_cast",
    )(h, wu_all)


def _ffn_down_cast_kernel(a_ref, wd_ref, acc_ref, wdb_ref):
    @pl.when(pl.program_id(0) == 0)
    def _():
        acc_ref[...] = jnp.zeros_like(acc_ref)

    wdb_ref[...] = wd_ref[...].astype(BF16)
    for s in range(a_ref.shape[0] // FFN_CAST_SUB):
        rows = _sub_rows(s, FFN_CAST_SUB)
        acc_ref[rows, :] += jnp.dot(a_ref[rows, :], wdb_ref[...], preferred_element_type=F32)


def ffn_down_cast(a, wd_all, layer, tk=512):
    m, hidden = a.shape
    d = wd_all.shape[-1]
    return pl.pallas_call(
        _ffn_down_cast_kernel,
        grid=(hidden // tk,),
        in_specs=[pl.BlockSpec((m, tk), lambda k: (0, k)),
                  pl.BlockSpec((None, tk, d), lambda k: (layer, k, 0))],
        out_specs=[pl.BlockSpec((m, d), lambda k: (0, 0)),
                   pl.BlockSpec((tk, d), lambda k: (k, 0))],
        out_shape=[jax.ShapeDtypeStruct((m, d), F32), jax.ShapeDtypeStruct((hidden, d), BF16)],
        compiler_params=_params("arbitrary"),
        name="ffn_down_cast",
    )(a, wd_all)


def _ffn_residual_kernel(acc_ref, x_ref, gpost_ref, *rest):
    xn = x_ref[...] + _rms(acc_ref[...], gpost_ref[...])
    if len(rest) == 3:
        gnext_ref, xo_ref, ho_ref = rest
        ho_ref[...] = _rms(xn, gnext_ref[...]).astype(ho_ref.dtype)
    else:
        (xo_ref,) = rest
    xo_ref[...] = xn


def ffn_residual(acc, x, gpost_all, gnext_all, layer, next_layer, tm=512):
    m, d = x.shape
    emit_next = next_layer is not None
    row_spec = pl.BlockSpec((tm, d), lambda i: (i, 0))
    in_specs = [row_spec, row_spec, pl.BlockSpec((None, 1, d), lambda i: (layer, 0, 0))]
    args = [acc, x, gpost_all]
    out_specs, out_shape = [row_spec], [jax.ShapeDtypeStruct((m, d), F32)]
    if emit_next:
        in_specs.append(pl.BlockSpec((None, 1, d), lambda i: (next_layer, 0, 0)))
        args.append(gnext_all)
        out_specs.append(row_spec)
        out_shape.append(jax.ShapeDtypeStruct((m, d), BF16))
    res = pl.pallas_call(
        _ffn_residual_kernel,
        grid=(m // tm,),
        in_specs=in_specs,
        out_specs=out_specs,
        out_shape=out_shape,
        compiler_params=_params("parallel"),
        name="ffn_residual",
    )(*args)
    return (res[0], res[1]) if emit_next else (res[0], None)


def kernel(x_prompt, x_sample, cache_swa_k, cache_swa_v, cache_mem_k, cache_mem_v, mem_prompt,
           w_in, ln_v_g, ln_v_b, w_s, b_s, sinks, mem_norm, w_mem_k, w_mem_v,
           w_pa, w_pb, w_pc, w_o, norm_mix_pre, norm_mix_post, norm_ffn_pre, norm_ffn_post,
           w_up, w_down):
    batch, seq, d = x_prompt.shape
    dec_batch, dec_seq, _ = x_sample.shape
    depth = w_in.shape[0]
    mem_tokens = mem_prompt.shape[1]

    row = lambda g: g.reshape(depth, 1, g.shape[-1])
    g_mix_pre, g_mix_post = row(norm_mix_pre), row(norm_mix_post)
    g_ffn_pre, g_ffn_post = row(norm_ffn_pre), row(norm_ffn_post)
    g_mem, ln_g, ln_b = row(mem_norm), row(ln_v_g), row(ln_v_b)
    bias_p = jnp.repeat(jnp.swapaxes(b_s[:, :, :GMLP_CHUNK], 1, 2), GMLP_GROUP_DIM, axis=2)
    bias_s = jnp.repeat(jnp.swapaxes(b_s[:, :, :dec_seq], 1, 2), GMLP_GROUP_DIM, axis=2)
    proj_rows = 1024
    tab_p = rope_table(jnp.arange(seq, dtype=jnp.int32))
    tab_s = jnp.tile(rope_table(PAST_LEN + jnp.arange(dec_seq, dtype=jnp.int32)), (1, proj_rows // dec_seq, 1))
    cache_k = cache_swa_k.reshape(depth, dec_batch, WINDOW, SWA_KV_WIDTH)
    cache_v = cache_swa_v.reshape(depth, dec_batch, WINDOW, SWA_KV_WIDTH)

    mk_p, mv_p = memory_kv(mem_prompt, g_mem, w_mem_k, w_mem_v)

    xp = x_prompt.reshape(batch * seq, d)
    xs = x_sample.reshape(dec_batch * dec_seq, d)
    hp = rmsnorm_cast(xp, g_mix_pre, 0)
    hs = rmsnorm_cast(xs, g_mix_pre, 0)

    k_p, v_p, k_s, v_s, gv_s = [], [], [], [], []
    for l in range(depth):
        nxt = l + 1 if l + 1 < depth else None

        z = proj_in(hs, w_in, tab_s, l, tm=proj_rows)
        oa, v_rows = gmlp(z, ln_g, ln_b, w_s, bias_s, l, dec_seq, True)
        ob = swa_sample(z, cache_k, cache_v, sinks, l, dec_batch, dec_seq)
        oc = mem_attn_cache(z, cache_mem_k, cache_mem_v, l, dec_batch, dec_seq)
        mg, w_p_b, w_o_b = merge_cast(z, oa, ob, oc, w_pa, w_pb, w_pc, w_o, l)
        xs, h2 = out_proj(mg, w_o_b, xs, g_mix_post, g_ffn_pre, l)
        act, w_up_b = ffn_up_cast(h2, w_up, l)
        acc, w_down_b = ffn_down_cast(act, w_down, l)
        xs, hs = ffn_residual(acc, xs, g_ffn_post, g_mix_pre, l, nxt)
        k_s.append(z[:, COL_K:COL_VV].astype(F32).reshape(dec_batch, dec_seq, SWA_KV_HEADS, SWA_HEAD_DIM))
        v_s.append(z[:, COL_VV:COL_CQ].astype(F32).reshape(dec_batch, dec_seq, SWA_KV_HEADS, SWA_HEAD_DIM))
        gv_s.append(v_rows.reshape(dec_batch, dec_seq, GMLP_WIDTH))

        z = proj_in(hp, w_in, tab_p, l, tm=proj_rows)
        oa, _ = gmlp(z, ln_g, ln_b, w_s, bias_p, l, GMLP_CHUNK, False)
        ob = swa_prompt(z, sinks, l, batch, seq)
        oc = mem_attn(z, mk_p, mv_p, l, batch, seq, 512)
        mg = merge(z, oa, ob, oc, w_p_b)
        xp, h2 = out_proj(mg, w_o_b, xp, g_mix_post, g_ffn_pre, l)
        xp, hp = ffn(h2, w_up_b, w_down_b, xp, g_ffn_post, g_mix_pre, l, nxt)
        kv_tail = z.reshape(batch, seq, -1)[:, -WINDOW:, COL_K:COL_CQ].astype(F32)
        k_p.append(kv_tail[..., :SWA_KV_WIDTH].reshape(batch, WINDOW, SWA_KV_HEADS, SWA_HEAD_DIM))
        v_p.append(kv_tail[..., SWA_KV_WIDTH:].reshape(batch, WINDOW, SWA_KV_HEADS, SWA_HEAD_DIM))

    mem_shape = (depth, batch, mem_tokens, MEM_HEADS, MEM_HEAD_DIM)
    return (xp.reshape(batch, seq, d), xs.reshape(dec_batch, dec_seq, d),
            jnp.stack(k_p), jnp.stack(v_p), mk_p.reshape(mem_shape), mv_p.reshape(mem_shape),
            jnp.stack(k_s), jnp.stack(v_s), jnp.stack(gv_s))
```

```python
import functools

import jax
import jax.numpy as jnp
from jax import lax
from jax.experimental import pallas as pl
from jax.experimental.pallas import tpu as pltpu

F32 = jnp.float32
BF16 = jnp.bfloat16

CHUNK = 64
GMLP_CHUNK = 128
GMLP_GROUPS = 12
GMLP_GROUP_DIM = 128
GMLP_WIDTH = GMLP_GROUPS * GMLP_GROUP_DIM
SWA_HEADS = 24
SWA_KV_HEADS = 4
SWA_HEAD_DIM = 64
SWA_WIDTH = SWA_HEADS * SWA_HEAD_DIM
SWA_KV_WIDTH = SWA_KV_HEADS * SWA_HEAD_DIM
WINDOW = 128
ROPE_THETA = 500000.0
ROPE_DIM = SWA_HEAD_DIM // 4
MEM_HEADS = 4
MEM_HEAD_DIM = 256
MEM_WIDTH = MEM_HEADS * MEM_HEAD_DIM
PAST_LEN = 1024
EPS = 1e-6
NEG = -1e30

COL_U = 0
COL_V = GMLP_WIDTH
COL_Q = 2 * GMLP_WIDTH
COL_K = COL_Q + SWA_WIDTH
COL_VV = COL_K + SWA_KV_WIDTH
COL_CQ = COL_VV + SWA_KV_WIDTH
COL_GATE = COL_CQ + MEM_WIDTH

LOG2E = 1.4426950408889634
Q_SCALE = SWA_HEAD_DIM ** -0.5 * LOG2E

LANES = 128
BF16_SUBLANES = 16
MERGE_COLS = 512
PROJ_SUB = 256
OUT_PROJ_SUB = 256
FFN_SUB = 256
FFN_CAST_SUB = 512
MERGE_CAST_SUB = 512
SWA_LOOKAHEAD = 3
HEADS_PER_LANE_TILE = LANES // SWA_HEAD_DIM
VMEM_LIMIT = 56 * 1024 * 1024


def _params(*sem):
    return pltpu.CompilerParams(dimension_semantics=sem, vmem_limit_bytes=VMEM_LIMIT)


def _rms(x, g):
    return x * lax.rsqrt(jnp.mean(x * x, axis=-1, keepdims=True) + EPS) * g


def _sub_rows(s, size):
    return slice(s * size, (s + 1) * size)


def _skewed(n, compute, finish):
    nxt = compute(0)
    for s in range(n):
        cur = nxt
        if s + 1 < n:
            nxt = compute(s + 1)
        finish(s, cur)


def _rmsnorm_kernel(x_ref, g_ref, o_ref):
    o_ref[...] = _rms(x_ref[...], g_ref[...]).astype(o_ref.dtype)


def rmsnorm_cast(x, g_all, layer, tm=512):
    m, d = x.shape
    return pl.pallas_call(
        _rmsnorm_kernel,
        grid=(m // tm,),
        in_specs=[pl.BlockSpec((tm, d), lambda i: (i, 0)),
                  pl.BlockSpec((None, 1, d), lambda i: (layer, 0, 0))],
        out_specs=pl.BlockSpec((tm, d), lambda i: (i, 0)),
        out_shape=jax.ShapeDtypeStruct((m, d), BF16),
        compiler_params=_params("parallel"),
        name="rmsnorm_cast",
    )(x, g_all)


def _rope(x, tab):
    return x * tab[0] + pltpu.roll(x, LANES - ROPE_DIM // 2, 1) * tab[1] + pltpu.roll(x, ROPE_DIM // 2, 1) * tab[2]


def rope_table(pos):
    half = ROPE_DIM // 2
    inv = ROPE_THETA ** (-jnp.arange(half, dtype=F32) / half)
    ang = pos.astype(F32)[:, None] * inv[None, :]
    cos, sin = jnp.cos(ang), jnp.sin(ang)
    s = pos.shape[0]
    rest = SWA_HEAD_DIM - ROPE_DIM
    c_head = jnp.concatenate([cos, cos, jnp.ones((s, rest), F32)], axis=1)
    up_head = jnp.concatenate([-sin, jnp.zeros((s, half + rest), F32)], axis=1)
    dn_head = jnp.concatenate([jnp.zeros((s, half), F32), sin, jnp.zeros((s, rest), F32)], axis=1)
    return jnp.stack([jnp.tile(t, (1, HEADS_PER_LANE_TILE)) for t in (c_head, up_head, dn_head)])


def _proj_in_kernel(h_ref, w_ref, tab_ref, o_ref, wb_ref, *, tn):
    j = pl.program_id(0)

    @pl.when(pl.program_id(1) == 0)
    def _():
        wb_ref[...] = w_ref[...].astype(BF16)

    def project(rows):
        return jnp.dot(h_ref[rows, :], wb_ref[...], preferred_element_type=F32)

    def rotary_tile(jj):
        def finish(acc, rows):
            for t in range(tn // LANES):
                col = jj * tn + t * LANES
                x = acc[:, t * LANES:(t + 1) * LANES]
                if col < COL_K:
                    x = _rope(x, tab_ref[:, rows, :]) * Q_SCALE
                elif col < COL_VV:
                    x = _rope(x, tab_ref[:, rows, :])
                o_ref[rows, t * LANES:(t + 1) * LANES] = x.astype(o_ref.dtype)
        return finish

    def plain(acc, rows):
        o_ref[rows, :] = acc.astype(o_ref.dtype)

    def gelu(acc, rows):
        o_ref[rows, :] = jax.nn.gelu(acc).astype(o_ref.dtype)

    def sigmoid(acc, rows):
        o_ref[rows, :] = jax.nn.sigmoid(acc).astype(o_ref.dtype)

    def run(finish):
        _skewed(h_ref.shape[0] // PROJ_SUB, lambda s: project(_sub_rows(s, PROJ_SUB)),
                lambda s, acc: finish(acc, _sub_rows(s, PROJ_SUB)))

    pl.when(j < COL_Q // tn)(lambda: run(gelu))
    for jj in range(COL_Q // tn, COL_GATE // tn):
        pl.when(j == jj)(functools.partial(run, rotary_tile(jj) if jj * tn < COL_VV else plain))
    pl.when(j >= COL_GATE // tn)(lambda: run(sigmoid))


def proj_in(h, w_all, tab, layer, tm=1024, tn=1024):
    m, d = h.shape
    n = w_all.shape[-1]
    tab_blocks = tab.shape[1] // tm

    def uses_tab(j):
        return jnp.logical_and(j >= COL_Q // tn, j * tn < COL_VV)

    return pl.pallas_call(
        functools.partial(_proj_in_kernel, tn=tn),
        grid=(n // tn, m // tm),
        in_specs=[pl.BlockSpec((tm, d), lambda j, i: (i, 0)),
                  pl.BlockSpec((None, d, tn), lambda j, i: (layer, 0, j)),
                  pl.BlockSpec((3, tm, LANES), lambda j, i: (0, jnp.where(uses_tab(j), i % tab_blocks, 0), 0))],
        out_specs=pl.BlockSpec((tm, tn), lambda j, i: (i, j)),
        out_shape=jax.ShapeDtypeStruct((m, n), BF16),
        scratch_shapes=[pltpu.VMEM((d, tn), BF16)],
        compiler_params=_params("arbitrary", "arbitrary"),
        name="proj_in",
    )(h, w_all, tab)


def _gmlp_kernel(u_ref, gv_ref, lng_ref, lnb_ref, ws_ref, bias_ref, o_ref, *v_refs, chunk):
    rows = u_ref.shape[0]
    n_chunks = rows // chunk
    gv = gv_ref[...].astype(F32)
    mu = jnp.mean(gv, axis=-1, keepdims=True)
    dev = gv - mu
    var = jnp.mean(dev * dev, axis=-1, keepdims=True)
    v = dev * lax.rsqrt(var + EPS) * lng_ref[...] + lnb_ref[...]
    if v_refs:
        v_refs[0][...] = v
    vb = v.astype(BF16)
    r_io = lax.broadcasted_iota(jnp.int32, (chunk, chunk), 0)
    c_io = lax.broadcasted_iota(jnp.int32, (chunk, chunk), 1)
    causal = c_io <= r_io
    for g in range(GMLP_GROUPS):
        cols = slice(g * GMLP_GROUP_DIM, (g + 1) * GMLP_GROUP_DIM)
        w = jnp.where(causal, ws_ref[g, :chunk, :chunk], 0.0).astype(BF16)
        vg = jnp.concatenate([vb[c * chunk:(c + 1) * chunk, cols] for c in range(n_chunks)], axis=1)
        s = jnp.dot(w, vg, preferred_element_type=F32)
        bias = bias_ref[:, cols]
        for c in range(n_chunks):
            rs = slice(c * chunk, (c + 1) * chunk)
            sc = s[:, c * GMLP_GROUP_DIM:(c + 1) * GMLP_GROUP_DIM] + bias
            o_ref[rs, cols] = (u_ref[rs, cols].astype(F32) * sc).astype(o_ref.dtype)


def gmlp(z, ln_g, ln_b, w_s, bias_tab, layer, chunk, emit_v, rows=512):
    m = z.shape[0]
    wdt = GMLP_WIDTH
    out_shape = [jax.ShapeDtypeStruct((m, wdt), BF16)]
    out_specs = [pl.BlockSpec((rows, wdt), lambda i: (i, 0))]
    if emit_v:
        out_shape.append(jax.ShapeDtypeStruct((m, wdt), F32))
        out_specs.append(pl.BlockSpec((rows, wdt), lambda i: (i, 0)))
    res = pl.pallas_call(
        functools.partial(_gmlp_kernel, chunk=chunk),
        grid=(m // rows,),
        in_specs=[pl.BlockSpec((rows, wdt), lambda i: (i, COL_U // wdt)),
                  pl.BlockSpec((rows, wdt), lambda i: (i, COL_V // wdt)),
                  pl.BlockSpec((None, 1, wdt), lambda i: (layer, 0, 0)),
                  pl.BlockSpec((None, 1, wdt), lambda i: (layer, 0, 0)),
                  pl.BlockSpec((None, GMLP_GROUPS, GMLP_CHUNK, GMLP_CHUNK), lambda i: (layer, 0, 0, 0)),
                  pl.BlockSpec((None, chunk, wdt), lambda i: (layer, 0, 0))],
        out_specs=out_specs,
        out_shape=out_shape,
        compiler_params=_params("parallel"),
        name="gmlp_c%d" % chunk,
    )(z, z, ln_g, ln_b, w_s, bias_tab)
    return res if emit_v else (res[0], None)


def _swa_core(q_ref, k_all, v_all, valid, sink_ref, o_ref):
    rows = q_ref.shape[0]
    nk = k_all.shape[0]
    tiles_per_group = SWA_HEADS // SWA_KV_HEADS // HEADS_PER_LANE_TILE
    low_half = lax.broadcasted_iota(jnp.int32, (nk, LANES), 1) < SWA_HEAD_DIM
    v_t = v_all.T
    ones_rows = jnp.ones((BF16_SUBLANES, nk), BF16)

    def operands(kvh):
        c0 = (kvh // HEADS_PER_LANE_TILE) * LANES
        kk = k_all[:, c0:c0 + LANES]
        kk_sw = pltpu.roll(kk, SWA_HEAD_DIM, 1)
        if kvh % HEADS_PER_LANE_TILE == 0:
            k_ext = [jnp.where(low_half, kk, 0.0), jnp.where(low_half, 0.0, kk_sw)]
        else:
            k_ext = [jnp.where(low_half, kk_sw, 0.0), jnp.where(low_half, 0.0, kk)]
        vt_h = jnp.concatenate([v_t[kvh * SWA_HEAD_DIM:(kvh + 1) * SWA_HEAD_DIM, :].astype(BF16), ones_rows],
                               axis=0)
        q_tiles = []
        for t in range(kvh * tiles_per_group, (kvh + 1) * tiles_per_group):
            qt = q_ref[:, t * LANES:(t + 1) * LANES]
            if rows < LANES:
                qt = jnp.concatenate([qt, jnp.zeros((LANES - rows, LANES), BF16)], axis=0)
            q_tiles.append(qt)
        q_stack = jnp.concatenate(q_tiles, axis=0)
        return [k.astype(BF16) for k in k_ext], vt_h, q_stack

    def scores(ops, e):
        return lax.dot_general(ops[0][e], ops[2], (((1,), (1,)), ((), ())),
                               preferred_element_type=F32)

    def attend(ops, kvh, e, s):
        probs, sink_p = [], []
        for t in range(tiles_per_group):
            sink = sink_ref[HEADS_PER_LANE_TILE * (kvh * tiles_per_group + t) + e] * LOG2E
            st = jnp.where(valid, s[:, t * LANES:(t + 1) * LANES], NEG)
            mx = jnp.maximum(jnp.max(st, axis=0, keepdims=True), sink)
            probs.append(jnp.exp2(st - mx).astype(BF16))
            sink_p.append(jnp.exp2(sink - mx))
        pv = jnp.dot(ops[1], jnp.concatenate(probs, axis=1), preferred_element_type=F32)
        den = pv[SWA_HEAD_DIM:SWA_HEAD_DIM + 1] + jnp.concatenate(sink_p, axis=1)
        return pv[:SWA_HEAD_DIM] * (1.0 / den)

    items = [(kvh, e) for kvh in range(SWA_KV_HEADS) for e in range(HEADS_PER_LANE_TILE)]
    ops_of, pending = {}, []

    def issue(idx):
        kvh, e = items[idx]
        if kvh not in ops_of:
            ops_of[kvh] = operands(kvh)
        pending.append(scores(ops_of[kvh], e))

    for idx in range(min(SWA_LOOKAHEAD, len(items))):
        issue(idx)
    halves = []
    for idx, (kvh, e) in enumerate(items):
        if idx + SWA_LOOKAHEAD < len(items):
            issue(idx + SWA_LOOKAHEAD)
        halves.append(attend(ops_of[kvh], kvh, e, pending.pop(0)))
        if e == HEADS_PER_LANE_TILE - 1:
            o_t = jnp.concatenate(halves, axis=0)
            halves = []
            for t in range(tiles_per_group):
                tile = kvh * tiles_per_group + t
                o_tile = o_t[:, t * LANES:(t + 1) * LANES].T
                o_ref[:, tile * LANES:(tile + 1) * LANES] = o_tile[:rows].astype(o_ref.dtype)


def _swa_prompt_kernel(sink_ref, q_ref, kc_ref, kp_ref, vc_ref, vp_ref, o_ref):
    tq = q_ref.shape[0]
    k_all = jnp.concatenate([kp_ref[...], kc_ref[...]], axis=0).astype(F32)
    v_all = jnp.concatenate([vp_ref[...], vc_ref[...]], axis=0).astype(F32)
    nk = k_all.shape[0]
    k_row = lax.broadcasted_iota(jnp.int32, (nk, tq), 0)
    k_chunk = k_row // CHUNK
    q_chunk = lax.broadcasted_iota(jnp.int32, (nk, tq), 1) // CHUNK
    first_valid = jnp.where(pl.program_id(1) == 0, WINDOW, 0)
    valid = (k_chunk >= q_chunk) & (k_chunk <= q_chunk + WINDOW // CHUNK) & (k_row >= first_valid)
    _swa_core(q_ref, k_all, v_all, valid, sink_ref, o_ref)


def swa_prompt(z, sinks, layer, batch, seq):
    tq = WINDOW
    nqb = seq // tq
    m = z.shape[0]

    def prev(i):
        return jnp.maximum(i - 1, 0)

    return pl.pallas_call(
        _swa_prompt_kernel,
        grid=(batch, nqb),
        in_specs=[pl.BlockSpec(memory_space=pltpu.SMEM),
                  pl.BlockSpec((tq, SWA_WIDTH), lambda b, i: (b * nqb + i, COL_Q // SWA_WIDTH)),
                  pl.BlockSpec((tq, SWA_KV_WIDTH), lambda b, i: (b * nqb + i, COL_K // SWA_KV_WIDTH)),
                  pl.BlockSpec((tq, SWA_KV_WIDTH), lambda b, i: (b * nqb + prev(i), COL_K // SWA_KV_WIDTH)),
                  pl.BlockSpec((tq, SWA_KV_WIDTH), lambda b, i: (b * nqb + i, COL_VV // SWA_KV_WIDTH)),
                  pl.BlockSpec((tq, SWA_KV_WIDTH), lambda b, i: (b * nqb + prev(i), COL_VV // SWA_KV_WIDTH))],
        out_specs=pl.BlockSpec((tq, SWA_WIDTH), lambda b, i: (b * nqb + i, 0)),
        out_shape=jax.ShapeDtypeStruct((m, SWA_WIDTH), BF16),
        compiler_params=_params("parallel", "arbitrary"),
        name="swa_prompt",
    )(sinks[layer], z, z, z, z, z)


def _swa_sample_kernel(sink_ref, q_ref, kn_ref, vn_ref, ck_ref, cv_ref, o_ref):
    tq = q_ref.shape[0]
    pad = jnp.zeros((WINDOW - tq, SWA_KV_WIDTH), F32)
    k_all = jnp.concatenate([ck_ref[...], kn_ref[...].astype(F32), pad], axis=0)
    v_all = jnp.concatenate([cv_ref[...], vn_ref[...].astype(F32), pad], axis=0)
    valid = lax.broadcasted_iota(jnp.int32, (k_all.shape[0], LANES), 0) < WINDOW + tq
    _swa_core(q_ref, k_all, v_all, valid, sink_ref, o_ref)


def swa_sample(z, cache_k, cache_v, sinks, layer, batch, seq):
    m = z.shape[0]
    return pl.pallas_call(
        _swa_sample_kernel,
        grid=(batch,),
        in_specs=[pl.BlockSpec(memory_space=pltpu.SMEM),
                  pl.BlockSpec((seq, SWA_WIDTH), lambda b: (b, COL_Q // SWA_WIDTH)),
                  pl.BlockSpec((seq, SWA_KV_WIDTH), lambda b: (b, COL_K // SWA_KV_WIDTH)),
                  pl.BlockSpec((seq, SWA_KV_WIDTH), lambda b: (b, COL_VV // SWA_KV_WIDTH)),
                  pl.BlockSpec((None, None, WINDOW, SWA_KV_WIDTH), lambda b: (layer, b, 0, 0)),
                  pl.BlockSpec((None, None, WINDOW, SWA_KV_WIDTH), lambda b: (layer, b, 0, 0))],
        out_specs=pl.BlockSpec((seq, SWA_WIDTH), lambda b: (b, 0)),
        out_shape=jax.ShapeDtypeStruct((m, SWA_WIDTH), BF16),
        compiler_params=_params("parallel"),
        name="swa_sample",
    )(sinks[layer], z, z, z, cache_k, cache_v)


def _memory_kv_kernel(mem_ref, g_ref, wk_ref, wv_ref, k_ref, v_ref, wkb_ref, wvb_ref):
    @pl.when(pl.program_id(1) == 0)
    def _():
        wkb_ref[...] = wk_ref[...].astype(BF16)
        wvb_ref[...] = wv_ref[...].astype(BF16)

    h = _rms(mem_ref[...], g_ref[...]).astype(BF16)
    k_ref[...] = jnp.dot(h, wkb_ref[...], preferred_element_type=F32)
    v_ref[...] = jnp.dot(h, wvb_ref[...], preferred_element_type=F32)


def memory_kv(mem, g_all, wk_all, wv_all):
    batch, tokens, d = mem.shape
    depth = wk_all.shape[0]
    shp = jax.ShapeDtypeStruct((depth, batch, tokens, MEM_WIDTH), F32)
    return pl.pallas_call(
        _memory_kv_kernel,
        grid=(depth, batch),
        in_specs=[pl.BlockSpec((None, tokens, d), lambda l, b: (b, 0, 0)),
                  pl.BlockSpec((None, 1, d), lambda l, b: (l, 0, 0)),
                  pl.BlockSpec((None, d, MEM_WIDTH), lambda l, b: (l, 0, 0)),
                  pl.BlockSpec((None, d, MEM_WIDTH), lambda l, b: (l, 0, 0))],
        out_specs=[pl.BlockSpec((None, None, tokens, MEM_WIDTH), lambda l, b: (l, b, 0, 0)),
                   pl.BlockSpec((None, None, tokens, MEM_WIDTH), lambda l, b: (l, b, 0, 0))],
        out_shape=[shp, shp],
        scratch_shapes=[pltpu.VMEM((d, MEM_WIDTH), BF16), pltpu.VMEM((d, MEM_WIDTH), BF16)],
        compiler_params=_params("arbitrary", "arbitrary"),
        name="memory_kv",
    )(mem, g_all, wk_all, wv_all)


def _head_cols(h):
    return slice(h * MEM_HEAD_DIM, (h + 1) * MEM_HEAD_DIM)


def _mem_attention(cq_ref, head_k, head_v, o_ref):
    scale = MEM_HEAD_DIM ** -0.5

    def scores(h):
        return lax.dot_general(cq_ref[:, _head_cols(h)], head_k(h).astype(BF16), (((1,), (1,)), ((), ())),
                               preferred_element_type=F32) * scale

    def attend(h, s):
        mx = jnp.max(s, axis=-1, keepdims=True)
        p = jnp.exp(s - mx)
        den = jnp.sum(p, axis=-1, keepdims=True)
        p = (p * (1.0 / den)).astype(BF16)
        o_ref[:, _head_cols(h)] = jnp.dot(p, head_v(h).astype(BF16),
                                          preferred_element_type=F32).astype(o_ref.dtype)

    _skewed(MEM_HEADS, scores, attend)


def _mem_attn_kernel(cq_ref, mk_ref, mv_ref, o_ref):
    _mem_attention(cq_ref, lambda h: mk_ref[:, _head_cols(h)], lambda h: mv_ref[:, _head_cols(h)], o_ref)


def mem_attn(z, mk_all, mv_all, layer, batch, seq, tq):
    m = z.shape[0]
    nqb = seq // tq
    tokens = mk_all.shape[2]
    kv_spec = pl.BlockSpec((None, None, tokens, MEM_WIDTH), lambda b, i: (layer, b, 0, 0))
    return pl.pallas_call(
        _mem_attn_kernel,
        grid=(batch, nqb),
        in_specs=[pl.BlockSpec((tq, MEM_WIDTH), lambda b, i: (b * nqb + i, COL_CQ // MEM_WIDTH)),
                  kv_spec, kv_spec],
        out_specs=pl.BlockSpec((tq, MEM_WIDTH), lambda b, i: (b * nqb + i, 0)),
        out_shape=jax.ShapeDtypeStruct((m, MEM_WIDTH), BF16),
        compiler_params=_params("parallel", "arbitrary"),
        name="mem_attn",
    )(z, mk_all, mv_all)


def _cache_head_copies(k_hbm, v_hbm, k_buf, v_buf, sems, layer, b, slot):
    copies = []
    for h in range(MEM_HEADS):
        copies.append(pltpu.make_async_copy(k_hbm.at[layer, b, :, h, :], k_buf.at[slot, h], sems.at[slot, h]))
        copies.append(pltpu.make_async_copy(v_hbm.at[layer, b, :, h, :], v_buf.at[slot, h],
                                            sems.at[slot, MEM_HEADS + h]))
    return copies


def _mem_attn_cache_kernel(cq_ref, k_hbm, v_hbm, o_ref, k_buf, v_buf, sems, *, layer):
    b = pl.program_id(0)
    slot = b % 2

    def copies(batch_index, into):
        return _cache_head_copies(k_hbm, v_hbm, k_buf, v_buf, sems, layer, batch_index, into)

    @pl.when(b == 0)
    def _():
        for c in copies(0, 0):
            c.start()

    @pl.when(b + 1 < pl.num_programs(0))
    def _():
        for c in copies(b + 1, 1 - slot):
            c.start()

    for c in copies(b, slot):
        c.wait()
    _mem_attention(cq_ref, lambda h: k_buf[slot, h], lambda h: v_buf[slot, h], o_ref)


def mem_attn_cache(z, k_cache, v_cache, layer, batch, seq):
    m = z.shape[0]
    tokens = k_cache.shape[2]
    buf = pltpu.VMEM((2, MEM_HEADS, tokens, MEM_HEAD_DIM), F32)
    return pl.pallas_call(
        functools.partial(_mem_attn_cache_kernel, layer=layer),
        grid=(batch,),
        in_specs=[pl.BlockSpec((seq, MEM_WIDTH), lambda b: (b, COL_CQ // MEM_WIDTH)),
                  pl.BlockSpec(memory_space=pl.ANY),
                  pl.BlockSpec(memory_space=pl.ANY)],
        out_specs=pl.BlockSpec((seq, MEM_WIDTH), lambda b: (b, 0)),
        out_shape=jax.ShapeDtypeStruct((m, MEM_WIDTH), BF16),
        scratch_shapes=[buf, buf, pltpu.SemaphoreType.DMA((2, 2 * MEM_HEADS))],
        compiler_params=_params("arbitrary"),
        name="mem_attn_cache",
    )(z, k_cache, v_cache)


def _merge_kernel(oa_ref, ob_ref, oc_ref, w_ref, g_ref, o_ref):
    d = o_ref.shape[1]
    ka, kb = GMLP_WIDTH, GMLP_WIDTH + SWA_WIDTH
    for c in range(d // MERGE_COLS):
        cols = slice(c * MERGE_COLS, (c + 1) * MERGE_COLS)

        def gate(k):
            return g_ref[:, k * d + c * MERGE_COLS:k * d + (c + 1) * MERGE_COLS].astype(F32)

        acc = gate(0) * jnp.dot(oa_ref[...], w_ref[0:ka, cols], preferred_element_type=F32)
        acc = acc + gate(1) * jnp.dot(ob_ref[...], w_ref[ka:kb, cols], preferred_element_type=F32)
        acc = acc + gate(2) * jnp.dot(oc_ref[...], w_ref[kb:, cols], preferred_element_type=F32)
        o_ref[:, cols] = acc.astype(o_ref.dtype)


def merge(z, oa, ob, oc, w, tm=512):
    m = z.shape[0]
    k, d = w.shape
    return pl.pallas_call(
        _merge_kernel,
        grid=(m // tm,),
        in_specs=[pl.BlockSpec((tm, GMLP_WIDTH), lambda i: (i, 0)),
                  pl.BlockSpec((tm, SWA_WIDTH), lambda i: (i, 0)),
                  pl.BlockSpec((tm, MEM_WIDTH), lambda i: (i, 0)),
                  pl.BlockSpec((k, d), lambda i: (0, 0), pipeline_mode=pl.Buffered(1)),
                  pl.BlockSpec((tm, 3 * d), lambda i: (i, COL_GATE // (3 * d)))],
        out_specs=pl.BlockSpec((tm, d), lambda i: (i, 0)),
        out_shape=jax.ShapeDtypeStruct((m, d), BF16),
        compiler_params=_params("parallel"),
        name="merge",
    )(oa, ob, oc, w, z)


def _merge_cast_kernel(oa_ref, ob_ref, oc_ref, wa_ref, wb_ref, wc_ref, wo_ref, ga_ref, gb_ref, gc_ref,
                       o_ref, wp_out_ref, wo_out_ref):
    ka, kb = GMLP_WIDTH, GMLP_WIDTH + SWA_WIDTH
    wp_out_ref[0:ka, :] = wa_ref[...].astype(BF16)
    wp_out_ref[ka:kb, :] = wb_ref[...].astype(BF16)
    wp_out_ref[kb:, :] = wc_ref[...].astype(BF16)
    wo_out_ref[...] = wo_ref[...].astype(BF16)
    for s in range(oa_ref.shape[0] // MERGE_CAST_SUB):
        rows = _sub_rows(s, MERGE_CAST_SUB)
        acc = ga_ref[rows, :].astype(F32) * jnp.dot(oa_ref[rows, :], wp_out_ref[0:ka, :],
                                                    preferred_element_type=F32)
        acc = acc + gb_ref[rows, :].astype(F32) * jnp.dot(ob_ref[rows, :], wp_out_ref[ka:kb, :],
                                                          preferred_element_type=F32)
        acc = acc + gc_ref[rows, :].astype(F32) * jnp.dot(oc_ref[rows, :], wp_out_ref[kb:, :],
                                                          preferred_element_type=F32)
        o_ref[rows, :] = acc.astype(o_ref.dtype)


def merge_cast(z, oa, ob, oc, wa_all, wb_all, wc_all, wo_all, layer, tc=256):
    m = z.shape[0]
    d = wa_all.shape[-1]
    g0 = COL_GATE // tc
    gs = d // tc
    resident = lambda width: pl.BlockSpec((m, width), lambda c: (0, 0), pipeline_mode=pl.Buffered(1))
    weight = lambda k: pl.BlockSpec((None, k, tc), lambda c: (layer, 0, c))
    return pl.pallas_call(
        _merge_cast_kernel,
        grid=(d // tc,),
        in_specs=[resident(GMLP_WIDTH), resident(SWA_WIDTH), resident(MEM_WIDTH),
                  weight(GMLP_WIDTH), weight(SWA_WIDTH), weight(MEM_WIDTH), weight(d),
                  pl.BlockSpec((m, tc), lambda c: (0, g0 + c)),
                  pl.BlockSpec((m, tc), lambda c: (0, g0 + gs + c)),
                  pl.BlockSpec((m, tc), lambda c: (0, g0 + 2 * gs + c))],
        out_specs=[pl.BlockSpec((m, tc), lambda c: (0, c)),
                   pl.BlockSpec((GMLP_WIDTH + SWA_WIDTH + MEM_WIDTH, tc), lambda c: (0, c)),
                   pl.BlockSpec((d, tc), lambda c: (0, c))],
        out_shape=[jax.ShapeDtypeStruct((m, d), BF16),
                   jax.ShapeDtypeStruct((GMLP_WIDTH + SWA_WIDTH + MEM_WIDTH, d), BF16),
                   jax.ShapeDtypeStruct((d, d), BF16)],
        compiler_params=_params("parallel"),
        name="merge_cast",
    )(oa, ob, oc, wa_all, wb_all, wc_all, wo_all, z, z, z)


def _out_proj_kernel(mg_ref, w_ref, x_ref, gpost_ref, gnext_ref, xo_ref, ho_ref):
    def mix(s):
        return jnp.dot(mg_ref[_sub_rows(s, OUT_PROJ_SUB), :], w_ref[...], preferred_element_type=F32)

    def finish(s, mixed):
        rows = _sub_rows(s, OUT_PROJ_SUB)
        xn = x_ref[rows, :] + _rms(mixed, gpost_ref[...])
        xo_ref[rows, :] = xn
        ho_ref[rows, :] = _rms(xn, gnext_ref[...]).astype(ho_ref.dtype)

    _skewed(mg_ref.shape[0] // OUT_PROJ_SUB, mix, finish)


def out_proj(mg, w, x, gpost_all, gnext_all, layer, tm=512):
    m, d = x.shape
    return pl.pallas_call(
        _out_proj_kernel,
        grid=(m // tm,),
        in_specs=[pl.BlockSpec((tm, d), lambda i: (i, 0)),
                  pl.BlockSpec((d, d), lambda i: (0, 0), pipeline_mode=pl.Buffered(1)),
                  pl.BlockSpec((tm, d), lambda i: (i, 0)),
                  pl.BlockSpec((None, 1, d), lambda i: (layer, 0, 0)),
                  pl.BlockSpec((None, 1, d), lambda i: (layer, 0, 0))],
        out_specs=[pl.BlockSpec((tm, d), lambda i: (i, 0)),
                   pl.BlockSpec((tm, d), lambda i: (i, 0))],
        out_shape=[jax.ShapeDtypeStruct((m, d), F32), jax.ShapeDtypeStruct((m, d), BF16)],
        compiler_params=_params("parallel"),
        name="out_proj",
    )(mg, w, x, gpost_all, gnext_all)


def _ffn_kernel(h_ref, wu_ref, wd_ref, x_ref, gpost_ref, *rest, emit_next):
    if emit_next:
        gnext_ref, xo_ref, ho_ref = rest
    else:
        (xo_ref,) = rest
    j = pl.program_id(1)

    @pl.when(j == 0)
    def _():
        xo_ref[...] = jnp.zeros_like(xo_ref)

    def partial_sum(rows):
        a = jnp.dot(h_ref[rows, :], wu_ref[...], preferred_element_type=F32)
        a = jnp.square(jnp.maximum(a, 0.0)).astype(BF16)
        return jnp.dot(a, wd_ref[...], preferred_element_type=F32)

    last = pl.num_programs(1) - 1

    @pl.when(j < last)
    def _():
        xo_ref[...] += partial_sum(slice(None))

    @pl.when(j == last)
    def _():
        def finish(s, part):
            rows = _sub_rows(s, FFN_SUB)
            xn = x_ref[rows, :] + _rms(xo_ref[rows, :] + part, gpost_ref[...])
            xo_ref[rows, :] = xn
            if emit_next:
                ho_ref[rows, :] = _rms(xn, gnext_ref[...]).astype(ho_ref.dtype)

        _skewed(h_ref.shape[0] // FFN_SUB, lambda s: partial_sum(_sub_rows(s, FFN_SUB)), finish)


def ffn(h, wu, wd, x, gpost_all, gnext_all, layer, next_layer, tm=512, th=1024):
    m, d = x.shape
    hidden = wu.shape[-1]
    emit_next = next_layer is not None
    in_specs = [pl.BlockSpec((tm, d), lambda i, j: (i, 0)),
                pl.BlockSpec((d, th), lambda i, j: (0, j)),
                pl.BlockSpec((th, d), lambda i, j: (j, 0)),
                pl.BlockSpec((tm, d), lambda i, j: (i, 0)),
                pl.BlockSpec((None, 1, d), lambda i, j: (layer, 0, 0))]
    args = [h, wu, wd, x, gpost_all]
    out_specs = [pl.BlockSpec((tm, d), lambda i, j: (i, 0))]
    out_shape = [jax.ShapeDtypeStruct((m, d), F32)]
    if emit_next:
        in_specs.append(pl.BlockSpec((None, 1, d), lambda i, j: (next_layer, 0, 0)))
        args.append(gnext_all)
        out_specs.append(pl.BlockSpec((tm, d), lambda i, j: (i, 0)))
        out_shape.append(jax.ShapeDtypeStruct((m, d), BF16))
    res = pl.pallas_call(
        functools.partial(_ffn_kernel, emit_next=emit_next),
        grid=(m // tm, hidden // th),
        in_specs=in_specs,
        out_specs=out_specs,
        out_shape=out_shape,
        compiler_params=_params("parallel", "arbitrary"),
        name="ffn",
    )(*args)
    return (res[0], res[1]) if emit_next else (res[0], None)


def _ffn_up_cast_kernel(h_ref, wu_ref, a_ref, wub_ref):
    wub_ref[...] = wu_ref[...].astype(BF16)

    def up(s):
        return jnp.dot(h_ref[_sub_rows(s, FFN_CAST_SUB), :], wub_ref[...], preferred_element_type=F32)

    def finish(s, a):
        a_ref[_sub_rows(s, FFN_CAST_SUB), :] = jnp.square(jnp.maximum(a, 0.0)).astype(a_ref.dtype)

    _skewed(h_ref.shape[0] // FFN_CAST_SUB, up, finish)


def ffn_up_cast(h, wu_all, layer, th=1024):
    m, d = h.shape
    hidden = wu_all.shape[-1]
    return pl.pallas_call(
        _ffn_up_cast_kernel,
        grid=(hidden // th,),
        in_specs=[pl.BlockSpec((m, d), lambda j: (0, 0), pipeline_mode=pl.Buffered(1)),
                  pl.BlockSpec((None, d, th), lambda j: (layer, 0, j))],
        out_specs=[pl.BlockSpec((m, th), lambda j: (0, j)),
                   pl.BlockSpec((d, th), lambda j: (0, j))],
        out_shape=[jax.ShapeDtypeStruct((m, hidden), BF16), jax.ShapeDtypeStruct((d, hidden), BF16)],
        compiler_params=_params("parallel"),
        name="ffn_up_cast",
    )(h, wu_all)


def _ffn_down_cast_kernel(a_ref, wd_ref, acc_ref, wdb_ref):
    @pl.when(pl.program_id(0) == 0)
    def _():
        acc_ref[...] = jnp.zeros_like(acc_ref)

    wdb_ref[...] = wd_ref[...].astype(BF16)
    for s in range(a_ref.shape[0] // FFN_CAST_SUB):
        rows = _sub_rows(s, FFN_CAST_SUB)
        acc_ref[rows, :] += jnp.dot(a_ref[rows, :], wdb_ref[...], preferred_element_type=F32)


def ffn_down_cast(a, wd_all, layer, tk=1024):
    m, hidden = a.shape
    d = wd_all.shape[-1]
    return pl.pallas_call(
        _ffn_down_cast_kernel,
        grid=(hidden // tk,),
        in_specs=[pl.BlockSpec((m, tk), lambda k: (0, k)),
                  pl.BlockSpec((None, tk, d), lambda k: (layer, k, 0))],
        out_specs=[pl.BlockSpec((m, d), lambda k: (0, 0), pipeline_mode=pl.Buffered(1)),
                   pl.BlockSpec((tk, d), lambda k: (k, 0))],
        out_shape=[jax.ShapeDtypeStruct((m, d), F32), jax.ShapeDtypeStruct((hidden, d), BF16)],
        compiler_params=_params("arbitrary"),
        name="ffn_down_cast",
    )(a, wd_all)


def _ffn_residual_kernel(acc_ref, x_ref, gpost_ref, *rest):
    xn = x_ref[...] + _rms(acc_ref[...], gpost_ref[...])
    if len(rest) == 3:
        gnext_ref, xo_ref, ho_ref = rest
        ho_ref[...] = _rms(xn, gnext_ref[...]).astype(ho_ref.dtype)
    else:
        (xo_ref,) = rest
    xo_ref[...] = xn


def ffn_residual(acc, x, gpost_all, gnext_all, layer, next_layer, tm=512):
    m, d = x.shape
    emit_next = next_layer is not None
    row_spec = pl.BlockSpec((tm, d), lambda i: (i, 0))
    in_specs = [row_spec, row_spec, pl.BlockSpec((None, 1, d), lambda i: (layer, 0, 0))]
    args = [acc, x, gpost_all]
    out_specs, out_shape = [row_spec], [jax.ShapeDtypeStruct((m, d), F32)]
    if emit_next:
        in_specs.append(pl.BlockSpec((None, 1, d), lambda i: (next_layer, 0, 0)))
        args.append(gnext_all)
        out_specs.append(row_spec)
        out_shape.append(jax.ShapeDtypeStruct((m, d), BF16))
    res = pl.pallas_call(
        _ffn_residual_kernel,
        grid=(m // tm,),
        in_specs=in_specs,
        out_specs=out_specs,
        out_shape=out_shape,
        compiler_params=_params("parallel"),
        name="ffn_residual",
    )(*args)
    return (res[0], res[1]) if emit_next else (res[0], None)


def kernel(x_prompt, x_sample, cache_swa_k, cache_swa_v, cache_mem_k, cache_mem_v, mem_prompt,
           w_in, ln_v_g, ln_v_b, w_s, b_s, sinks, mem_norm, w_mem_k, w_mem_v,
           w_pa, w_pb, w_pc, w_o, norm_mix_pre, norm_mix_post, norm_ffn_pre, norm_ffn_post,
           w_up, w_down):
    batch, seq, d = x_prompt.shape
    dec_batch, dec_seq, _ = x_sample.shape
    depth = w_in.shape[0]
    mem_tokens = mem_prompt.shape[1]

    row = lambda g: g.reshape(depth, 1, g.shape[-1])
    g_mix_pre, g_mix_post = row(norm_mix_pre), row(norm_mix_post)
    g_ffn_pre, g_ffn_post = row(norm_ffn_pre), row(norm_ffn_post)
    g_mem, ln_g, ln_b = row(mem_norm), row(ln_v_g), row(ln_v_b)
    bias_p = jnp.repeat(jnp.swapaxes(b_s[:, :, :GMLP_CHUNK], 1, 2), GMLP_GROUP_DIM, axis=2)
    bias_s = jnp.repeat(jnp.swapaxes(b_s[:, :, :dec_seq], 1, 2), GMLP_GROUP_DIM, axis=2)
    proj_rows = 1024
    tab_p = rope_table(jnp.arange(seq, dtype=jnp.int32))
    tab_s = jnp.tile(rope_table(PAST_LEN + jnp.arange(dec_seq, dtype=jnp.int32)), (1, proj_rows // dec_seq, 1))
    cache_k = cache_swa_k.reshape(depth, dec_batch, WINDOW, SWA_KV_WIDTH)
    cache_v = cache_swa_v.reshape(depth, dec_batch, WINDOW, SWA_KV_WIDTH)

    mk_p, mv_p = memory_kv(mem_prompt, g_mem, w_mem_k, w_mem_v)

    xp = x_prompt.reshape(batch * seq, d)
    xs = x_sample.reshape(dec_batch * dec_seq, d)
    hp = rmsnorm_cast(xp, g_mix_pre, 0)
    hs = rmsnorm_cast(xs, g_mix_pre, 0)

    k_p, v_p, k_s, v_s, gv_s = [], [], [], [], []
    for l in range(depth):
        nxt = l + 1 if l + 1 < depth else None

        z = proj_in(hs, w_in, tab_s, l, tm=proj_rows)
        oa, v_rows = gmlp(z, ln_g, ln_b, w_s, bias_s, l, dec_seq, True)
        ob = swa_sample(z, cache_k, cache_v, sinks, l, dec_batch, dec_seq)
        oc = mem_attn_cache(z, cache_mem_k, cache_mem_v, l, dec_batch, dec_seq)
        mg, w_p_b, w_o_b = merge_cast(z, oa, ob, oc, w_pa, w_pb, w_pc, w_o, l)
        xs, h2 = out_proj(mg, w_o_b, xs, g_mix_post, g_ffn_pre, l)
        act, w_up_b = ffn_up_cast(h2, w_up, l)
        acc, w_down_b = ffn_down_cast(act, w_down, l)
        xs, hs = ffn_residual(acc, xs, g_ffn_post, g_mix_pre, l, nxt)
        k_s.append(z[:, COL_K:COL_VV].astype(F32).reshape(dec_batch, dec_seq, SWA_KV_HEADS, SWA_HEAD_DIM))
        v_s.append(z[:, COL_VV:COL_CQ].astype(F32).reshape(dec_batch, dec_seq, SWA_KV_HEADS, SWA_HEAD_DIM))
        gv_s.append(v_rows.reshape(dec_batch, dec_seq, GMLP_WIDTH))

        z = proj_in(hp, w_in, tab_p, l, tm=proj_rows)
        oa, _ = gmlp(z, ln_g, ln_b, w_s, bias_p, l, GMLP_CHUNK, False)
        ob = swa_prompt(z, sinks, l, batch, seq)
        oc = mem_attn(z, mk_p, mv_p, l, batch, seq, 1024)
        mg = merge(z, oa, ob, oc, w_p_b)
        xp, h2 = out_proj(mg, w_o_b, xp, g_mix_post, g_ffn_pre, l)
        xp, hp = ffn(h2, w_up_b, w_down_b, xp, g_ffn_post, g_mix_pre, l, nxt)
        kv_tail = z.reshape(batch, seq, -1)[:, -WINDOW:, COL_K:COL_CQ].astype(F32)
        k_p.append(kv_tail[..., :SWA_KV_WIDTH].reshape(batch, WINDOW, SWA_KV_HEADS, SWA_HEAD_DIM))
        v_p.append(kv_tail[..., SWA_KV_WIDTH:].reshape(batch, WINDOW, SWA_KV_HEADS, SWA_HEAD_DIM))

    mem_shape = (depth, batch, mem_tokens, MEM_HEADS, MEM_HEAD_DIM)
    return (xp.reshape(batch, seq, d), xs.reshape(dec_batch, dec_seq, d),
            jnp.stack(k_p), jnp.stack(v_p), mk_p.reshape(mem_shape), mv_p.reshape(mem_shape),
            jnp.stack(k_s), jnp.stack(v_s), jnp.stack(gv_s))
```

```python
import functools

import jax
import jax.numpy as jnp
from jax import lax
from jax.experimental import pallas as pl
from jax.experimental.pallas import tpu as pltpu

F32 = jnp.float32
BF16 = jnp.bfloat16

CHUNK = 64
GMLP_CHUNK = 128
GMLP_GROUPS = 12
GMLP_GROUP_DIM = 128
GMLP_WIDTH = GMLP_GROUPS * GMLP_GROUP_DIM
SWA_HEADS = 24
SWA_KV_HEADS = 4
SWA_HEAD_DIM = 64
SWA_WIDTH = SWA_HEADS * SWA_HEAD_DIM
SWA_KV_WIDTH = SWA_KV_HEADS * SWA_HEAD_DIM
WINDOW = 128
ROPE_THETA = 500000.0
ROPE_DIM = SWA_HEAD_DIM // 4
MEM_HEADS = 4
MEM_HEAD_DIM = 256
MEM_WIDTH = MEM_HEADS * MEM_HEAD_DIM
PAST_LEN = 1024
EPS = 1e-6
NEG = -1e30

COL_U = 0
COL_V = GMLP_WIDTH
COL_Q = 2 * GMLP_WIDTH
COL_K = COL_Q + SWA_WIDTH
COL_VV = COL_K + SWA_KV_WIDTH
COL_CQ = COL_VV + SWA_KV_WIDTH
COL_GATE = COL_CQ + MEM_WIDTH

LOG2E = 1.4426950408889634
Q_SCALE = SWA_HEAD_DIM ** -0.5 * LOG2E

LANES = 128
BF16_SUBLANES = 16
MERGE_COLS = 512
PROJ_SUB = 256
OUT_PROJ_SUB = 256
FFN_SUB = 256
FFN_CAST_SUB = 512
MERGE_CAST_SUB = 512
SWA_LOOKAHEAD = 3
HEADS_PER_LANE_TILE = LANES // SWA_HEAD_DIM
VMEM_LIMIT = 56 * 1024 * 1024


def _params(*sem):
    return pltpu.CompilerParams(dimension_semantics=sem, vmem_limit_bytes=VMEM_LIMIT)


def _rms(x, g):
    return x * lax.rsqrt(jnp.mean(x * x, axis=-1, keepdims=True) + EPS) * g


def _sub_rows(s, size):
    return slice(s * size, (s + 1) * size)


def _skewed(n, compute, finish):
    nxt = compute(0)
    for s in range(n):
        cur = nxt
        if s + 1 < n:
            nxt = compute(s + 1)
        finish(s, cur)


def _rmsnorm_kernel(x_ref, g_ref, o_ref):
    o_ref[...] = _rms(x_ref[...], g_ref[...]).astype(o_ref.dtype)


def rmsnorm_cast(x, g_all, layer, tm=512):
    m, d = x.shape
    return pl.pallas_call(
        _rmsnorm_kernel,
        grid=(m // tm,),
        in_specs=[pl.BlockSpec((tm, d), lambda i: (i, 0)),
                  pl.BlockSpec((None, 1, d), lambda i: (layer, 0, 0))],
        out_specs=pl.BlockSpec((tm, d), lambda i: (i, 0)),
        out_shape=jax.ShapeDtypeStruct((m, d), BF16),
        compiler_params=_params("parallel"),
        name="rmsnorm_cast",
    )(x, g_all)


def _rope(x, tab):
    return x * tab[0] + pltpu.roll(x, LANES - ROPE_DIM // 2, 1) * tab[1] + pltpu.roll(x, ROPE_DIM // 2, 1) * tab[2]


def rope_table(pos):
    half = ROPE_DIM // 2
    inv = ROPE_THETA ** (-jnp.arange(half, dtype=F32) / half)
    ang = pos.astype(F32)[:, None] * inv[None, :]
    cos, sin = jnp.cos(ang), jnp.sin(ang)
    s = pos.shape[0]
    rest = SWA_HEAD_DIM - ROPE_DIM
    c_head = jnp.concatenate([cos, cos, jnp.ones((s, rest), F32)], axis=1)
    up_head = jnp.concatenate([-sin, jnp.zeros((s, half + rest), F32)], axis=1)
    dn_head = jnp.concatenate([jnp.zeros((s, half), F32), sin, jnp.zeros((s, rest), F32)], axis=1)
    return jnp.stack([jnp.tile(t, (1, HEADS_PER_LANE_TILE)) for t in (c_head, up_head, dn_head)])


def _proj_in_kernel(h_ref, w_ref, tab_ref, o_ref, wb_ref, *, tn):
    j = pl.program_id(0)

    @pl.when(pl.program_id(1) == 0)
    def _():
        wb_ref[...] = w_ref[...].astype(BF16)

    def project(rows):
        return jnp.dot(h_ref[rows, :], wb_ref[...], preferred_element_type=F32)

    def rotary_tile(jj):
        def finish(acc, rows):
            for t in range(tn // LANES):
                col = jj * tn + t * LANES
                x = acc[:, t * LANES:(t + 1) * LANES]
                if col < COL_K:
                    x = _rope(x, tab_ref[:, rows, :]) * Q_SCALE
                elif col < COL_VV:
                    x = _rope(x, tab_ref[:, rows, :])
                o_ref[rows, t * LANES:(t + 1) * LANES] = x.astype(o_ref.dtype)
        return finish

    def plain(acc, rows):
        o_ref[rows, :] = acc.astype(o_ref.dtype)

    def gelu(acc, rows):
        o_ref[rows, :] = jax.nn.gelu(acc).astype(o_ref.dtype)

    def sigmoid(acc, rows):
        o_ref[rows, :] = jax.nn.sigmoid(acc).astype(o_ref.dtype)

    def run(finish):
        _skewed(h_ref.shape[0] // PROJ_SUB, lambda s: project(_sub_rows(s, PROJ_SUB)),
                lambda s, acc: finish(acc, _sub_rows(s, PROJ_SUB)))

    pl.when(j < COL_Q // tn)(lambda: run(gelu))
    for jj in range(COL_Q // tn, COL_GATE // tn):
        pl.when(j == jj)(functools.partial(run, rotary_tile(jj) if jj * tn < COL_VV else plain))
    pl.when(j >= COL_GATE // tn)(lambda: run(sigmoid))


def proj_in(h, w_all, tab, layer, tm=1024, tn=1024):
    m, d = h.shape
    n = w_all.shape[-1]
    tab_blocks = tab.shape[1] // tm

    def uses_tab(j):
        return jnp.logical_and(j >= COL_Q // tn, j * tn < COL_VV)

    return pl.pallas_call(
        functools.partial(_proj_in_kernel, tn=tn),
        grid=(n // tn, m // tm),
        in_specs=[pl.BlockSpec((tm, d), lambda j, i: (i, 0)),
                  pl.BlockSpec((None, d, tn), lambda j, i: (layer, 0, j)),
                  pl.BlockSpec((3, tm, LANES), lambda j, i: (0, jnp.where(uses_tab(j), i % tab_blocks, 0), 0))],
        out_specs=pl.BlockSpec((tm, tn), lambda j, i: (i, j)),
        out_shape=jax.ShapeDtypeStruct((m, n), BF16),
        scratch_shapes=[pltpu.VMEM((d, tn), BF16)],
        compiler_params=_params("arbitrary", "arbitrary"),
        name="proj_in",
    )(h, w_all, tab)


def _gmlp_kernel(u_ref, gv_ref, lng_ref, lnb_ref, ws_ref, bias_ref, o_ref, *v_refs, chunk):
    rows = u_ref.shape[0]
    n_chunks = rows // chunk
    gv = gv_ref[...].astype(F32)
    mu = jnp.mean(gv, axis=-1, keepdims=True)
    dev = gv - mu
    var = jnp.mean(dev * dev, axis=-1, keepdims=True)
    v = dev * lax.rsqrt(var + EPS) * lng_ref[...] + lnb_ref[...]
    if v_refs:
        v_refs[0][...] = v
    vb = v.astype(BF16)
    r_io = lax.broadcasted_iota(jnp.int32, (chunk, chunk), 0)
    c_io = lax.broadcasted_iota(jnp.int32, (chunk, chunk), 1)
    causal = c_io <= r_io
    for g in range(GMLP_GROUPS):
        cols = slice(g * GMLP_GROUP_DIM, (g + 1) * GMLP_GROUP_DIM)
        w = jnp.where(causal, ws_ref[g, :chunk, :chunk], 0.0).astype(BF16)
        vg = jnp.concatenate([vb[c * chunk:(c + 1) * chunk, cols] for c in range(n_chunks)], axis=1)
        s = jnp.dot(w, vg, preferred_element_type=F32)
        bias = bias_ref[:, cols]
        for c in range(n_chunks):
            rs = slice(c * chunk, (c + 1) * chunk)
            sc = s[:, c * GMLP_GROUP_DIM:(c + 1) * GMLP_GROUP_DIM] + bias
            o_ref[rs, cols] = (u_ref[rs, cols].astype(F32) * sc).astype(o_ref.dtype)


def gmlp(z, ln_g, ln_b, w_s, bias_tab, layer, chunk, emit_v, rows=1024):
    m = z.shape[0]
    wdt = GMLP_WIDTH
    out_shape = [jax.ShapeDtypeStruct((m, wdt), BF16)]
    out_specs = [pl.BlockSpec((rows, wdt), lambda i: (i, 0))]
    if emit_v:
        out_shape.append(jax.ShapeDtypeStruct((m, wdt), F32))
        out_specs.append(pl.BlockSpec((rows, wdt), lambda i: (i, 0)))
    res = pl.pallas_call(
        functools.partial(_gmlp_kernel, chunk=chunk),
        grid=(m // rows,),
        in_specs=[pl.BlockSpec((rows, wdt), lambda i: (i, COL_U // wdt)),
                  pl.BlockSpec((rows, wdt), lambda i: (i, COL_V // wdt)),
                  pl.BlockSpec((None, 1, wdt), lambda i: (layer, 0, 0)),
                  pl.BlockSpec((None, 1, wdt), lambda i: (layer, 0, 0)),
                  pl.BlockSpec((None, GMLP_GROUPS, GMLP_CHUNK, GMLP_CHUNK), lambda i: (layer, 0, 0, 0)),
                  pl.BlockSpec((None, chunk, wdt), lambda i: (layer, 0, 0))],
        out_specs=out_specs,
        out_shape=out_shape,
        compiler_params=_params("parallel"),
        name="gmlp_c%d" % chunk,
    )(z, z, ln_g, ln_b, w_s, bias_tab)
    return res if emit_v else (res[0], None)


def _swa_core(q_ref, k_all, v_all, valid, sink_ref, o_ref):
    rows = q_ref.shape[0]
    nk = k_all.shape[0]
    tiles_per_group = SWA_HEADS // SWA_KV_HEADS // HEADS_PER_LANE_TILE
    low_half = lax.broadcasted_iota(jnp.int32, (nk, LANES), 1) < SWA_HEAD_DIM
    v_t = v_all.T
    ones_rows = jnp.ones((BF16_SUBLANES, nk), BF16)

    def operands(kvh):
        c0 = (kvh // HEADS_PER_LANE_TILE) * LANES
        kk = k_all[:, c0:c0 + LANES]
        kk_sw = pltpu.roll(kk, SWA_HEAD_DIM, 1)
        if kvh % HEADS_PER_LANE_TILE == 0:
            k_ext = [jnp.where(low_half, kk, 0.0), jnp.where(low_half, 0.0, kk_sw)]
        else:
            k_ext = [jnp.where(low_half, kk_sw, 0.0), jnp.where(low_half, 0.0, kk)]
        vt_h = jnp.concatenate([v_t[kvh * SWA_HEAD_DIM:(kvh + 1) * SWA_HEAD_DIM, :].astype(BF16), ones_rows],
                               axis=0)
        q_tiles = []
        for t in range(kvh * tiles_per_group, (kvh + 1) * tiles_per_group):
            qt = q_ref[:, t * LANES:(t + 1) * LANES]
            if rows < LANES:
                qt = jnp.concatenate([qt, jnp.zeros((LANES - rows, LANES), BF16)], axis=0)
            q_tiles.append(qt)
        q_stack = jnp.concatenate(q_tiles, axis=0)
        return [k.astype(BF16) for k in k_ext], vt_h, q_stack

    def scores(ops, e):
        return lax.dot_general(ops[0][e], ops[2], (((1,), (1,)), ((), ())),
                               preferred_element_type=F32)

    def attend(ops, kvh, e, s):
        probs, sink_p = [], []
        for t in range(tiles_per_group):
            sink = sink_ref[HEADS_PER_LANE_TILE * (kvh * tiles_per_group + t) + e] * LOG2E
            st = jnp.where(valid, s[:, t * LANES:(t + 1) * LANES], NEG)
            mx = jnp.maximum(jnp.max(st, axis=0, keepdims=True), sink)
            probs.append(jnp.exp2(st - mx).astype(BF16))
            sink_p.append(jnp.exp2(sink - mx))
        pv = jnp.dot(ops[1], jnp.concatenate(probs, axis=1), preferred_element_type=F32)
        den = pv[SWA_HEAD_DIM:SWA_HEAD_DIM + 1] + jnp.concatenate(sink_p, axis=1)
        return pv[:SWA_HEAD_DIM] * (1.0 / den)

    items = [(kvh, e) for kvh in range(SWA_KV_HEADS) for e in range(HEADS_PER_LANE_TILE)]
    ops_of, pending = {}, []

    def issue(idx):
        kvh, e = items[idx]
        if kvh not in ops_of:
            ops_of[kvh] = operands(kvh)
        pending.append(scores(ops_of[kvh], e))

    for idx in range(min(SWA_LOOKAHEAD, len(items))):
        issue(idx)
    halves = []
    for idx, (kvh, e) in enumerate(items):
        if idx + SWA_LOOKAHEAD < len(items):
            issue(idx + SWA_LOOKAHEAD)
        halves.append(attend(ops_of[kvh], kvh, e, pending.pop(0)))
        if e == HEADS_PER_LANE_TILE - 1:
            o_t = jnp.concatenate(halves, axis=0)
            halves = []
            for t in range(tiles_per_group):
                tile = kvh * tiles_per_group + t
                o_tile = o_t[:, t * LANES:(t + 1) * LANES].T
                o_ref[:, tile * LANES:(tile + 1) * LANES] = o_tile[:rows].astype(o_ref.dtype)


def _swa_prompt_kernel(sink_ref, q_ref, kc_ref, kp_ref, vc_ref, vp_ref, o_ref):
    tq = q_ref.shape[0]
    k_all = jnp.concatenate([kp_ref[...], kc_ref[...]], axis=0).astype(F32)
    v_all = jnp.concatenate([vp_ref[...], vc_ref[...]], axis=0).astype(F32)
    nk = k_all.shape[0]
    k_row = lax.broadcasted_iota(jnp.int32, (nk, tq), 0)
    k_chunk = k_row // CHUNK
    q_chunk = lax.broadcasted_iota(jnp.int32, (nk, tq), 1) // CHUNK
    first_valid = jnp.where(pl.program_id(1) == 0, WINDOW, 0)
    valid = (k_chunk >= q_chunk) & (k_chunk <= q_chunk + WINDOW // CHUNK) & (k_row >= first_valid)
    _swa_core(q_ref, k_all, v_all, valid, sink_ref, o_ref)


def swa_prompt(z, sinks, layer, batch, seq):
    tq = WINDOW
    nqb = seq // tq
    m = z.shape[0]

    def prev(i):
        return jnp.maximum(i - 1, 0)

    return pl.pallas_call(
        _swa_prompt_kernel,
        grid=(batch, nqb),
        in_specs=[pl.BlockSpec(memory_space=pltpu.SMEM),
                  pl.BlockSpec((tq, SWA_WIDTH), lambda b, i: (b * nqb + i, COL_Q // SWA_WIDTH)),
                  pl.BlockSpec((tq, SWA_KV_WIDTH), lambda b, i: (b * nqb + i, COL_K // SWA_KV_WIDTH)),
                  pl.BlockSpec((tq, SWA_KV_WIDTH), lambda b, i: (b * nqb + prev(i), COL_K // SWA_KV_WIDTH)),
                  pl.BlockSpec((tq, SWA_KV_WIDTH), lambda b, i: (b * nqb + i, COL_VV // SWA_KV_WIDTH)),
                  pl.BlockSpec((tq, SWA_KV_WIDTH), lambda b, i: (b * nqb + prev(i), COL_VV // SWA_KV_WIDTH))],
        out_specs=pl.BlockSpec((tq, SWA_WIDTH), lambda b, i: (b * nqb + i, 0)),
        out_shape=jax.ShapeDtypeStruct((m, SWA_WIDTH), BF16),
        compiler_params=_params("parallel", "arbitrary"),
        name="swa_prompt",
    )(sinks[layer], z, z, z, z, z)


def _swa_sample_kernel(sink_ref, q_ref, kn_ref, vn_ref, ck_ref, cv_ref, o_ref):
    tq = q_ref.shape[0]
    pad = jnp.zeros((WINDOW - tq, SWA_KV_WIDTH), F32)
    k_all = jnp.concatenate([ck_ref[...], kn_ref[...].astype(F32), pad], axis=0)
    v_all = jnp.concatenate([cv_ref[...], vn_ref[...].astype(F32), pad], axis=0)
    valid = lax.broadcasted_iota(jnp.int32, (k_all.shape[0], LANES), 0) < WINDOW + tq
    _swa_core(q_ref, k_all, v_all, valid, sink_ref, o_ref)


def swa_sample(z, cache_k, cache_v, sinks, layer, batch, seq):
    m = z.shape[0]
    return pl.pallas_call(
        _swa_sample_kernel,
        grid=(batch,),
        in_specs=[pl.BlockSpec(memory_space=pltpu.SMEM),
                  pl.BlockSpec((seq, SWA_WIDTH), lambda b: (b, COL_Q // SWA_WIDTH)),
                  pl.BlockSpec((seq, SWA_KV_WIDTH), lambda b: (b, COL_K // SWA_KV_WIDTH)),
                  pl.BlockSpec((seq, SWA_KV_WIDTH), lambda b: (b, COL_VV // SWA_KV_WIDTH)),
                  pl.BlockSpec((None, None, WINDOW, SWA_KV_WIDTH), lambda b: (layer, b, 0, 0)),
                  pl.BlockSpec((None, None, WINDOW, SWA_KV_WIDTH), lambda b: (layer, b, 0, 0))],
        out_specs=pl.BlockSpec((seq, SWA_WIDTH), lambda b: (b, 0)),
        out_shape=jax.ShapeDtypeStruct((m, SWA_WIDTH), BF16),
        compiler_params=_params("parallel"),
        name="swa_sample",
    )(sinks[layer], z, z, z, cache_k, cache_v)


def _memory_kv_kernel(mem_ref, g_ref, wk_ref, wv_ref, k_ref, v_ref, wkb_ref, wvb_ref):
    @pl.when(pl.program_id(1) == 0)
    def _():
        wkb_ref[...] = wk_ref[...].astype(BF16)
        wvb_ref[...] = wv_ref[...].astype(BF16)

    h = _rms(mem_ref[...], g_ref[...]).astype(BF16)
    k_ref[...] = jnp.dot(h, wkb_ref[...], preferred_element_type=F32)
    v_ref[...] = jnp.dot(h, wvb_ref[...], preferred_element_type=F32)


def memory_kv(mem, g_all, wk_all, wv_all):
    batch, tokens, d = mem.shape
    depth = wk_all.shape[0]
    shp = jax.ShapeDtypeStruct((depth, batch, tokens, MEM_WIDTH), F32)
    return pl.pallas_call(
        _memory_kv_kernel,
        grid=(depth, batch),
        in_specs=[pl.BlockSpec((None, tokens, d), lambda l, b: (b, 0, 0)),
                  pl.BlockSpec((None, 1, d), lambda l, b: (l, 0, 0)),
                  pl.BlockSpec((None, d, MEM_WIDTH), lambda l, b: (l, 0, 0)),
                  pl.BlockSpec((None, d, MEM_WIDTH), lambda l, b: (l, 0, 0))],
        out_specs=[pl.BlockSpec((None, None, tokens, MEM_WIDTH), lambda l, b: (l, b, 0, 0)),
                   pl.BlockSpec((None, None, tokens, MEM_WIDTH), lambda l, b: (l, b, 0, 0))],
        out_shape=[shp, shp],
        scratch_shapes=[pltpu.VMEM((d, MEM_WIDTH), BF16), pltpu.VMEM((d, MEM_WIDTH), BF16)],
        compiler_params=_params("arbitrary", "arbitrary"),
        name="memory_kv",
    )(mem, g_all, wk_all, wv_all)


def _head_cols(h):
    return slice(h * MEM_HEAD_DIM, (h + 1) * MEM_HEAD_DIM)


def _mem_attention(cq_ref, head_k, head_v, o_ref):
    scale = MEM_HEAD_DIM ** -0.5

    def scores(h):
        return lax.dot_general(cq_ref[:, _head_cols(h)], head_k(h).astype(BF16), (((1,), (1,)), ((), ())),
                               preferred_element_type=F32) * scale

    def attend(h, s):
        mx = jnp.max(s, axis=-1, keepdims=True)
        p = jnp.exp(s - mx)
        den = jnp.sum(p, axis=-1, keepdims=True)
        p = (p * (1.0 / den)).astype(BF16)
        o_ref[:, _head_cols(h)] = jnp.dot(p, head_v(h).astype(BF16),
                                          preferred_element_type=F32).astype(o_ref.dtype)

    _skewed(MEM_HEADS, scores, attend)


def _mem_attn_kernel(cq_ref, mk_ref, mv_ref, o_ref):
    _mem_attention(cq_ref, lambda h: mk_ref[:, _head_cols(h)], lambda h: mv_ref[:, _head_cols(h)], o_ref)


def mem_attn(z, mk_all, mv_all, layer, batch, seq, tq):
    m = z.shape[0]
    nqb = seq // tq
    tokens = mk_all.shape[2]
    kv_spec = pl.BlockSpec((None, None, tokens, MEM_WIDTH), lambda b, i: (layer, b, 0, 0))
    return pl.pallas_call(
        _mem_attn_kernel,
        grid=(batch, nqb),
        in_specs=[pl.BlockSpec((tq, MEM_WIDTH), lambda b, i: (b * nqb + i, COL_CQ // MEM_WIDTH)),
                  kv_spec, kv_spec],
        out_specs=pl.BlockSpec((tq, MEM_WIDTH), lambda b, i: (b * nqb + i, 0)),
        out_shape=jax.ShapeDtypeStruct((m, MEM_WIDTH), BF16),
        compiler_params=_params("parallel", "arbitrary"),
        name="mem_attn",
    )(z, mk_all, mv_all)


def _cache_head_copies(k_hbm, v_hbm, k_buf, v_buf, sems, layer, b, slot):
    copies = []
    for h in range(MEM_HEADS):
        copies.append(pltpu.make_async_copy(k_hbm.at[layer, b, :, h, :], k_buf.at[slot, h], sems.at[slot, h]))
        copies.append(pltpu.make_async_copy(v_hbm.at[layer, b, :, h, :], v_buf.at[slot, h],
                                            sems.at[slot, MEM_HEADS + h]))
    return copies


def _mem_attn_cache_kernel(cq_ref, k_hbm, v_hbm, o_ref, k_buf, v_buf, sems, *, layer):
    b = pl.program_id(0)
    slot = b % 2

    def copies(batch_index, into):
        return _cache_head_copies(k_hbm, v_hbm, k_buf, v_buf, sems, layer, batch_index, into)

    @pl.when(b == 0)
    def _():
        for c in copies(0, 0):
            c.start()

    @pl.when(b + 1 < pl.num_programs(0))
    def _():
        for c in copies(b + 1, 1 - slot):
            c.start()

    for c in copies(b, slot):
        c.wait()
    _mem_attention(cq_ref, lambda h: k_buf[slot, h], lambda h: v_buf[slot, h], o_ref)


def mem_attn_cache(z, k_cache, v_cache, layer, batch, seq):
    m = z.shape[0]
    tokens = k_cache.shape[2]
    buf = pltpu.VMEM((2, MEM_HEADS, tokens, MEM_HEAD_DIM), F32)
    return pl.pallas_call(
        functools.partial(_mem_attn_cache_kernel, layer=layer),
        grid=(batch,),
        in_specs=[pl.BlockSpec((seq, MEM_WIDTH), lambda b: (b, COL_CQ // MEM_WIDTH)),
                  pl.BlockSpec(memory_space=pl.ANY),
                  pl.BlockSpec(memory_space=pl.ANY)],
        out_specs=pl.BlockSpec((seq, MEM_WIDTH), lambda b: (b, 0)),
        out_shape=jax.ShapeDtypeStruct((m, MEM_WIDTH), BF16),
        scratch_shapes=[buf, buf, pltpu.SemaphoreType.DMA((2, 2 * MEM_HEADS))],
        compiler_params=_params("arbitrary"),
        name="mem_attn_cache",
    )(z, k_cache, v_cache)


def _merge_kernel(oa_ref, ob_ref, oc_ref, w_ref, g_ref, o_ref):
    d = o_ref.shape[1]
    ka, kb = GMLP_WIDTH, GMLP_WIDTH + SWA_WIDTH
    for c in range(d // MERGE_COLS):
        cols = slice(c * MERGE_COLS, (c + 1) * MERGE_COLS)

        def gate(k):
            return g_ref[:, k * d + c * MERGE_COLS:k * d + (c + 1) * MERGE_COLS].astype(F32)

        acc = gate(0) * jnp.dot(oa_ref[...], w_ref[0:ka, cols], preferred_element_type=F32)
        acc = acc + gate(1) * jnp.dot(ob_ref[...], w_ref[ka:kb, cols], preferred_element_type=F32)
        acc = acc + gate(2) * jnp.dot(oc_ref[...], w_ref[kb:, cols], preferred_element_type=F32)
        o_ref[:, cols] = acc.astype(o_ref.dtype)


def merge(z, oa, ob, oc, w, tm=512):
    m = z.shape[0]
    k, d = w.shape
    return pl.pallas_call(
        _merge_kernel,
        grid=(m // tm,),
        in_specs=[pl.BlockSpec((tm, GMLP_WIDTH), lambda i: (i, 0)),
                  pl.BlockSpec((tm, SWA_WIDTH), lambda i: (i, 0)),
                  pl.BlockSpec((tm, MEM_WIDTH), lambda i: (i, 0)),
                  pl.BlockSpec((k, d), lambda i: (0, 0), pipeline_mode=pl.Buffered(1)),
                  pl.BlockSpec((tm, 3 * d), lambda i: (i, COL_GATE // (3 * d)))],
        out_specs=pl.BlockSpec((tm, d), lambda i: (i, 0)),
        out_shape=jax.ShapeDtypeStruct((m, d), BF16),
        compiler_params=_params("parallel"),
        name="merge",
    )(oa, ob, oc, w, z)


def _merge_cast_kernel(oa_ref, ob_ref, oc_ref, wa_ref, wb_ref, wc_ref, wo_ref, ga_ref, gb_ref, gc_ref,
                       o_ref, wp_out_ref, wo_out_ref):
    ka, kb = GMLP_WIDTH, GMLP_WIDTH + SWA_WIDTH
    wp_out_ref[0:ka, :] = wa_ref[...].astype(BF16)
    wp_out_ref[ka:kb, :] = wb_ref[...].astype(BF16)
    wp_out_ref[kb:, :] = wc_ref[...].astype(BF16)
    wo_out_ref[...] = wo_ref[...].astype(BF16)
    for s in range(oa_ref.shape[0] // MERGE_CAST_SUB):
        rows = _sub_rows(s, MERGE_CAST_SUB)
        acc = ga_ref[rows, :].astype(F32) * jnp.dot(oa_ref[rows, :], wp_out_ref[0:ka, :],
                                                    preferred_element_type=F32)
        acc = acc + gb_ref[rows, :].astype(F32) * jnp.dot(ob_ref[rows, :], wp_out_ref[ka:kb, :],
                                                          preferred_element_type=F32)
        acc = acc + gc_ref[rows, :].astype(F32) * jnp.dot(oc_ref[rows, :], wp_out_ref[kb:, :],
                                                          preferred_element_type=F32)
        o_ref[rows, :] = acc.astype(o_ref.dtype)


def merge_cast(z, oa, ob, oc, wa_all, wb_all, wc_all, wo_all, layer, tc=256):
    m = z.shape[0]
    d = wa_all.shape[-1]
    g0 = COL_GATE // tc
    gs = d // tc
    resident = lambda width: pl.BlockSpec((m, width), lambda c: (0, 0), pipeline_mode=pl.Buffered(1))
    weight = lambda k: pl.BlockSpec((None, k, tc), lambda c: (layer, 0, c))
    return pl.pallas_call(
        _merge_cast_kernel,
        grid=(d // tc,),
        in_specs=[resident(GMLP_WIDTH), resident(SWA_WIDTH), resident(MEM_WIDTH),
                  weight(GMLP_WIDTH), weight(SWA_WIDTH), weight(MEM_WIDTH), weight(d),
                  pl.BlockSpec((m, tc), lambda c: (0, g0 + c)),
                  pl.BlockSpec((m, tc), lambda c: (0, g0 + gs + c)),
                  pl.BlockSpec((m, tc), lambda c: (0, g0 + 2 * gs + c))],
        out_specs=[pl.BlockSpec((m, tc), lambda c: (0, c)),
                   pl.BlockSpec((GMLP_WIDTH + SWA_WIDTH + MEM_WIDTH, tc), lambda c: (0, c)),
                   pl.BlockSpec((d, tc), lambda c: (0, c))],
        out_shape=[jax.ShapeDtypeStruct((m, d), BF16),
                   jax.ShapeDtypeStruct((GMLP_WIDTH + SWA_WIDTH + MEM_WIDTH, d), BF16),
                   jax.ShapeDtypeStruct((d, d), BF16)],
        compiler_params=_params("parallel"),
        name="merge_cast",
    )(oa, ob, oc, wa_all, wb_all, wc_all, wo_all, z, z, z)


def _out_proj_kernel(mg_ref, w_ref, x_ref, gpost_ref, gnext_ref, xo_ref, ho_ref):
    def mix(s):
        return jnp.dot(mg_ref[_sub_rows(s, OUT_PROJ_SUB), :], w_ref[...], preferred_element_type=F32)

    def finish(s, mixed):
        rows = _sub_rows(s, OUT_PROJ_SUB)
        xn = x_ref[rows, :] + _rms(mixed, gpost_ref[...])
        xo_ref[rows, :] = xn
        ho_ref[rows, :] = _rms(xn, gnext_ref[...]).astype(ho_ref.dtype)

    _skewed(mg_ref.shape[0] // OUT_PROJ_SUB, mix, finish)


def out_proj(mg, w, x, gpost_all, gnext_all, layer, tm=512):
    m, d = x.shape
    return pl.pallas_call(
        _out_proj_kernel,
        grid=(m // tm,),
        in_specs=[pl.BlockSpec((tm, d), lambda i: (i, 0)),
                  pl.BlockSpec((d, d), lambda i: (0, 0), pipeline_mode=pl.Buffered(1)),
                  pl.BlockSpec((tm, d), lambda i: (i, 0)),
                  pl.BlockSpec((None, 1, d), lambda i: (layer, 0, 0)),
                  pl.BlockSpec((None, 1, d), lambda i: (layer, 0, 0))],
        out_specs=[pl.BlockSpec((tm, d), lambda i: (i, 0)),
                   pl.BlockSpec((tm, d), lambda i: (i, 0))],
        out_shape=[jax.ShapeDtypeStruct((m, d), F32), jax.ShapeDtypeStruct((m, d), BF16)],
        compiler_params=_params("parallel"),
        name="out_proj",
    )(mg, w, x, gpost_all, gnext_all)


def _ffn_kernel(h_ref, wu_ref, wd_ref, x_ref, gpost_ref, *rest, emit_next):
    if emit_next:
        gnext_ref, xo_ref, ho_ref = rest
    else:
        (xo_ref,) = rest
    j = pl.program_id(1)

    @pl.when(j == 0)
    def _():
        xo_ref[...] = jnp.zeros_like(xo_ref)

    def partial_sum(rows):
        a = jnp.dot(h_ref[rows, :], wu_ref[...], preferred_element_type=F32)
        a = jnp.square(jnp.maximum(a, 0.0)).astype(BF16)
        return jnp.dot(a, wd_ref[...], preferred_element_type=F32)

    last = pl.num_programs(1) - 1

    @pl.when(j < last)
    def _():
        xo_ref[...] += partial_sum(slice(None))

    @pl.when(j == last)
    def _():
        def finish(s, part):
            rows = _sub_rows(s, FFN_SUB)
            xn = x_ref[rows, :] + _rms(xo_ref[rows, :] + part, gpost_ref[...])
            xo_ref[rows, :] = xn
            if emit_next:
                ho_ref[rows, :] = _rms(xn, gnext_ref[...]).astype(ho_ref.dtype)

        _skewed(h_ref.shape[0] // FFN_SUB, lambda s: partial_sum(_sub_rows(s, FFN_SUB)), finish)


def ffn(h, wu, wd, x, gpost_all, gnext_all, layer, next_layer, tm=512, th=1024):
    m, d = x.shape
    hidden = wu.shape[-1]
    emit_next = next_layer is not None
    in_specs = [pl.BlockSpec((tm, d), lambda i, j: (i, 0)),
                pl.BlockSpec((d, th), lambda i, j: (0, j)),
                pl.BlockSpec((th, d), lambda i, j: (j, 0)),
                pl.BlockSpec((tm, d), lambda i, j: (i, 0)),
                pl.BlockSpec((None, 1, d), lambda i, j: (layer, 0, 0))]
    args = [h, wu, wd, x, gpost_all]
    out_specs = [pl.BlockSpec((tm, d), lambda i, j: (i, 0))]
    out_shape = [jax.ShapeDtypeStruct((m, d), F32)]
    if emit_next:
        in_specs.append(pl.BlockSpec((None, 1, d), lambda i, j: (next_layer, 0, 0)))
        args.append(gnext_all)
        out_specs.append(pl.BlockSpec((tm, d), lambda i, j: (i, 0)))
        out_shape.append(jax.ShapeDtypeStruct((m, d), BF16))
    res = pl.pallas_call(
        functools.partial(_ffn_kernel, emit_next=emit_next),
        grid=(m // tm, hidden // th),
        in_specs=in_specs,
        out_specs=out_specs,
        out_shape=out_shape,
        compiler_params=_params("parallel", "arbitrary"),
        name="ffn",
    )(*args)
    return (res[0], res[1]) if emit_next else (res[0], None)


def _ffn_up_cast_kernel(h_ref, wu_ref, a_ref, wub_ref):
    wub_ref[...] = wu_ref[...].astype(BF16)

    def up(s):
        return jnp.dot(h_ref[_sub_rows(s, FFN_CAST_SUB), :], wub_ref[...], preferred_element_type=F32)

    def finish(s, a):
        a_ref[_sub_rows(s, FFN_CAST_SUB), :] = jnp.square(jnp.maximum(a, 0.0)).astype(a_ref.dtype)

    _skewed(h_ref.shape[0] // FFN_CAST_SUB, up, finish)


def ffn_up_cast(h, wu_all, layer, th=1024):
    m, d = h.shape
    hidden = wu_all.shape[-1]
    return pl.pallas_call(
        _ffn_up_cast_kernel,
        grid=(hidden // th,),
        in_specs=[pl.BlockSpec((m, d), lambda j: (0, 0), pipeline_mode=pl.Buffered(1)),
                  pl.BlockSpec((None, d, th), lambda j: (layer, 0, j))],
        out_specs=[pl.BlockSpec((m, th), lambda j: (0, j)),
                   pl.BlockSpec((d, th), lambda j: (0, j))],
        out_shape=[jax.ShapeDtypeStruct((m, hidden), BF16), jax.ShapeDtypeStruct((d, hidden), BF16)],
        compiler_params=_params("parallel"),
        name="ffn_up_cast",
    )(h, wu_all)


def _ffn_down_cast_kernel(a_ref, wd_ref, acc_ref, wdb_ref):
    @pl.when(pl.program_id(0) == 0)
    def _():
        acc_ref[...] = jnp.zeros_like(acc_ref)

    wdb_ref[...] = wd_ref[...].astype(BF16)
    for s in range(a_ref.shape[0] // FFN_CAST_SUB):
        rows = _sub_rows(s, FFN_CAST_SUB)
        acc_ref[rows, :] += jnp.dot(a_ref[rows, :], wdb_ref[...], preferred_element_type=F32)


def ffn_down_cast(a, wd_all, layer, tk=1024):
    m, hidden = a.shape
    d = wd_all.shape[-1]
    return pl.pallas_call(
        _ffn_down_cast_kernel,
        grid=(hidden // tk,),
        in_specs=[pl.BlockSpec((m, tk), lambda k: (0, k)),
                  pl.BlockSpec((None, tk, d), lambda k: (layer, k, 0))],
        out_specs=[pl.BlockSpec((m, d), lambda k: (0, 0), pipeline_mode=pl.Buffered(1)),
                   pl.BlockSpec((tk, d), lambda k: (k, 0))],
        out_shape=[jax.ShapeDtypeStruct((m, d), F32), jax.ShapeDtypeStruct((hidden, d), BF16)],
        compiler_params=_params("arbitrary"),
        name="ffn_down_cast",
    )(a, wd_all)


def _ffn_residual_kernel(acc_ref, x_ref, gpost_ref, *rest):
    xn = x_ref[...] + _rms(acc_ref[...], gpost_ref[...])
    if len(rest) == 3:
        gnext_ref, xo_ref, ho_ref = rest
        ho_ref[...] = _rms(xn, gnext_ref[...]).astype(ho_ref.dtype)
    else:
        (xo_ref,) = rest
    xo_ref[...] = xn


def ffn_residual(acc, x, gpost_all, gnext_all, layer, next_layer, tm=512):
    m, d = x.shape
    emit_next = next_layer is not None
    row_spec = pl.BlockSpec((tm, d), lambda i: (i, 0))
    in_specs = [row_spec, row_spec, pl.BlockSpec((None, 1, d), lambda i: (layer, 0, 0))]
    args = [acc, x, gpost_all]
    out_specs, out_shape = [row_spec], [jax.ShapeDtypeStruct((m, d), F32)]
    if emit_next:
        in_specs.append(pl.BlockSpec((None, 1, d), lambda i: (next_layer, 0, 0)))
        args.append(gnext_all)
        out_specs.append(row_spec)
        out_shape.append(jax.ShapeDtypeStruct((m, d), BF16))
    res = pl.pallas_call(
        _ffn_residual_kernel,
        grid=(m // tm,),
        in_specs=in_specs,
        out_specs=out_specs,
        out_shape=out_shape,
        compiler_params=_params("parallel"),
        name="ffn_residual",
    )(*args)
    return (res[0], res[1]) if emit_next else (res[0], None)


def kernel(x_prompt, x_sample, cache_swa_k, cache_swa_v, cache_mem_k, cache_mem_v, mem_prompt,
           w_in, ln_v_g, ln_v_b, w_s, b_s, sinks, mem_norm, w_mem_k, w_mem_v,
           w_pa, w_pb, w_pc, w_o, norm_mix_pre, norm_mix_post, norm_ffn_pre, norm_ffn_post,
           w_up, w_down):
    batch, seq, d = x_prompt.shape
    dec_batch, dec_seq, _ = x_sample.shape
    depth = w_in.shape[0]
    mem_tokens = mem_prompt.shape[1]

    row = lambda g: g.reshape(depth, 1, g.shape[-1])
    g_mix_pre, g_mix_post = row(norm_mix_pre), row(norm_mix_post)
    g_ffn_pre, g_ffn_post = row(norm_ffn_pre), row(norm_ffn_post)
    g_mem, ln_g, ln_b = row(mem_norm), row(ln_v_g), row(ln_v_b)
    bias_p = jnp.repeat(jnp.swapaxes(b_s[:, :, :GMLP_CHUNK], 1, 2), GMLP_GROUP_DIM, axis=2)
    bias_s = jnp.repeat(jnp.swapaxes(b_s[:, :, :dec_seq], 1, 2), GMLP_GROUP_DIM, axis=2)
    proj_rows = 1024
    tab_p = rope_table(jnp.arange(seq, dtype=jnp.int32))
    tab_s = jnp.tile(rope_table(PAST_LEN + jnp.arange(dec_seq, dtype=jnp.int32)), (1, proj_rows // dec_seq, 1))
    cache_k = cache_swa_k.reshape(depth, dec_batch, WINDOW, SWA_KV_WIDTH)
    cache_v = cache_swa_v.reshape(depth, dec_batch, WINDOW, SWA_KV_WIDTH)

    mk_p, mv_p = memory_kv(mem_prompt, g_mem, w_mem_k, w_mem_v)

    xp = x_prompt.reshape(batch * seq, d)
    xs = x_sample.reshape(dec_batch * dec_seq, d)
    hp = rmsnorm_cast(xp, g_mix_pre, 0)
    hs = rmsnorm_cast(xs, g_mix_pre, 0)

    k_p, v_p, k_s, v_s, gv_s = [], [], [], [], []
    for l in range(depth):
        nxt = l + 1 if l + 1 < depth else None

        z = proj_in(hs, w_in, tab_s, l, tm=proj_rows)
        oa, v_rows = gmlp(z, ln_g, ln_b, w_s, bias_s, l, dec_seq, True)
        ob = swa_sample(z, cache_k, cache_v, sinks, l, dec_batch, dec_seq)
        oc = mem_attn_cache(z, cache_mem_k, cache_mem_v, l, dec_batch, dec_seq)
        mg, w_p_b, w_o_b = merge_cast(z, oa, ob, oc, w_pa, w_pb, w_pc, w_o, l)
        xs, h2 = out_proj(mg, w_o_b, xs, g_mix_post, g_ffn_pre, l)
        act, w_up_b = ffn_up_cast(h2, w_up, l)
        acc, w_down_b = ffn_down_cast(act, w_down, l)
        xs, hs = ffn_residual(acc, xs, g_ffn_post, g_mix_pre, l, nxt)
        k_s.append(z[:, COL_K:COL_VV].astype(F32).reshape(dec_batch, dec_seq, SWA_KV_HEADS, SWA_HEAD_DIM))
        v_s.append(z[:, COL_VV:COL_CQ].astype(F32).reshape(dec_batch, dec_seq, SWA_KV_HEADS, SWA_HEAD_DIM))
        gv_s.append(v_rows.reshape(dec_batch, dec_seq, GMLP_WIDTH))

        z = proj_in(hp, w_in, tab_p, l, tm=proj_rows)
        oa, _ = gmlp(z, ln_g, ln_b, w_s, bias_p, l, GMLP_CHUNK, False)
        ob = swa_prompt(z, sinks, l, batch, seq)
        oc = mem_attn(z, mk_p, mv_p, l, batch, seq, 1024)
        mg = merge(z, oa, ob, oc, w_p_b)
        xp, h2 = out_proj(mg, w_o_b, xp, g_mix_post, g_ffn_pre, l)
        xp, hp = ffn(h2, w_up_b, w_down_b, xp, g_ffn_post, g_mix_pre, l, nxt)
        kv_tail = z.reshape(batch, seq, -1)[:, -WINDOW:, COL_K:COL_CQ].astype(F32)
        k_p.append(kv_tail[..., :SWA_KV_WIDTH].reshape(batch, WINDOW, SWA_KV_HEADS, SWA_HEAD_DIM))
        v_p.append(kv_tail[..., SWA_KV_WIDTH:].reshape(batch, WINDOW, SWA_KV_HEADS, SWA_HEAD_DIM))

    mem_shape = (depth, batch, mem_tokens, MEM_HEADS, MEM_HEAD_DIM)
    return (xp.reshape(batch, seq, d), xs.reshape(dec_batch, dec_seq, d),
            jnp.stack(k_p), jnp.stack(v_p), mk_p.reshape(mem_shape), mv_p.reshape(mem_shape),
            jnp.stack(k_s), jnp.stack(v_s), jnp.stack(gv_s))
```

```python
import functools

import jax
import jax.numpy as jnp
from jax import lax
from jax.experimental import pallas as pl
from jax.experimental.pallas import tpu as pltpu

F32 = jnp.float32
BF16 = jnp.bfloat16

CHUNK = 64
GMLP_CHUNK = 128
GMLP_GROUPS = 12
GMLP_GROUP_DIM = 128
GMLP_WIDTH = GMLP_GROUPS * GMLP_GROUP_DIM
SWA_HEADS = 24
SWA_KV_HEADS = 4
SWA_HEAD_DIM = 64
SWA_WIDTH = SWA_HEADS * SWA_HEAD_DIM
SWA_KV_WIDTH = SWA_KV_HEADS * SWA_HEAD_DIM
WINDOW = 128
ROPE_THETA = 500000.0
ROPE_DIM = SWA_HEAD_DIM // 4
MEM_HEADS = 4
MEM_HEAD_DIM = 256
MEM_WIDTH = MEM_HEADS * MEM_HEAD_DIM
PAST_LEN = 1024
EPS = 1e-6
NEG = -1e30

COL_U = 0
COL_V = GMLP_WIDTH
COL_Q = 2 * GMLP_WIDTH
COL_K = COL_Q + SWA_WIDTH
COL_VV = COL_K + SWA_KV_WIDTH
COL_CQ = COL_VV + SWA_KV_WIDTH
COL_GATE = COL_CQ + MEM_WIDTH

LOG2E = 1.4426950408889634
Q_SCALE = SWA_HEAD_DIM ** -0.5 * LOG2E

LANES = 128
BF16_SUBLANES = 16
MERGE_COLS = 512
PROJ_SUB = 256
OUT_PROJ_SUB = 256
FFN_SUB = 256
FFN_CAST_SUB = 512
MERGE_CAST_SUB = 512
CACHE_RING_SLOTS = 3
SWA_LOOKAHEAD = 3
HEADS_PER_LANE_TILE = LANES // SWA_HEAD_DIM
VMEM_LIMIT = 56 * 1024 * 1024


def _params(*sem):
    return pltpu.CompilerParams(dimension_semantics=sem, vmem_limit_bytes=VMEM_LIMIT)


def _rms(x, g):
    return x * lax.rsqrt(jnp.mean(x * x, axis=-1, keepdims=True) + EPS) * g


def _sub_rows(s, size):
    return slice(s * size, (s + 1) * size)


def _skewed(n, compute, finish):
    nxt = compute(0)
    for s in range(n):
        cur = nxt
        if s + 1 < n:
            nxt = compute(s + 1)
        finish(s, cur)


def _rmsnorm_kernel(x_ref, g_ref, o_ref):
    o_ref[...] = _rms(x_ref[...], g_ref[...]).astype(o_ref.dtype)


def rmsnorm_cast(x, g_all, layer, tm=512):
    m, d = x.shape
    return pl.pallas_call(
        _rmsnorm_kernel,
        grid=(m // tm,),
        in_specs=[pl.BlockSpec((tm, d), lambda i: (i, 0)),
                  pl.BlockSpec((None, 1, d), lambda i: (layer, 0, 0))],
        out_specs=pl.BlockSpec((tm, d), lambda i: (i, 0)),
        out_shape=jax.ShapeDtypeStruct((m, d), BF16),
        compiler_params=_params("parallel"),
        name="rmsnorm_cast",
    )(x, g_all)


def _rope(x, tab):
    return x * tab[0] + pltpu.roll(x, LANES - ROPE_DIM // 2, 1) * tab[1] + pltpu.roll(x, ROPE_DIM // 2, 1) * tab[2]


def rope_table(pos):
    half = ROPE_DIM // 2
    inv = ROPE_THETA ** (-jnp.arange(half, dtype=F32) / half)
    ang = pos.astype(F32)[:, None] * inv[None, :]
    cos, sin = jnp.cos(ang), jnp.sin(ang)
    s = pos.shape[0]
    rest = SWA_HEAD_DIM - ROPE_DIM
    c_head = jnp.concatenate([cos, cos, jnp.ones((s, rest), F32)], axis=1)
    up_head = jnp.concatenate([-sin, jnp.zeros((s, half + rest), F32)], axis=1)
    dn_head = jnp.concatenate([jnp.zeros((s, half), F32), sin, jnp.zeros((s, rest), F32)], axis=1)
    return jnp.stack([jnp.tile(t, (1, HEADS_PER_LANE_TILE)) for t in (c_head, up_head, dn_head)])


def _proj_in_kernel(h_ref, w_ref, tab_ref, o_ref, wb_ref, *, tn):
    j = pl.program_id(0)

    @pl.when(pl.program_id(1) == 0)
    def _():
        wb_ref[...] = w_ref[...].astype(BF16)

    def project(rows):
        return jnp.dot(h_ref[rows, :], wb_ref[...], preferred_element_type=F32)

    def rotary_tile(jj):
        def finish(acc, rows):
            for t in range(tn // LANES):
                col = jj * tn + t * LANES
                x = acc[:, t * LANES:(t + 1) * LANES]
                if col < COL_K:
                    x = _rope(x, tab_ref[:, rows, :]) * Q_SCALE
                elif col < COL_VV:
                    x = _rope(x, tab_ref[:, rows, :])
                o_ref[rows, t * LANES:(t + 1) * LANES] = x.astype(o_ref.dtype)
        return finish

    def plain(acc, rows):
        o_ref[rows, :] = acc.astype(o_ref.dtype)

    def gelu(acc, rows):
        o_ref[rows, :] = jax.nn.gelu(acc).astype(o_ref.dtype)

    def sigmoid(acc, rows):
        o_ref[rows, :] = jax.nn.sigmoid(acc).astype(o_ref.dtype)

    def run(finish):
        _skewed(h_ref.shape[0] // PROJ_SUB, lambda s: project(_sub_rows(s, PROJ_SUB)),
                lambda s, acc: finish(acc, _sub_rows(s, PROJ_SUB)))

    pl.when(j < COL_Q // tn)(lambda: run(gelu))
    for jj in range(COL_Q // tn, COL_GATE // tn):
        pl.when(j == jj)(functools.partial(run, rotary_tile(jj) if jj * tn < COL_VV else plain))
    pl.when(j >= COL_GATE // tn)(lambda: run(sigmoid))


def proj_in(h, w_all, tab, layer, tm=1024, tn=1024):
    m, d = h.shape
    n = w_all.shape[-1]
    tab_blocks = tab.shape[1] // tm

    def uses_tab(j):
        return jnp.logical_and(j >= COL_Q // tn, j * tn < COL_VV)

    return pl.pallas_call(
        functools.partial(_proj_in_kernel, tn=tn),
        grid=(n // tn, m // tm),
        in_specs=[pl.BlockSpec((tm, d), lambda j, i: (i, 0)),
                  pl.BlockSpec((None, d, tn), lambda j, i: (layer, 0, j)),
                  pl.BlockSpec((3, tm, LANES), lambda j, i: (0, jnp.where(uses_tab(j), i % tab_blocks, 0), 0))],
        out_specs=pl.BlockSpec((tm, tn), lambda j, i: (i, j)),
        out_shape=jax.ShapeDtypeStruct((m, n), BF16),
        scratch_shapes=[pltpu.VMEM((d, tn), BF16)],
        compiler_params=_params("arbitrary", "arbitrary"),
        name="proj_in",
    )(h, w_all, tab)


def _gmlp_kernel(u_ref, gv_ref, lng_ref, lnb_ref, ws_ref, bias_ref, o_ref, *v_refs, chunk):
    rows = u_ref.shape[0]
    n_chunks = rows // chunk
    gv = gv_ref[...].astype(F32)
    mu = jnp.mean(gv, axis=-1, keepdims=True)
    dev = gv - mu
    var = jnp.mean(dev * dev, axis=-1, keepdims=True)
    v = dev * lax.rsqrt(var + EPS) * lng_ref[...] + lnb_ref[...]
    if v_refs:
        v_refs[0][...] = v
    vb = v.astype(BF16)
    r_io = lax.broadcasted_iota(jnp.int32, (chunk, chunk), 0)
    c_io = lax.broadcasted_iota(jnp.int32, (chunk, chunk), 1)
    causal = c_io <= r_io
    for g in range(GMLP_GROUPS):
        cols = slice(g * GMLP_GROUP_DIM, (g + 1) * GMLP_GROUP_DIM)
        w = jnp.where(causal, ws_ref[g, :chunk, :chunk], 0.0).astype(BF16)
        vg = jnp.concatenate([vb[c * chunk:(c + 1) * chunk, cols] for c in range(n_chunks)], axis=1)
        s = jnp.dot(w, vg, preferred_element_type=F32)
        bias = bias_ref[:, cols]
        for c in range(n_chunks):
            rs = slice(c * chunk, (c + 1) * chunk)
            sc = s[:, c * GMLP_GROUP_DIM:(c + 1) * GMLP_GROUP_DIM] + bias
            o_ref[rs, cols] = (u_ref[rs, cols].astype(F32) * sc).astype(o_ref.dtype)


def gmlp(z, ln_g, ln_b, w_s, bias_tab, layer, chunk, emit_v, rows=1024):
    m = z.shape[0]
    wdt = GMLP_WIDTH
    out_shape = [jax.ShapeDtypeStruct((m, wdt), BF16)]
    out_specs = [pl.BlockSpec((rows, wdt), lambda i: (i, 0))]
    if emit_v:
        out_shape.append(jax.ShapeDtypeStruct((m, wdt), F32))
        out_specs.append(pl.BlockSpec((rows, wdt), lambda i: (i, 0)))
    res = pl.pallas_call(
        functools.partial(_gmlp_kernel, chunk=chunk),
        grid=(m // rows,),
        in_specs=[pl.BlockSpec((rows, wdt), lambda i: (i, COL_U // wdt)),
                  pl.BlockSpec((rows, wdt), lambda i: (i, COL_V // wdt)),
                  pl.BlockSpec((None, 1, wdt), lambda i: (layer, 0, 0)),
                  pl.BlockSpec((None, 1, wdt), lambda i: (layer, 0, 0)),
                  pl.BlockSpec((None, GMLP_GROUPS, GMLP_CHUNK, GMLP_CHUNK), lambda i: (layer, 0, 0, 0)),
                  pl.BlockSpec((None, chunk, wdt), lambda i: (layer, 0, 0))],
        out_specs=out_specs,
        out_shape=out_shape,
        compiler_params=_params("parallel"),
        name="gmlp_c%d" % chunk,
    )(z, z, ln_g, ln_b, w_s, bias_tab)
    return res if emit_v else (res[0], None)


def _swa_core(q_ref, k_all, v_all, valid, sink_ref, o_ref):
    rows = q_ref.shape[0]
    nk = k_all.shape[0]
    tiles_per_group = SWA_HEADS // SWA_KV_HEADS // HEADS_PER_LANE_TILE
    low_half = lax.broadcasted_iota(jnp.int32, (nk, LANES), 1) < SWA_HEAD_DIM
    v_t = v_all.T
    ones_rows = jnp.ones((BF16_SUBLANES, nk), BF16)

    def operands(kvh):
        c0 = (kvh // HEADS_PER_LANE_TILE) * LANES
        kk = k_all[:, c0:c0 + LANES]
        kk_sw = pltpu.roll(kk, SWA_HEAD_DIM, 1)
        if kvh % HEADS_PER_LANE_TILE == 0:
            k_ext = [jnp.where(low_half, kk, 0.0), jnp.where(low_half, 0.0, kk_sw)]
        else:
            k_ext = [jnp.where(low_half, kk_sw, 0.0), jnp.where(low_half, 0.0, kk)]
        vt_h = jnp.concatenate([v_t[kvh * SWA_HEAD_DIM:(kvh + 1) * SWA_HEAD_DIM, :].astype(BF16), ones_rows],
                               axis=0)
        q_tiles = []
        for t in range(kvh * tiles_per_group, (kvh + 1) * tiles_per_group):
            qt = q_ref[:, t * LANES:(t + 1) * LANES]
            if rows < LANES:
                qt = jnp.concatenate([qt, jnp.zeros((LANES - rows, LANES), BF16)], axis=0)
            q_tiles.append(qt)
        q_stack = jnp.concatenate(q_tiles, axis=0)
        return [k.astype(BF16) for k in k_ext], vt_h, q_stack

    def scores(ops, e):
        return lax.dot_general(ops[0][e], ops[2], (((1,), (1,)), ((), ())),
                               preferred_element_type=F32)

    def attend(ops, kvh, e, s):
        probs, sink_p = [], []
        for t in range(tiles_per_group):
            sink = sink_ref[HEADS_PER_LANE_TILE * (kvh * tiles_per_group + t) + e] * LOG2E
            st = jnp.where(valid, s[:, t * LANES:(t + 1) * LANES], NEG)
            mx = jnp.maximum(jnp.max(st, axis=0, keepdims=True), sink)
            probs.append(jnp.exp2(st - mx).astype(BF16))
            sink_p.append(jnp.exp2(sink - mx))
        pv = jnp.dot(ops[1], jnp.concatenate(probs, axis=1), preferred_element_type=F32)
        den = pv[SWA_HEAD_DIM:SWA_HEAD_DIM + 1] + jnp.concatenate(sink_p, axis=1)
        return pv[:SWA_HEAD_DIM] * (1.0 / den)

    items = [(kvh, e) for kvh in range(SWA_KV_HEADS) for e in range(HEADS_PER_LANE_TILE)]
    ops_of, pending = {}, []

    def issue(idx):
        kvh, e = items[idx]
        if kvh not in ops_of:
            ops_of[kvh] = operands(kvh)
        pending.append(scores(ops_of[kvh], e))

    for idx in range(min(SWA_LOOKAHEAD, len(items))):
        issue(idx)
    halves = []
    for idx, (kvh, e) in enumerate(items):
        if idx + SWA_LOOKAHEAD < len(items):
            issue(idx + SWA_LOOKAHEAD)
        halves.append(attend(ops_of[kvh], kvh, e, pending.pop(0)))
        if e == HEADS_PER_LANE_TILE - 1:
            o_t = jnp.concatenate(halves, axis=0)
            halves = []
            for t in range(tiles_per_group):
                tile = kvh * tiles_per_group + t
                o_tile = o_t[:, t * LANES:(t + 1) * LANES].T
                o_ref[:, tile * LANES:(tile + 1) * LANES] = o_tile[:rows].astype(o_ref.dtype)


def _swa_prompt_kernel(sink_ref, q_ref, kc_ref, kp_ref, vc_ref, vp_ref, o_ref):
    tq = q_ref.shape[0]
    k_all = jnp.concatenate([kp_ref[...], kc_ref[...]], axis=0).astype(F32)
    v_all = jnp.concatenate([vp_ref[...], vc_ref[...]], axis=0).astype(F32)
    nk = k_all.shape[0]
    k_row = lax.broadcasted_iota(jnp.int32, (nk, tq), 0)
    k_chunk = k_row // CHUNK
    q_chunk = lax.broadcasted_iota(jnp.int32, (nk, tq), 1) // CHUNK
    first_valid = jnp.where(pl.program_id(1) == 0, WINDOW, 0)
    valid = (k_chunk >= q_chunk) & (k_chunk <= q_chunk + WINDOW // CHUNK) & (k_row >= first_valid)
    _swa_core(q_ref, k_all, v_all, valid, sink_ref, o_ref)


def swa_prompt(z, sinks, layer, batch, seq):
    tq = WINDOW
    nqb = seq // tq
    m = z.shape[0]

    def prev(i):
        return jnp.maximum(i - 1, 0)

    return pl.pallas_call(
        _swa_prompt_kernel,
        grid=(batch, nqb),
        in_specs=[pl.BlockSpec(memory_space=pltpu.SMEM),
                  pl.BlockSpec((tq, SWA_WIDTH), lambda b, i: (b * nqb + i, COL_Q // SWA_WIDTH)),
                  pl.BlockSpec((tq, SWA_KV_WIDTH), lambda b, i: (b * nqb + i, COL_K // SWA_KV_WIDTH)),
                  pl.BlockSpec((tq, SWA_KV_WIDTH), lambda b, i: (b * nqb + prev(i), COL_K // SWA_KV_WIDTH)),
                  pl.BlockSpec((tq, SWA_KV_WIDTH), lambda b, i: (b * nqb + i, COL_VV // SWA_KV_WIDTH)),
                  pl.BlockSpec((tq, SWA_KV_WIDTH), lambda b, i: (b * nqb + prev(i), COL_VV // SWA_KV_WIDTH))],
        out_specs=pl.BlockSpec((tq, SWA_WIDTH), lambda b, i: (b * nqb + i, 0)),
        out_shape=jax.ShapeDtypeStruct((m, SWA_WIDTH), BF16),
        compiler_params=_params("parallel", "arbitrary"),
        name="swa_prompt",
    )(sinks[layer], z, z, z, z, z)


def _swa_sample_kernel(sink_ref, q_ref, kn_ref, vn_ref, ck_ref, cv_ref, o_ref):
    tq = q_ref.shape[0]
    pad = jnp.zeros((WINDOW - tq, SWA_KV_WIDTH), F32)
    k_all = jnp.concatenate([ck_ref[...], kn_ref[...].astype(F32), pad], axis=0)
    v_all = jnp.concatenate([cv_ref[...], vn_ref[...].astype(F32), pad], axis=0)
    valid = lax.broadcasted_iota(jnp.int32, (k_all.shape[0], LANES), 0) < WINDOW + tq
    _swa_core(q_ref, k_all, v_all, valid, sink_ref, o_ref)


def swa_sample(z, cache_k, cache_v, sinks, layer, batch, seq):
    m = z.shape[0]
    return pl.pallas_call(
        _swa_sample_kernel,
        grid=(batch,),
        in_specs=[pl.BlockSpec(memory_space=pltpu.SMEM),
                  pl.BlockSpec((seq, SWA_WIDTH), lambda b: (b, COL_Q // SWA_WIDTH)),
                  pl.BlockSpec((seq, SWA_KV_WIDTH), lambda b: (b, COL_K // SWA_KV_WIDTH)),
                  pl.BlockSpec((seq, SWA_KV_WIDTH), lambda b: (b, COL_VV // SWA_KV_WIDTH)),
                  pl.BlockSpec((None, None, WINDOW, SWA_KV_WIDTH), lambda b: (layer, b, 0, 0)),
                  pl.BlockSpec((None, None, WINDOW, SWA_KV_WIDTH), lambda b: (layer, b, 0, 0))],
        out_specs=pl.BlockSpec((seq, SWA_WIDTH), lambda b: (b, 0)),
        out_shape=jax.ShapeDtypeStruct((m, SWA_WIDTH), BF16),
        compiler_params=_params("parallel"),
        name="swa_sample",
    )(sinks[layer], z, z, z, cache_k, cache_v)


def _memory_kv_kernel(mem_ref, g_ref, wk_ref, wv_ref, k_ref, v_ref, wkb_ref, wvb_ref):
    @pl.when(pl.program_id(1) == 0)
    def _():
        wkb_ref[...] = wk_ref[...].astype(BF16)
        wvb_ref[...] = wv_ref[...].astype(BF16)

    h = _rms(mem_ref[...], g_ref[...]).astype(BF16)
    k_ref[...] = jnp.dot(h, wkb_ref[...], preferred_element_type=F32)
    v_ref[...] = jnp.dot(h, wvb_ref[...], preferred_element_type=F32)


def memory_kv(mem, g_all, wk_all, wv_all):
    batch, tokens, d = mem.shape
    depth = wk_all.shape[0]
    shp = jax.ShapeDtypeStruct((depth, batch, tokens, MEM_WIDTH), F32)
    return pl.pallas_call(
        _memory_kv_kernel,
        grid=(depth, batch),
        in_specs=[pl.BlockSpec((None, tokens, d), lambda l, b: (b, 0, 0)),
                  pl.BlockSpec((None, 1, d), lambda l, b: (l, 0, 0)),
                  pl.BlockSpec((None, d, MEM_WIDTH), lambda l, b: (l, 0, 0)),
                  pl.BlockSpec((None, d, MEM_WIDTH), lambda l, b: (l, 0, 0))],
        out_specs=[pl.BlockSpec((None, None, tokens, MEM_WIDTH), lambda l, b: (l, b, 0, 0)),
                   pl.BlockSpec((None, None, tokens, MEM_WIDTH), lambda l, b: (l, b, 0, 0))],
        out_shape=[shp, shp],
        scratch_shapes=[pltpu.VMEM((d, MEM_WIDTH), BF16), pltpu.VMEM((d, MEM_WIDTH), BF16)],
        compiler_params=_params("arbitrary", "arbitrary"),
        name="memory_kv",
    )(mem, g_all, wk_all, wv_all)


def _head_cols(h):
    return slice(h * MEM_HEAD_DIM, (h + 1) * MEM_HEAD_DIM)


def _mem_attention(cq_ref, head_k, head_v, o_ref):
    scale = MEM_HEAD_DIM ** -0.5

    def scores(h):
        return lax.dot_general(cq_ref[:, _head_cols(h)], head_k(h).astype(BF16), (((1,), (1,)), ((), ())),
                               preferred_element_type=F32) * scale

    def attend(h, s):
        mx = jnp.max(s, axis=-1, keepdims=True)
        p = jnp.exp(s - mx)
        den = jnp.sum(p, axis=-1, keepdims=True)
        p = (p * (1.0 / den)).astype(BF16)
        o_ref[:, _head_cols(h)] = jnp.dot(p, head_v(h).astype(BF16),
                                          preferred_element_type=F32).astype(o_ref.dtype)

    _skewed(MEM_HEADS, scores, attend)


def _mem_attn_kernel(cq_ref, mk_ref, mv_ref, o_ref):
    _mem_attention(cq_ref, lambda h: mk_ref[:, _head_cols(h)], lambda h: mv_ref[:, _head_cols(h)], o_ref)


def mem_attn(z, mk_all, mv_all, layer, batch, seq, tq):
    m = z.shape[0]
    nqb = seq // tq
    tokens = mk_all.shape[2]
    kv_spec = pl.BlockSpec((None, None, tokens, MEM_WIDTH), lambda b, i: (layer, b, 0, 0))
    return pl.pallas_call(
        _mem_attn_kernel,
        grid=(batch, nqb),
        in_specs=[pl.BlockSpec((tq, MEM_WIDTH), lambda b, i: (b * nqb + i, COL_CQ // MEM_WIDTH)),
                  kv_spec, kv_spec],
        out_specs=pl.BlockSpec((tq, MEM_WIDTH), lambda b, i: (b * nqb + i, 0)),
        out_shape=jax.ShapeDtypeStruct((m, MEM_WIDTH), BF16),
        compiler_params=_params("parallel", "arbitrary"),
        name="mem_attn",
    )(z, mk_all, mv_all)


def _cache_head_copies(k_hbm, v_hbm, k_buf, v_buf, sems, layer, b, slot):
    copies = []
    for h in range(MEM_HEADS):
        copies.append(pltpu.make_async_copy(k_hbm.at[layer, b, :, h, :], k_buf.at[slot, h], sems.at[slot, h]))
        copies.append(pltpu.make_async_copy(v_hbm.at[layer, b, :, h, :], v_buf.at[slot, h],
                                            sems.at[slot, MEM_HEADS + h]))
    return copies


def _mem_attn_cache_kernel(cq_ref, k_hbm, v_hbm, o_ref, k_buf, v_buf, sems, *, layer):
    b = pl.program_id(0)
    n_slots = k_buf.shape[0]
    ahead = n_slots - 1
    slot = b % n_slots

    def copies(batch_index, into):
        return _cache_head_copies(k_hbm, v_hbm, k_buf, v_buf, sems, layer, batch_index, into)

    @pl.when(b == 0)
    def _():
        for first in range(ahead):
            for c in copies(first, first):
                c.start()

    @pl.when(b + ahead < pl.num_programs(0))
    def _():
        for c in copies(b + ahead, (b + ahead) % n_slots):
            c.start()

    for c in copies(b, slot):
        c.wait()
    _mem_attention(cq_ref, lambda h: k_buf[slot, h], lambda h: v_buf[slot, h], o_ref)


def mem_attn_cache(z, k_cache, v_cache, layer, batch, seq):
    m = z.shape[0]
    tokens = k_cache.shape[2]
    assert batch >= CACHE_RING_SLOTS - 1
    buf = pltpu.VMEM((CACHE_RING_SLOTS, MEM_HEADS, tokens, MEM_HEAD_DIM), F32)
    return pl.pallas_call(
        functools.partial(_mem_attn_cache_kernel, layer=layer),
        grid=(batch,),
        in_specs=[pl.BlockSpec((seq, MEM_WIDTH), lambda b: (b, COL_CQ // MEM_WIDTH)),
                  pl.BlockSpec(memory_space=pl.ANY),
                  pl.BlockSpec(memory_space=pl.ANY)],
        out_specs=pl.BlockSpec((seq, MEM_WIDTH), lambda b: (b, 0)),
        out_shape=jax.ShapeDtypeStruct((m, MEM_WIDTH), BF16),
        scratch_shapes=[buf, buf, pltpu.SemaphoreType.DMA((CACHE_RING_SLOTS, 2 * MEM_HEADS))],
        compiler_params=_params("arbitrary"),
        name="mem_attn_cache",
    )(z, k_cache, v_cache)


def _merge_kernel(oa_ref, ob_ref, oc_ref, w_ref, g_ref, o_ref):
    d = o_ref.shape[1]
    ka, kb = GMLP_WIDTH, GMLP_WIDTH + SWA_WIDTH
    for c in range(d // MERGE_COLS):
        cols = slice(c * MERGE_COLS, (c + 1) * MERGE_COLS)

        def gate(k):
            return g_ref[:, k * d + c * MERGE_COLS:k * d + (c + 1) * MERGE_COLS].astype(F32)

        acc = gate(0) * jnp.dot(oa_ref[...], w_ref[0:ka, cols], preferred_element_type=F32)
        acc = acc + gate(1) * jnp.dot(ob_ref[...], w_ref[ka:kb, cols], preferred_element_type=F32)
        acc = acc + gate(2) * jnp.dot(oc_ref[...], w_ref[kb:, cols], preferred_element_type=F32)
        o_ref[:, cols] = acc.astype(o_ref.dtype)


def merge(z, oa, ob, oc, w, tm=512):
    m = z.shape[0]
    k, d = w.shape
    return pl.pallas_call(
        _merge_kernel,
        grid=(m // tm,),
        in_specs=[pl.BlockSpec((tm, GMLP_WIDTH), lambda i: (i, 0)),
                  pl.BlockSpec((tm, SWA_WIDTH), lambda i: (i, 0)),
                  pl.BlockSpec((tm, MEM_WIDTH), lambda i: (i, 0)),
                  pl.BlockSpec((k, d), lambda i: (0, 0), pipeline_mode=pl.Buffered(1)),
                  pl.BlockSpec((tm, 3 * d), lambda i: (i, COL_GATE // (3 * d)))],
        out_specs=pl.BlockSpec((tm, d), lambda i: (i, 0)),
        out_shape=jax.ShapeDtypeStruct((m, d), BF16),
        compiler_params=_params("parallel"),
        name="merge",
    )(oa, ob, oc, w, z)


def _merge_cast_kernel(oa_ref, ob_ref, oc_ref, wa_ref, wb_ref, wc_ref, wo_ref, ga_ref, gb_ref, gc_ref,
                       o_ref, wp_out_ref, wo_out_ref):
    ka, kb = GMLP_WIDTH, GMLP_WIDTH + SWA_WIDTH
    wp_out_ref[0:ka, :] = wa_ref[...].astype(BF16)
    wp_out_ref[ka:kb, :] = wb_ref[...].astype(BF16)
    wp_out_ref[kb:, :] = wc_ref[...].astype(BF16)
    wo_out_ref[...] = wo_ref[...].astype(BF16)
    for s in range(oa_ref.shape[0] // MERGE_CAST_SUB):
        rows = _sub_rows(s, MERGE_CAST_SUB)
        acc = ga_ref[rows, :].astype(F32) * jnp.dot(oa_ref[rows, :], wp_out_ref[0:ka, :],
                                                    preferred_element_type=F32)
        acc = acc + gb_ref[rows, :].astype(F32) * jnp.dot(ob_ref[rows, :], wp_out_ref[ka:kb, :],
                                                          preferred_element_type=F32)
        acc = acc + gc_ref[rows, :].astype(F32) * jnp.dot(oc_ref[rows, :], wp_out_ref[kb:, :],
                                                          preferred_element_type=F32)
        o_ref[rows, :] = acc.astype(o_ref.dtype)


def merge_cast(z, oa, ob, oc, wa_all, wb_all, wc_all, wo_all, layer, tc=256):
    m = z.shape[0]
    d = wa_all.shape[-1]
    g0 = COL_GATE // tc
    gs = d // tc
    resident = lambda width: pl.BlockSpec((m, width), lambda c: (0, 0), pipeline_mode=pl.Buffered(1))
    weight = lambda k: pl.BlockSpec((None, k, tc), lambda c: (layer, 0, c))
    return pl.pallas_call(
        _merge_cast_kernel,
        grid=(d // tc,),
        in_specs=[resident(GMLP_WIDTH), resident(SWA_WIDTH), resident(MEM_WIDTH),
                  weight(GMLP_WIDTH), weight(SWA_WIDTH), weight(MEM_WIDTH), weight(d),
                  pl.BlockSpec((m, tc), lambda c: (0, g0 + c)),
                  pl.BlockSpec((m, tc), lambda c: (0, g0 + gs + c)),
                  pl.BlockSpec((m, tc), lambda c: (0, g0 + 2 * gs + c))],
        out_specs=[pl.BlockSpec((m, tc), lambda c: (0, c)),
                   pl.BlockSpec((GMLP_WIDTH + SWA_WIDTH + MEM_WIDTH, tc), lambda c: (0, c)),
                   pl.BlockSpec((d, tc), lambda c: (0, c))],
        out_shape=[jax.ShapeDtypeStruct((m, d), BF16),
                   jax.ShapeDtypeStruct((GMLP_WIDTH + SWA_WIDTH + MEM_WIDTH, d), BF16),
                   jax.ShapeDtypeStruct((d, d), BF16)],
        compiler_params=_params("parallel"),
        name="merge_cast",
    )(oa, ob, oc, wa_all, wb_all, wc_all, wo_all, z, z, z)


def _out_proj_kernel(mg_ref, w_ref, x_ref, gpost_ref, gnext_ref, xo_ref, ho_ref):
    def mix(s):
        return jnp.dot(mg_ref[_sub_rows(s, OUT_PROJ_SUB), :], w_ref[...], preferred_element_type=F32)

    def finish(s, mixed):
        rows = _sub_rows(s, OUT_PROJ_SUB)
        xn = x_ref[rows, :] + _rms(mixed, gpost_ref[...])
        xo_ref[rows, :] = xn
        ho_ref[rows, :] = _rms(xn, gnext_ref[...]).astype(ho_ref.dtype)

    _skewed(mg_ref.shape[0] // OUT_PROJ_SUB, mix, finish)


def out_proj(mg, w, x, gpost_all, gnext_all, layer, tm=512):
    m, d = x.shape
    return pl.pallas_call(
        _out_proj_kernel,
        grid=(m // tm,),
        in_specs=[pl.BlockSpec((tm, d), lambda i: (i, 0)),
                  pl.BlockSpec((d, d), lambda i: (0, 0), pipeline_mode=pl.Buffered(1)),
                  pl.BlockSpec((tm, d), lambda i: (i, 0)),
                  pl.BlockSpec((None, 1, d), lambda i: (layer, 0, 0)),
                  pl.BlockSpec((None, 1, d), lambda i: (layer, 0, 0))],
        out_specs=[pl.BlockSpec((tm, d), lambda i: (i, 0)),
                   pl.BlockSpec((tm, d), lambda i: (i, 0))],
        out_shape=[jax.ShapeDtypeStruct((m, d), F32), jax.ShapeDtypeStruct((m, d), BF16)],
        compiler_params=_params("parallel"),
        name="out_proj",
    )(mg, w, x, gpost_all, gnext_all)


def _ffn_kernel(h_ref, wu_ref, wd_ref, x_ref, gpost_ref, *rest, emit_next):
    if emit_next:
        gnext_ref, xo_ref, ho_ref = rest
    else:
        (xo_ref,) = rest
    j = pl.program_id(1)

    @pl.when(j == 0)
    def _():
        xo_ref[...] = jnp.zeros_like(xo_ref)

    def partial_sum(rows):
        a = jnp.dot(h_ref[rows, :], wu_ref[...], preferred_element_type=F32)
        a = jnp.square(jnp.maximum(a, 0.0)).astype(BF16)
        return jnp.dot(a, wd_ref[...], preferred_element_type=F32)

    last = pl.num_programs(1) - 1

    @pl.when(j < last)
    def _():
        xo_ref[...] += partial_sum(slice(None))

    @pl.when(j == last)
    def _():
        def finish(s, part):
            rows = _sub_rows(s, FFN_SUB)
            xn = x_ref[rows, :] + _rms(xo_ref[rows, :] + part, gpost_ref[...])
            xo_ref[rows, :] = xn
            if emit_next:
                ho_ref[rows, :] = _rms(xn, gnext_ref[...]).astype(ho_ref.dtype)

        _skewed(h_ref.shape[0] // FFN_SUB, lambda s: partial_sum(_sub_rows(s, FFN_SUB)), finish)


def ffn(h, wu, wd, x, gpost_all, gnext_all, layer, next_layer, tm=512, th=1024):
    m, d = x.shape
    hidden = wu.shape[-1]
    emit_next = next_layer is not None
    in_specs = [pl.BlockSpec((tm, d), lambda i, j: (i, 0)),
                pl.BlockSpec((d, th), lambda i, j: (0, j)),
                pl.BlockSpec((th, d), lambda i, j: (j, 0)),
                pl.BlockSpec((tm, d), lambda i, j: (i, 0)),
                pl.BlockSpec((None, 1, d), lambda i, j: (layer, 0, 0))]
    args = [h, wu, wd, x, gpost_all]
    out_specs = [pl.BlockSpec((tm, d), lambda i, j: (i, 0))]
    out_shape = [jax.ShapeDtypeStruct((m, d), F32)]
    if emit_next:
        in_specs.append(pl.BlockSpec((None, 1, d), lambda i, j: (next_layer, 0, 0)))
        args.append(gnext_all)
        out_specs.append(pl.BlockSpec((tm, d), lambda i, j: (i, 0)))
        out_shape.append(jax.ShapeDtypeStruct((m, d), BF16))
    res = pl.pallas_call(
        functools.partial(_ffn_kernel, emit_next=emit_next),
        grid=(m // tm, hidden // th),
        in_specs=in_specs,
        out_specs=out_specs,
        out_shape=out_shape,
        compiler_params=_params("parallel", "arbitrary"),
        name="ffn",
    )(*args)
    return (res[0], res[1]) if emit_next else (res[0], None)


def _ffn_up_cast_kernel(h_ref, wu_ref, a_ref, wub_ref):
    wub_ref[...] = wu_ref[...].astype(BF16)

    def up(s):
        return jnp.dot(h_ref[_sub_rows(s, FFN_CAST_SUB), :], wub_ref[...], preferred_element_type=F32)

    def finish(s, a):
        a_ref[_sub_rows(s, FFN_CAST_SUB), :] = jnp.square(jnp.maximum(a, 0.0)).astype(a_ref.dtype)

    _skewed(h_ref.shape[0] // FFN_CAST_SUB, up, finish)


def ffn_up_cast(h, wu_all, layer, th=1024):
    m, d = h.shape
    hidden = wu_all.shape[-1]
    return pl.pallas_call(
        _ffn_up_cast_kernel,
        grid=(hidden // th,),
        in_specs=[pl.BlockSpec((m, d), lambda j: (0, 0), pipeline_mode=pl.Buffered(1)),
                  pl.BlockSpec((None, d, th), lambda j: (layer, 0, j))],
        out_specs=[pl.BlockSpec((m, th), lambda j: (0, j)),
                   pl.BlockSpec((d, th), lambda j: (0, j))],
        out_shape=[jax.ShapeDtypeStruct((m, hidden), BF16), jax.ShapeDtypeStruct((d, hidden), BF16)],
        compiler_params=_params("parallel"),
        name="ffn_up_cast",
    )(h, wu_all)


def _ffn_down_cast_kernel(a_ref, wd_ref, acc_ref, wdb_ref):
    @pl.when(pl.program_id(0) == 0)
    def _():
        acc_ref[...] = jnp.zeros_like(acc_ref)

    wdb_ref[...] = wd_ref[...].astype(BF16)
    for s in range(a_ref.shape[0] // FFN_CAST_SUB):
        rows = _sub_rows(s, FFN_CAST_SUB)
        acc_ref[rows, :] += jnp.dot(a_ref[rows, :], wdb_ref[...], preferred_element_type=F32)


def ffn_down_cast(a, wd_all, layer, tk=1024):
    m, hidden = a.shape
    d = wd_all.shape[-1]
    return pl.pallas_call(
        _ffn_down_cast_kernel,
        grid=(hidden // tk,),
        in_specs=[pl.BlockSpec((m, tk), lambda k: (0, k)),
                  pl.BlockSpec((None, tk, d), lambda k: (layer, k, 0))],
        out_specs=[pl.BlockSpec((m, d), lambda k: (0, 0), pipeline_mode=pl.Buffered(1)),
                   pl.BlockSpec((tk, d), lambda k: (k, 0))],
        out_shape=[jax.ShapeDtypeStruct((m, d), F32), jax.ShapeDtypeStruct((hidden, d), BF16)],
        compiler_params=_params("arbitrary"),
        name="ffn_down_cast",
    )(a, wd_all)


def _ffn_residual_kernel(acc_ref, x_ref, gpost_ref, *rest):
    xn = x_ref[...] + _rms(acc_ref[...], gpost_ref[...])
    if len(rest) == 3:
        gnext_ref, xo_ref, ho_ref = rest
        ho_ref[...] = _rms(xn, gnext_ref[...]).astype(ho_ref.dtype)
    else:
        (xo_ref,) = rest
    xo_ref[...] = xn


def ffn_residual(acc, x, gpost_all, gnext_all, layer, next_layer, tm=512):
    m, d = x.shape
    emit_next = next_layer is not None
    row_spec = pl.BlockSpec((tm, d), lambda i: (i, 0))
    in_specs = [row_spec, row_spec, pl.BlockSpec((None, 1, d), lambda i: (layer, 0, 0))]
    args = [acc, x, gpost_all]
    out_specs, out_shape = [row_spec], [jax.ShapeDtypeStruct((m, d), F32)]
    if emit_next:
        in_specs.append(pl.BlockSpec((None, 1, d), lambda i: (next_layer, 0, 0)))
        args.append(gnext_all)
        out_specs.append(row_spec)
        out_shape.append(jax.ShapeDtypeStruct((m, d), BF16))
    res = pl.pallas_call(
        _ffn_residual_kernel,
        grid=(m // tm,),
        in_specs=in_specs,
        out_specs=out_specs,
        out_shape=out_shape,
        compiler_params=_params("parallel"),
        name="ffn_residual",
    )(*args)
    return (res[0], res[1]) if emit_next else (res[0], None)


def kernel(x_prompt, x_sample, cache_swa_k, cache_swa_v, cache_mem_k, cache_mem_v, mem_prompt,
           w_in, ln_v_g, ln_v_b, w_s, b_s, sinks, mem_norm, w_mem_k, w_mem_v,
           w_pa, w_pb, w_pc, w_o, norm_mix_pre, norm_mix_post, norm_ffn_pre, norm_ffn_post,
           w_up, w_down):
    batch, seq, d = x_prompt.shape
    dec_batch, dec_seq, _ = x_sample.shape
    depth = w_in.shape[0]
    mem_tokens = mem_prompt.shape[1]

    row = lambda g: g.reshape(depth, 1, g.shape[-1])
    g_mix_pre, g_mix_post = row(norm_mix_pre), row(norm_mix_post)
    g_ffn_pre, g_ffn_post = row(norm_ffn_pre), row(norm_ffn_post)
    g_mem, ln_g, ln_b = row(mem_norm), row(ln_v_g), row(ln_v_b)
    bias_p = jnp.repeat(jnp.swapaxes(b_s[:, :, :GMLP_CHUNK], 1, 2), GMLP_GROUP_DIM, axis=2)
    bias_s = jnp.repeat(jnp.swapaxes(b_s[:, :, :dec_seq], 1, 2), GMLP_GROUP_DIM, axis=2)
    proj_rows = 1024
    tab_p = rope_table(jnp.arange(seq, dtype=jnp.int32))
    tab_s = jnp.tile(rope_table(PAST_LEN + jnp.arange(dec_seq, dtype=jnp.int32)), (1, proj_rows // dec_seq, 1))
    cache_k = cache_swa_k.reshape(depth, dec_batch, WINDOW, SWA_KV_WIDTH)
    cache_v = cache_swa_v.reshape(depth, dec_batch, WINDOW, SWA_KV_WIDTH)

    mk_p, mv_p = memory_kv(mem_prompt, g_mem, w_mem_k, w_mem_v)

    xp = x_prompt.reshape(batch * seq, d)
    xs = x_sample.reshape(dec_batch * dec_seq, d)
    hp = rmsnorm_cast(xp, g_mix_pre, 0)
    hs = rmsnorm_cast(xs, g_mix_pre, 0)

    k_p, v_p, k_s, v_s, gv_s = [], [], [], [], []
    for l in range(depth):
        nxt = l + 1 if l + 1 < depth else None

        z = proj_in(hs, w_in, tab_s, l, tm=proj_rows)
        oa, v_rows = gmlp(z, ln_g, ln_b, w_s, bias_s, l, dec_seq, True)
        ob = swa_sample(z, cache_k, cache_v, sinks, l, dec_batch, dec_seq)
        oc = mem_attn_cache(z, cache_mem_k, cache_mem_v, l, dec_batch, dec_seq)
        mg, w_p_b, w_o_b = merge_cast(z, oa, ob, oc, w_pa, w_pb, w_pc, w_o, l)
        xs, h2 = out_proj(mg, w_o_b, xs, g_mix_post, g_ffn_pre, l)
        act, w_up_b = ffn_up_cast(h2, w_up, l)
        acc, w_down_b = ffn_down_cast(act, w_down, l)
        xs, hs = ffn_residual(acc, xs, g_ffn_post, g_mix_pre, l, nxt)
        k_s.append(z[:, COL_K:COL_VV].astype(F32).reshape(dec_batch, dec_seq, SWA_KV_HEADS, SWA_HEAD_DIM))
        v_s.append(z[:, COL_VV:COL_CQ].astype(F32).reshape(dec_batch, dec_seq, SWA_KV_HEADS, SWA_HEAD_DIM))
        gv_s.append(v_rows.reshape(dec_batch, dec_seq, GMLP_WIDTH))

        z = proj_in(hp, w_in, tab_p, l, tm=proj_rows)
        oa, _ = gmlp(z, ln_g, ln_b, w_s, bias_p, l, GMLP_CHUNK, False)
        ob = swa_prompt(z, sinks, l, batch, seq)
        oc = mem_attn(z, mk_p, mv_p, l, batch, seq, 1024)
        mg = merge(z, oa, ob, oc, w_p_b)
        xp, h2 = out_proj(mg, w_o_b, xp, g_mix_post, g_ffn_pre, l)
        xp, hp = ffn(h2, w_up_b, w_down_b, xp, g_ffn_post, g_mix_pre, l, nxt)
        kv_tail = z.reshape(batch, seq, -1)[:, -WINDOW:, COL_K:COL_CQ].astype(F32)
        k_p.append(kv_tail[..., :SWA_KV_WIDTH].reshape(batch, WINDOW, SWA_KV_HEADS, SWA_HEAD_DIM))
        v_p.append(kv_tail[..., SWA_KV_WIDTH:].reshape(batch, WINDOW, SWA_KV_HEADS, SWA_HEAD_DIM))

    mem_shape = (depth, batch, mem_tokens, MEM_HEADS, MEM_HEAD_DIM)
    return (xp.reshape(batch, seq, d), xs.reshape(dec_batch, dec_seq, d),
            jnp.stack(k_p), jnp.stack(v_p), mk_p.reshape(mem_shape), mv_p.reshape(mem_shape),
            jnp.stack(k_s), jnp.stack(v_s), jnp.stack(gv_s))
```
